```python
import jax, jax.numpy as jnp
from jax import lax
import numpy as np

D_MODEL = 1024
BATCH = 2
SEQ = 8192
DEPTH = 1
DEC_BATCH = 8
DEC_SEQ = 16
PAST_LEN = 1024

CHUNK = 64
N_HEADS = 16
N_KV_HEADS = 2
HEAD_DIM = 64
GROUP = N_HEADS // N_KV_HEADS
ROT_DIM = HEAD_DIM // 4
ROPE_THETA = 500000.0
WINDOW = 128
WINDOW_CHUNKS = WINDOW // CHUNK
CONV_CH = D_MODEL
CONV_WIDTH = 31
N_EXPERTS = 32
TOP_K = 4
D_FF = D_MODEL
SWIGLU_LIMIT = 7.0
SWIGLU_ALPHA = 1.702
MOE_BLOCK = 128
NORM_EPS = 1e-5
NEG_INF = -1e30
Q_W = N_HEADS * HEAD_DIM
KV_W = N_KV_HEADS * HEAD_DIM
IN_COLS = Q_W + 2 * KV_W + 2 * CONV_CH + 2 * D_MODEL

kernel_name = 'chunk_causal_swa_conformer_moe_step'


def rms_norm(x, g):
    xf = x.astype(jnp.float32)
    y = xf * lax.rsqrt(jnp.mean(xf * xf, axis=-1, keepdims=True) + NORM_EPS)
    return (y * g.astype(jnp.float32)).astype(x.dtype)


def layer_norm(x, g, b):
    xf = x.astype(jnp.float32)
    mu = jnp.mean(xf, axis=-1, keepdims=True)
    var = jnp.mean(jnp.square(xf - mu), axis=-1, keepdims=True)
    y = (xf - mu) * lax.rsqrt(var + NORM_EPS)
    return (y * g.astype(jnp.float32) + b.astype(jnp.float32)).astype(x.dtype)


def apply_rope(x, pos):
    half = ROT_DIM // 2
    inv = ROPE_THETA ** (-jnp.arange(half, dtype=jnp.float32) * 2.0 / ROT_DIM)
    ang = pos[:, None] * inv[None, :]
    cos = jnp.cos(ang)[None, :, None, :]
    sin = jnp.sin(ang)[None, :, None, :]
    xf = x.astype(jnp.float32)
    x1 = xf[..., :half]
    x2 = xf[..., half:ROT_DIM]
    out = jnp.concatenate([x1 * cos - x2 * sin, x2 * cos + x1 * sin, xf[..., ROT_DIM:]], axis=-1)
    return out.astype(x.dtype)


def sink_attention(q, k, v, mask, sinks):
    s = jnp.einsum('bnqkgd,bnskd->bnkgqs', q, k).astype(jnp.float32)
    s = jnp.where(mask[None, :, None, None], s, NEG_INF)
    sink = sinks.astype(jnp.float32).reshape(1, 1, N_KV_HEADS, GROUP, 1, 1)
    m = jnp.maximum(jnp.max(s, axis=-1, keepdims=True), sink)
    p = jnp.exp(s - m)
    denom = jnp.sum(p, axis=-1, keepdims=True) + jnp.exp(sink - m)
    p = (p / denom).astype(v.dtype)
    return jnp.einsum('bnkgqs,bnskd->bnqkgd', p, v)


def banded_attention(q, k, v, sinks):
    B, T = q.shape[0], q.shape[1]
    nc = T // CHUNK
    span = (WINDOW_CHUNKS + 1) * CHUNK
    pad = WINDOW_CHUNKS * CHUNK
    qb = q.reshape(B, nc, CHUNK, N_KV_HEADS, GROUP, HEAD_DIM)
    kp = jnp.pad(k, ((0, 0), (pad, 0), (0, 0), (0, 0)))
    vp = jnp.pad(v, ((0, 0), (pad, 0), (0, 0), (0, 0)))
    kb = jnp.concatenate([kp[:, i * CHUNK:i * CHUNK + T].reshape(B, nc, CHUNK, N_KV_HEADS, HEAD_DIM)
                          for i in range(WINDOW_CHUNKS + 1)], axis=2)
    vb = jnp.concatenate([vp[:, i * CHUNK:i * CHUNK + T].reshape(B, nc, CHUNK, N_KV_HEADS, HEAD_DIM)
                          for i in range(WINDOW_CHUNKS + 1)], axis=2)
    kpos = jnp.arange(nc)[:, None] * CHUNK - pad + jnp.arange(span)[None, :]
    mask = jnp.broadcast_to((kpos >= 0)[:, None, :], (nc, CHUNK, span))
    out = sink_attention(qb, kb, vb, mask, sinks)
    return out.reshape(B, T, Q_W)


def cached_attention(q, k, v, k_hist, v_hist, sinks, pos0):
    B, T = q.shape[0], q.shape[1]
    W = k_hist.shape[1]
    kc = jnp.concatenate([k_hist, k], axis=1)
    vc = jnp.concatenate([v_hist, v], axis=1)
    qpos = pos0 + jnp.arange(T)
    kpos = pos0 - W + jnp.arange(W + T)
    qch = qpos // CHUNK
    kch = kpos // CHUNK
    mask = (kch[None, :] >= qch[:, None] - WINDOW_CHUNKS) & (kch[None, :] <= qch[:, None]) & (kpos[None, :] >= 0)
    out = sink_attention(q.reshape(B, 1, T, N_KV_HEADS, GROUP, HEAD_DIM), kc[:, None], vc[:, None], mask[None], sinks)
    return out.reshape(B, T, Q_W), kc[:, -W:], vc[:, -W:]


def conv_module(u, hist, conv_dw_w, conv_dw_b, conv_ln_g, conv_ln_b, w_pw, b_pw):
    full = jnp.concatenate([hist, u], axis=1)
    y = lax.conv_general_dilated(full, conv_dw_w[:, None, :], window_strides=(1,), padding='VALID',
                                 dimension_numbers=('NWC', 'WIO', 'NWC'),
                                 feature_group_count=CONV_CH) + conv_dw_b
    y = jax.nn.silu(layer_norm(y, conv_ln_g, conv_ln_b))
    return y @ w_pw + b_pw, full[:, -(CONV_WIDTH - 1):]


def moe(xn, w_router, b_router, w_gu, b_gu, w_down, b_down):
    N = xn.shape[0]
    nk = N * TOP_K
    logits = (xn @ w_router).astype(jnp.float32) + b_router.astype(jnp.float32)
    top_logit, top_e = lax.top_k(logits, TOP_K)
    gate = jax.nn.softmax(top_logit, axis=-1).astype(xn.dtype)
    flat_e = top_e.reshape(nk)
    order = jnp.argsort(flat_e)
    sorted_e = flat_e[order]
    src_tok = order // TOP_K
    counts = jnp.bincount(flat_e, length=N_EXPERTS)
    padded = (counts + MOE_BLOCK - 1) // MOE_BLOCK * MOE_BLOCK
    pad_end = jnp.cumsum(padded)
    pad_start = pad_end - padded
    start = jnp.cumsum(counts) - counts
    dest = pad_start[sorted_e] + (jnp.arange(nk) - start[sorted_e])
    n_blocks = -(-nk // MOE_BLOCK) + N_EXPERTS
    slot_tok = jnp.full((n_blocks * MOE_BLOCK,), N, dtype=jnp.int32).at[dest].set(src_tok.astype(jnp.int32))
    x_slots = jnp.concatenate([xn, jnp.zeros((1, xn.shape[1]), xn.dtype)], axis=0)[slot_tok]
    x_slots = x_slots.reshape(n_blocks, MOE_BLOCK, xn.shape[1])
    block_e = jnp.minimum(jnp.searchsorted(pad_end, jnp.arange(n_blocks) * MOE_BLOCK, side='right'), N_EXPERTS - 1)

    def expert_block(args):
        xb, e = args
        gu = xb @ w_gu[e] + b_gu[e]
        g = jnp.minimum(gu[:, :D_FF], SWIGLU_LIMIT)
        up = jnp.clip(gu[:, D_FF:], -SWIGLU_LIMIT, SWIGLU_LIMIT)
        h = (up + 1.0) * (g * jax.nn.sigmoid(SWIGLU_ALPHA * g))
        return h @ w_down[e] + b_down[e]

    y_slots = lax.map(expert_block, (x_slots, block_e)).reshape(n_blocks * MOE_BLOCK, -1)
    y_assign = y_slots[dest] * gate.reshape(nk)[order][:, None]
    return jax.ops.segment_sum(y_assign, src_tok, num_segments=N)


def layer(x, k_hist, v_hist, conv_hist, pos0, norm_mix, w_in, attn_sinks, conv_dw_w, conv_dw_b,
          conv_ln_g, conv_ln_b, w_pw, b_pw, w_out, norm_ffn, w_router, b_router, w_gu, b_gu, w_down, b_down):
    B, T, _ = x.shape
    xn = rms_norm(x, norm_mix)
    proj = xn @ w_in
    cuts = [int(c) for c in np.cumsum([Q_W, KV_W, KV_W, CONV_CH, CONV_CH, D_MODEL])]
    q, k, v, glu_a, glu_b, gate_attn, gate_conv = jnp.split(proj, cuts, axis=-1)
    pos = pos0 + jnp.arange(T, dtype=jnp.float32)
    q = apply_rope(q.reshape(B, T, N_HEADS, HEAD_DIM), pos) * (HEAD_DIM ** -0.5)
    k = apply_rope(k.reshape(B, T, N_KV_HEADS, HEAD_DIM), pos)
    v = v.reshape(B, T, N_KV_HEADS, HEAD_DIM)
    if k_hist is None:
        attn = banded_attention(q, k, v, attn_sinks)
        new_k, new_v = k[:, -WINDOW:], v[:, -WINDOW:]
        conv_hist = jnp.zeros((B, CONV_WIDTH - 1, CONV_CH), x.dtype)
    else:
        attn, new_k, new_v = cached_attention(q, k, v, k_hist, v_hist, attn_sinks, pos0)
    u = glu_a * jax.nn.sigmoid(glu_b)
    conv_out, new_conv = conv_module(u, conv_hist, conv_dw_w, conv_dw_b, conv_ln_g, conv_ln_b, w_pw, b_pw)
    h = jax.nn.sigmoid(gate_attn) * attn + jax.nn.sigmoid(gate_conv) * conv_out
    x = x + h @ w_out
    xf = rms_norm(x, norm_ffn).reshape(B * T, D_MODEL)
    x = x + moe(xf, w_router, b_router, w_gu, b_gu, w_down, b_down).reshape(B, T, D_MODEL)
    return x, new_k, new_v, new_conv


def setup_inputs(seed: int = 0) -> dict:
    key = jax.random.key(seed)
    ks = jax.random.split(key, 26)
    f32 = jnp.float32
    L = DEPTH
    W = min(WINDOW, PAST_LEN)

    def nrm(k, shape, scale):
        return jax.random.normal(k, shape, f32) * scale

    return {
        'x_prompt': nrm(ks[0], (BATCH, SEQ, D_MODEL), 1.0),
        'x_sample': nrm(ks[1], (DEC_BATCH, DEC_SEQ, D_MODEL), 1.0),
        'cache_k': nrm(ks[2], (L, DEC_BATCH, W, N_KV_HEADS, HEAD_DIM), 1.0),
        'cache_v': nrm(ks[3], (L, DEC_BATCH, W, N_KV_HEADS, HEAD_DIM), 1.0),
        'cache_conv': nrm(ks[4], (L, DEC_BATCH, CONV_WIDTH - 1, CONV_CH), 0.5),
        'norm_mix': 1.0 + nrm(ks[5], (L, D_MODEL), 0.02),
        'w_in': nrm(ks[6], (L, D_MODEL, IN_COLS), D_MODEL ** -0.5),
        'attn_sinks': nrm(ks[7], (L, N_HEADS), 0.5),
        'conv_dw_w': nrm(ks[8], (L, CONV_WIDTH, CONV_CH), CONV_WIDTH ** -0.5),
        'conv_dw_b': nrm(ks[9], (L, CONV_CH), 0.02),
        'conv_ln_g': 1.0 + nrm(ks[10], (L, CONV_CH), 0.02),
        'conv_ln_b': nrm(ks[11], (L, CONV_CH), 0.02),
        'w_pw': nrm(ks[12], (L, CONV_CH, D_MODEL), CONV_CH ** -0.5),
        'b_pw': nrm(ks[13], (L, D_MODEL), 0.02),
        'w_out': nrm(ks[14], (L, D_MODEL, D_MODEL), D_MODEL ** -0.5),
        'norm_ffn': 1.0 + nrm(ks[15], (L, D_MODEL), 0.02),
        'w_router': nrm(ks[16], (L, D_MODEL, N_EXPERTS), D_MODEL ** -0.5),
        'b_router': nrm(ks[17], (L, N_EXPERTS), 0.01),
        'w_gu': nrm(ks[18], (L, N_EXPERTS, D_MODEL, 2 * D_FF), D_MODEL ** -0.5),
        'b_gu': nrm(ks[19], (L, N_EXPERTS, 2 * D_FF), 0.02),
        'w_down': nrm(ks[20], (L, N_EXPERTS, D_FF, D_MODEL), D_FF ** -0.5),
        'b_down': nrm(ks[21], (L, N_EXPERTS, D_MODEL), 0.02),
        'norm_final': 1.0 + nrm(ks[22], (D_MODEL,), 0.02),
    }


def reference(x_prompt, x_sample, cache_k, cache_v, cache_conv, norm_mix, w_in, attn_sinks, conv_dw_w,
              conv_dw_b, conv_ln_g, conv_ln_b, w_pw, b_pw, w_out, norm_ffn, w_router, b_router, w_gu, b_gu,
              w_down, b_down, norm_final):
    xp, xs = x_prompt, x_sample
    kp_l, vp_l, cp_l, ks_l, vs_l, cs_l = [], [], [], [], [], []
    for l in range(DEPTH):
        wl = (norm_mix[l], w_in[l], attn_sinks[l], conv_dw_w[l], conv_dw_b[l], conv_ln_g[l], conv_ln_b[l],
              w_pw[l], b_pw[l], w_out[l], norm_ffn[l], w_router[l], b_router[l], w_gu[l], b_gu[l],
              w_down[l], b_down[l])
        xp, kp, vp, cp = layer(xp, None, None, None, 0, *wl)
        xs, kn, vn, cn = layer(xs, cache_k[l], cache_v[l], cache_conv[l], PAST_LEN, *wl)
        kp_l.append(kp); vp_l.append(vp); cp_l.append(cp)
        ks_l.append(kn); vs_l.append(vn); cs_l.append(cn)
    y_prompt = rms_norm(xp, norm_final)
    y_sample = rms_norm(xs, norm_final)
    return (y_prompt, y_sample, jnp.stack(kp_l), jnp.stack(vp_l), jnp.stack(cp_l),
            jnp.stack(ks_l), jnp.stack(vs_l), jnp.stack(cs_l))
```

```python
import functools

import numpy as np
import jax
import jax.numpy as jnp
from jax import lax
from jax.experimental import pallas as pl
from jax.experimental.pallas import tpu as pltpu

f32 = jnp.float32
bf16 = jnp.bfloat16
i32 = jnp.int32

D_MODEL = 1024
PAST_LEN = 1024
CHUNK = 64
N_HEADS = 16
N_KV_HEADS = 2
HEAD_DIM = 64
GROUP = N_HEADS // N_KV_HEADS
ROT_DIM = HEAD_DIM // 4
ROPE_THETA = 500000.0
WINDOW = 128
WINDOW_CHUNKS = WINDOW // CHUNK
SPAN = (WINDOW_CHUNKS + 1) * CHUNK
CONV_CH = D_MODEL
CONV_WIDTH = 31
N_EXPERTS = 32
TOP_K = 4
D_FF = D_MODEL
SWIGLU_LIMIT = 7.0
SWIGLU_ALPHA = 1.702
NORM_EPS = 1e-5
NEG_INF = -1e30
Q_W = N_HEADS * HEAD_DIM
KV_W = N_KV_HEADS * HEAD_DIM
IN_COLS = Q_W + 2 * KV_W + 2 * CONV_CH + 2 * D_MODEL
Q_SCALE = HEAD_DIM ** -0.5

LANES = 128
SUBLANES = 8
VMEM_LIMIT = 56 * 1024 * 1024

HALO = 32
TM_PROJ = 512
TM_MIX = 256
ATTN_CHUNKS = 4
ROW_ALIGN = SUBLANES
R_BLK = 256
LOCAL_ROWS = -(-(TOP_K * TM_MIX + N_EXPERTS * (ROW_ALIGN - 1)) // LANES) * LANES
SEG_CHUNKS = tuple(2 ** p for p in range(int(np.log2(TM_MIX)), int(np.log2(ROW_ALIGN)) - 1, -1))
TAIL_CHUNKS = tuple(c for c in SEG_CHUNKS if c < R_BLK)


def _cparams(n_axes):
    return pltpu.CompilerParams(dimension_semantics=("arbitrary",) * n_axes,
                                vmem_limit_bytes=VMEM_LIMIT)


def _rms(x, g):
    return x * lax.rsqrt(jnp.mean(x * x, axis=-1, keepdims=True) + NORM_EPS) * g


def _inproj_kernel(x_ref, g_ref, w_ref, cos_ref, sa_ref, sb_ref,
                   q_ref, kv_ref, u_ref, ga_ref, gc_ref, ut_ref):
    xn = _rms(x_ref[...], g_ref[...]).astype(bf16)
    cos = cos_ref[...]
    sa = sa_ref[...]
    sb = sb_ref[...]

    def rope(t):
        return t * cos + pltpu.roll(t, LANES - ROT_DIM // 2, 1) * sa + pltpu.roll(t, ROT_DIM // 2, 1) * sb

    def proj(c0, n):
        return jnp.dot(xn, w_ref[:, c0:c0 + n], preferred_element_type=f32)

    q = proj(0, Q_W)
    for c in range(Q_W // LANES):
        q_ref[:, c * LANES:(c + 1) * LANES] = (rope(q[:, c * LANES:(c + 1) * LANES]) * Q_SCALE).astype(bf16)
    kv = proj(Q_W, 2 * KV_W)
    kv_ref[:, :KV_W] = rope(kv[:, :KV_W])
    kv_ref[:, KV_W:] = kv[:, KV_W:]
    c0 = Q_W + 2 * KV_W
    u = proj(c0, CONV_CH) * jax.nn.sigmoid(proj(c0 + CONV_CH, CONV_CH))
    u_ref[...] = u.astype(u_ref.dtype)
    ut_ref[...] = u[u.shape[0] - HALO:, :]
    c0 += 2 * CONV_CH
    ga_ref[...] = jax.nn.sigmoid(proj(c0, D_MODEL)).astype(bf16)
    gc_ref[...] = jax.nn.sigmoid(proj(c0 + D_MODEL, D_MODEL)).astype(bf16)


def _rope_tables(pos):
    half = ROT_DIM // 2
    inv = ROPE_THETA ** (-jnp.arange(half, dtype=f32) * 2.0 / ROT_DIM)
    ang = pos[:, None] * inv[None, :]
    cos, sin = jnp.cos(ang), jnp.sin(ang)
    t = pos.shape[0]
    ones = jnp.ones((t, HEAD_DIM - ROT_DIM), f32)
    zeros = jnp.zeros((t, HEAD_DIM - ROT_DIM), f32)
    zh = jnp.zeros((t, half), f32)
    c = jnp.concatenate([cos, cos, ones], axis=1)
    a = jnp.concatenate([-sin, zh, zeros], axis=1)
    b = jnp.concatenate([zh, sin, zeros], axis=1)
    rep = LANES // HEAD_DIM
    return jnp.tile(c, (1, rep)), jnp.tile(a, (1, rep)), jnp.tile(b, (1, rep))


def _in_proj(x2d, norm_g, w_bf, tables, tm, tiles_per_seq, u_dtype):
    n = x2d.shape[0]
    nt = n // tm
    row = lambda i: (i, 0)
    const = lambda i: (0, 0)
    tab = lambda i: (i % tiles_per_seq, 0)
    return pl.pallas_call(
        _inproj_kernel,
        grid=(nt,),
        in_specs=[pl.BlockSpec((tm, D_MODEL), row),
                  pl.BlockSpec((1, D_MODEL), const),
                  pl.BlockSpec((D_MODEL, IN_COLS), const, pipeline_mode=pl.Buffered(1)),
                  pl.BlockSpec((tm, LANES), tab),
                  pl.BlockSpec((tm, LANES), tab),
                  pl.BlockSpec((tm, LANES), tab)],
        out_specs=[pl.BlockSpec((tm, Q_W), row),
                   pl.BlockSpec((tm, 2 * KV_W), row),
                   pl.BlockSpec((tm, CONV_CH), row),
                   pl.BlockSpec((tm, D_MODEL), row),
                   pl.BlockSpec((tm, D_MODEL), row),
                   pl.BlockSpec((None, HALO, CONV_CH), lambda i: (i, 0, 0))],
        out_shape=[jax.ShapeDtypeStruct((n, Q_W), bf16),
                   jax.ShapeDtypeStruct((n, 2 * KV_W), f32),
                   jax.ShapeDtypeStruct((n, CONV_CH), u_dtype),
                   jax.ShapeDtypeStruct((n, D_MODEL), bf16),
                   jax.ShapeDtypeStruct((n, D_MODEL), bf16),
                   jax.ShapeDtypeStruct((nt, HALO, CONV_CH), f32)],
        compiler_params=_cparams(1),
        name="in_proj",
    )(x2d, norm_g, w_bf, *tables)


def _attn_group(qg, kg, vg, mask, sink):
    s = lax.dot_general(qg, kg, (((1,), (1,)), ((), ())), preferred_element_type=f32)
    s = jnp.where(mask, s, NEG_INF)
    m = jnp.maximum(jnp.max(s, axis=-1, keepdims=True), sink)
    p = jnp.exp(s - m)
    denom = jnp.sum(p, axis=-1, keepdims=True) + jnp.exp(sink - m)
    o = jnp.dot(p.astype(bf16), vg, preferred_element_type=f32)
    return o / denom


def _heads_to_rows(q, g):
    base = g * GROUP * HEAD_DIM
    return jnp.concatenate([q[:, base + h * HEAD_DIM: base + (h + 1) * HEAD_DIM] for h in range(GROUP)], axis=0)


def _rows_to_heads(o, tq):
    return [o[h * tq:(h + 1) * tq, :] for h in range(GROUP)]


def _attn_prompt_kernel(q_ref, kv_ref, sink_ref, o_ref):
    j = pl.program_id(1)
    for c in range(ATTN_CHUNKS):
        n = j * ATTN_CHUNKS + c
        first = jnp.maximum(n - WINDOW_CHUNKS, 0)
        win = kv_ref[pl.ds(pl.multiple_of(first * CHUNK, CHUNK), SPAN), :]
        kchunk = first + lax.broadcasted_iota(i32, (1, SPAN), 1) // CHUNK
        mask = kchunk <= n
        q = q_ref[c * CHUNK:(c + 1) * CHUNK, :]
        outs = []
        for g in range(N_KV_HEADS):
            kg = win[:, g * HEAD_DIM:(g + 1) * HEAD_DIM].astype(bf16)
            vg = win[:, KV_W + g * HEAD_DIM: KV_W + (g + 1) * HEAD_DIM].astype(bf16)
            o = _attn_group(_heads_to_rows(q, g), kg, vg, mask, sink_ref[g])
            outs += _rows_to_heads(o, CHUNK)
        o_ref[c * CHUNK:(c + 1) * CHUNK, :] = jnp.concatenate(outs, axis=1).astype(bf16)


def _attn_prompt(q, kv, sink_rows, batch, seq):
    qrows = ATTN_CHUNKS * CHUNK
    steps = seq // qrows
    return pl.pallas_call(
        _attn_prompt_kernel,
        grid=(batch, steps),
        in_specs=[pl.BlockSpec((qrows, Q_W), lambda b, j: (b * steps + j, 0)),
                  pl.BlockSpec((seq, 2 * KV_W), lambda b, j: (b, 0)),
                  pl.BlockSpec((N_KV_HEADS, GROUP * CHUNK, 1), lambda b, j: (0, 0, 0))],
        out_specs=pl.BlockSpec((qrows, Q_W), lambda b, j: (b * steps + j, 0)),
        out_shape=jax.ShapeDtypeStruct((batch * seq, Q_W), bf16),
        compiler_params=_cparams(2),
        name="attn_prompt",
    )(q, kv, sink_rows)


def _attn_sample_kernel(q_ref, kvn_ref, ck_ref, cv_ref, mask_ref, sink_ref, o_ref):
    tq = q_ref.shape[0]
    q = q_ref[...]
    kvn = kvn_ref[...]
    ck = ck_ref[...]
    cv = cv_ref[...]
    mask = mask_ref[...] > 0.5
    outs = []
    for g in range(N_KV_HEADS):
        sl = slice(g * HEAD_DIM, (g + 1) * HEAD_DIM)
        kg = jnp.concatenate([ck[:, sl], kvn[:, sl]], axis=0).astype(bf16)
        vg = jnp.concatenate([cv[:, sl], kvn[:, KV_W + g * HEAD_DIM: KV_W + (g + 1) * HEAD_DIM]], axis=0).astype(bf16)
        o = _attn_group(_heads_to_rows(q, g), kg, vg, mask, sink_ref[g])
        outs += _rows_to_heads(o, tq)
    o_ref[...] = jnp.concatenate(outs, axis=1).astype(bf16)


def _attn_sample(q, kv_new, cache_k, cache_v, mask_rows, sink_rows, batch, tq):
    w = cache_k.shape[1]
    return pl.pallas_call(
        _attn_sample_kernel,
        grid=(batch,),
        in_specs=[pl.BlockSpec((tq, Q_W), lambda b: (b, 0)),
                  pl.BlockSpec((tq, 2 * KV_W), lambda b: (b, 0)),
                  pl.BlockSpec((None, w, KV_W), lambda b: (b, 0, 0)),
                  pl.BlockSpec((None, w, KV_W), lambda b: (b, 0, 0)),
                  pl.BlockSpec((GROUP * tq, w + tq), lambda b: (0, 0)),
                  pl.BlockSpec((N_KV_HEADS, GROUP * tq, 1), lambda b: (0, 0, 0))],
        out_specs=pl.BlockSpec((tq, Q_W), lambda b: (b, 0)),
        out_shape=jax.ShapeDtypeStruct((batch * tq, Q_W), bf16),
        compiler_params=_cparams(1),
        name="attn_sample",
    )(q, kv_new, cache_k, cache_v, mask_rows, sink_rows)


def _dwconv(s_ref, row0, n_rows, w_ref, b_ref, y_ref, yrow0, rc):
    lead = HALO - (CONV_WIDTH - 1)
    offsets = range(lead, lead + CONV_WIDTH)

    def lane_body(lc, carry):
        c0 = pl.multiple_of(lc * LANES, LANES)
        cols = pl.ds(c0, LANES)
        bias = b_ref[:, cols]
        for t0 in range(0, n_rows, rc):
            acc = jnp.broadcast_to(bias, (rc, LANES))
            for r in range(SUBLANES):
                offs = [o for o in offsets if o % SUBLANES == r]
                if not offs:
                    continue
                amax = max(offs) // SUBLANES
                z = s_ref[pl.ds(row0 + t0 + r, rc + SUBLANES * amax), cols]
                for o in offs:
                    a = o // SUBLANES
                    acc = acc + z[SUBLANES * a: SUBLANES * a + rc, :] * w_ref[o - lead: o - lead + 1, cols]
            y_ref[pl.ds(yrow0 + t0, rc), cols] = acc
        return carry

    lax.fori_loop(0, CONV_CH // LANES, lane_body, 0)


def _mix_tail(y, attn, ga, gc, x, valid, lg_ref, lb_ref, wpw_ref, bpw_ref, wout_ref, nf_ref, wr_ref, br_ref):
    tm = y.shape[0]
    mu = jnp.mean(y, axis=-1, keepdims=True)
    d = y - mu
    var = jnp.mean(d * d, axis=-1, keepdims=True)
    yn = d * lax.rsqrt(var + NORM_EPS) * lg_ref[...] + lb_ref[...]
    act = (yn * jax.nn.sigmoid(yn)).astype(bf16)
    conv_out = jnp.dot(act, wpw_ref[...], preferred_element_type=f32) + bpw_ref[...]
    h = (ga.astype(f32) * attn.astype(f32) + gc.astype(f32) * conv_out).astype(bf16)
    x1 = x + jnp.dot(h, wout_ref[...], preferred_element_type=f32)
    xf = _rms(x1, nf_ref[...]).astype(bf16)
    logits = lax.dot_general(wr_ref[...], xf, (((1,), (1,)), ((), ())), preferred_element_type=f32) + br_ref[...]
    eidx = lax.broadcasted_iota(i32, (N_EXPERTS, tm), 0)
    routed = jnp.zeros((N_EXPERTS, tm), jnp.bool_)
    top_l, top_e = [], []
    l = logits
    for _ in range(TOP_K):
        m = jnp.max(l, axis=0, keepdims=True)
        idx = jnp.min(jnp.where(l == m, eidx, N_EXPERTS), axis=0, keepdims=True)
        sel = eidx == idx
        routed = routed | sel
        l = jnp.where(sel, -jnp.inf, l)
        top_l.append(m)
        top_e.append(idx)
    ex = [jnp.exp(t - top_l[0]) for t in top_l]
    tot = ex[0] + ex[1] + ex[2] + ex[3]
    gates = jnp.concatenate([e / tot for e in ex], axis=0)
    top_e = jnp.concatenate(top_e, axis=0)
    if valid is not None:
        routed = routed & valid
        top_e = jnp.where(valid, top_e, -1)
        gates = jnp.where(valid, gates, 0.0)
    counts = jnp.sum(jnp.where(routed, 1, 0).astype(i32), axis=1, keepdims=True)
    return x1, xf, top_e, gates, counts


def _mix_prompt_kernel(u_ref, halo_ref, attn_ref, ga_ref, gc_ref, x_ref, cw_ref, cb_ref, lg_ref, lb_ref,
                       wpw_ref, bpw_ref, wout_ref, nf_ref, wr_ref, br_ref,
                       x1_ref, xf_ref, te_ref, gt_ref, cnt_ref, s_scr, y_scr, *, tiles_per_seq):
    i = pl.program_id(0)
    first = (i % tiles_per_seq) == 0
    s_scr[0:HALO, :] = jnp.where(first, 0.0, halo_ref[...].astype(f32))
    s_scr[HALO:, :] = u_ref[...].astype(f32)
    _dwconv(s_scr, 0, TM_MIX, cw_ref, cb_ref, y_scr, 0, 64)
    x1, xf, top_e, gates, counts = _mix_tail(
        y_scr[...], attn_ref[...], ga_ref[...], gc_ref[...], x_ref[...], None,
        lg_ref, lb_ref, wpw_ref, bpw_ref, wout_ref, nf_ref, wr_ref, br_ref)
    x1_ref[...] = x1
    xf_ref[...] = xf
    te_ref[...] = top_e
    gt_ref[...] = gates
    cnt_ref[...] = counts


def _mix_sample_kernel(u_ref, hist_ref, attn_ref, ga_ref, gc_ref, x_ref, cw_ref, cb_ref, lg_ref, lb_ref,
                       wpw_ref, bpw_ref, wout_ref, nf_ref, wr_ref, br_ref,
                       x1_ref, xf_ref, te_ref, gt_ref, cnt_ref, s_scr, y_scr, *, batch, tq):
    n = batch * tq
    hist_rows = CONV_WIDTH - 1
    lead = HALO - hist_rows
    stride = HALO + tq
    s_scr[...] = jnp.zeros(s_scr.shape, f32)
    y_scr[...] = jnp.zeros(y_scr.shape, f32)
    for b in range(batch):
        s_scr[b * stride + lead: b * stride + HALO, :] = hist_ref[b]
        s_scr[b * stride + HALO: (b + 1) * stride, :] = u_ref[b * tq:(b + 1) * tq, :]
        _dwconv(s_scr, b * stride, tq, cw_ref, cb_ref, y_scr, b * tq, tq)
    valid = lax.broadcasted_iota(i32, (1, TM_MIX), 1) < n
    x1, xf, top_e, gates, counts = _mix_tail(
        y_scr[...], attn_ref[...], ga_ref[...], gc_ref[...], x_ref[...], valid,
        lg_ref, lb_ref, wpw_ref, bpw_ref, wout_ref, nf_ref, wr_ref, br_ref)
    x1_ref[...] = x1
    xf_ref[...] = xf
    te_ref[...] = top_e
    gt_ref[...] = gates
    cnt_ref[0] = counts


def _mix_weight_specs(nidx):
    const = (lambda i: (0, 0)) if nidx == 1 else (lambda: (0, 0))
    shapes = [(HALO, CONV_CH), (1, CONV_CH), (1, CONV_CH), (1, CONV_CH), (CONV_CH, D_MODEL), (1, D_MODEL),
              (D_MODEL, D_MODEL), (1, D_MODEL), (N_EXPERTS, D_MODEL), (N_EXPERTS, 1)]
    return [pl.BlockSpec(s, const) for s in shapes]


def _mix_out(n, nt):
    shapes = [jax.ShapeDtypeStruct((n, D_MODEL), f32), jax.ShapeDtypeStruct((n, D_MODEL), bf16),
              jax.ShapeDtypeStruct((TOP_K, n), i32), jax.ShapeDtypeStruct((TOP_K, n), f32),
              jax.ShapeDtypeStruct((nt, N_EXPERTS, 1), i32)]
    return shapes


def _mix_prompt(u, attn, ga, gc, x2d, weights, tiles_per_seq):
    n = x2d.shape[0]
    nt = n // TM_MIX
    row = lambda i: (i, 0)
    halo = lambda i: (jnp.maximum(i * (TM_MIX // HALO) - 1, 0), 0)
    tok = lambda i: (0, i)
    return pl.pallas_call(
        functools.partial(_mix_prompt_kernel, tiles_per_seq=tiles_per_seq),
        grid=(nt,),
        in_specs=[pl.BlockSpec((TM_MIX, CONV_CH), row), pl.BlockSpec((HALO, CONV_CH), halo),
                  pl.BlockSpec((TM_MIX, Q_W), row), pl.BlockSpec((TM_MIX, D_MODEL), row),
                  pl.BlockSpec((TM_MIX, D_MODEL), row), pl.BlockSpec((TM_MIX, D_MODEL), row)] + _mix_weight_specs(1),
        out_specs=[pl.BlockSpec((TM_MIX, D_MODEL), row), pl.BlockSpec((TM_MIX, D_MODEL), row),
                   pl.BlockSpec((TOP_K, TM_MIX), tok), pl.BlockSpec((TOP_K, TM_MIX), tok),
                   pl.BlockSpec((None, N_EXPERTS, 1), lambda i: (i, 0, 0))],
        out_shape=_mix_out(n, nt),
        scratch_shapes=[pltpu.VMEM((TM_MIX + HALO, CONV_CH), f32), pltpu.VMEM((TM_MIX, CONV_CH), f32)],
        compiler_params=_cparams(1),
        name="mix_prompt",
    )(u, u, attn, ga, gc, x2d, *weights)


def _mix_sample(u, hist, attn, ga, gc, x2d, weights, batch, tq):
    pad = lambda a: jnp.pad(a, ((0, TM_MIX - a.shape[0]), (0, 0)))
    return pl.pallas_call(
        functools.partial(_mix_sample_kernel, batch=batch, tq=tq),
        out_shape=_mix_out(TM_MIX, 1),
        scratch_shapes=[pltpu.VMEM((batch * (HALO + tq), CONV_CH), f32), pltpu.VMEM((TM_MIX, CONV_CH), f32)],
        compiler_params=pltpu.CompilerParams(vmem_limit_bytes=VMEM_LIMIT),
        name="mix_sample",
    )(u, hist, pad(attn), pad(ga), pad(gc), pad(x2d), *weights)


def _segment_dmas(tile, seg_rows, seg_local, seg_slot, make_copy, start):
    def body(e, carry):
        idx = tile * N_EXPERTS + e
        n = seg_rows[idx]
        loc = seg_local[idx]
        dst = seg_slot[idx]
        for sz in SEG_CHUNKS:
            @pl.when((n & sz) != 0)
            def _():
                off = pl.multiple_of(n & ~(2 * sz - 1), ROW_ALIGN)
                cp = make_copy(pl.multiple_of(loc + off, ROW_ALIGN), pl.multiple_of(dst + off, ROW_ALIGN), sz)
                if start:
                    cp.start()
                else:
                    cp.wait()
        return carry

    lax.fori_loop(0, N_EXPERTS, body, 0)


def _dispatch_kernel(seg_slot, seg_local, seg_rows, tail_slot, tail_rows,
                     xa_ref, xb_ref, te_ref, lb_ref, lpos_ref, slots_hbm, buf, zbuf, sem, zsem,
                     *, n_prompt_tiles, n_tiles):
    i = pl.program_id(0)
    slot = i % 2
    x = jnp.where(i < n_prompt_tiles, xa_ref[...], xb_ref[...])
    te = te_ref[...]
    eidx = lax.broadcasted_iota(i32, (N_EXPERTS, TM_MIX), 0)
    hits = [te[k:k + 1, :] == eidx for k in range(TOP_K)]
    routed = hits[0] | hits[1] | hits[2] | hits[3]
    before = lax.broadcasted_iota(i32, (TM_MIX, TM_MIX), 0) < lax.broadcasted_iota(i32, (TM_MIX, TM_MIX), 1)
    rank = jnp.dot(jnp.where(routed, 1.0, 0.0).astype(bf16), jnp.where(before, 1.0, 0.0).astype(bf16),
                   preferred_element_type=f32)
    pos = lb_ref[...] + rank.astype(i32)
    lpos = []
    for k in range(TOP_K):
        p = jnp.sum(jnp.where(hits[k], pos, 0), axis=0, keepdims=True)
        lpos.append(jnp.where(te[k:k + 1, :] >= 0, p, -1))
    lpos_ref[...] = jnp.concatenate(lpos, axis=0)
    ridx = lax.broadcasted_iota(i32, (LOCAL_ROWS, TM_MIX), 0)
    onehot = (ridx == lpos[0]) | (ridx == lpos[1]) | (ridx == lpos[2]) | (ridx == lpos[3])
    srt = jnp.dot(jnp.where(onehot, 1.0, 0.0).astype(bf16), x, preferred_element_type=f32)
    half = D_MODEL // 2
    lo = lax.shift_right_logical(lax.bitcast_convert_type(srt[:, :half], i32), 16)
    hi = lax.bitcast_convert_type(srt[:, half:], i32) & jnp.int32(-65536)
    buf[slot] = hi | lo

    def copy_of(s):
        def make(loc, dst, sz):
            return pltpu.make_async_copy(buf.at[s, pl.ds(loc, sz), :], slots_hbm.at[pl.ds(dst, sz), :], sem.at[s])
        return make

    _segment_dmas(i, seg_rows, seg_local, seg_slot, copy_of(slot), True)

    @pl.when(i > 0)
    def _():
        _segment_dmas(i - 1, seg_rows, seg_local, seg_slot, copy_of(1 - slot), False)

    @pl.when(i == n_tiles - 1)
    def _():
        _segment_dmas(i, seg_rows, seg_local, seg_slot, copy_of(slot), False)
        zbuf[...] = jnp.zeros(zbuf.shape, i32)
        for start in (True, False):
            def body(e, carry):
                n = tail_rows[e]
                dst = tail_slot[e]
                for sz in TAIL_CHUNKS:
                    @pl.when((n & sz) != 0)
                    def _():
                        off = pl.multiple_of(n & ~(2 * sz - 1), ROW_ALIGN)
                        cp = pltpu.make_async_copy(zbuf.at[pl.ds(0, sz), :],
                                                   slots_hbm.at[pl.ds(pl.multiple_of(dst + off, ROW_ALIGN), sz), :], zsem)
                        if start:
                            cp.start()
                        else:
                            cp.wait()
                return carry
            lax.fori_loop(0, N_EXPERTS, body, 0)


def _dispatch(xf_p, xf_s, top_e, lbase_v, meta, n_slots):
    n_prompt_tiles = xf_p.shape[0] // TM_MIX
    n_tiles = n_prompt_tiles + 1
    half = D_MODEL // 2
    grid_spec = pltpu.PrefetchScalarGridSpec(
        num_scalar_prefetch=5,
        grid=(n_tiles,),
        in_specs=[pl.BlockSpec((TM_MIX, D_MODEL), lambda i, *_: (jnp.minimum(i, n_prompt_tiles - 1), 0)),
                  pl.BlockSpec((TM_MIX, D_MODEL), lambda i, *_: (0, 0)),
                  pl.BlockSpec((TOP_K, TM_MIX), lambda i, *_: (0, i)),
                  pl.BlockSpec((None, N_EXPERTS, 1), lambda i, *_: (i, 0, 0))],
        out_specs=[pl.BlockSpec((TOP_K, TM_MIX), lambda i, *_: (0, i)),
                   pl.BlockSpec(memory_space=pl.ANY)],
        scratch_shapes=[pltpu.VMEM((2, LOCAL_ROWS, half), i32), pltpu.VMEM((max(TAIL_CHUNKS), half), i32),
                        pltpu.SemaphoreType.DMA((2,)), pltpu.SemaphoreType.DMA(())],
    )
    return pl.pallas_call(
        functools.partial(_dispatch_kernel, n_prompt_tiles=n_prompt_tiles, n_tiles=n_tiles),
        grid_spec=grid_spec,
        out_shape=[jax.ShapeDtypeStruct((TOP_K, n_tiles * TM_MIX), i32),
                   jax.ShapeDtypeStruct((n_slots, half), i32)],
        compiler_params=_cparams(1),
        name="dispatch",
    )(*meta, xf_p, xf_s, top_e, lbase_v)


def _combine_kernel(seg_slot, seg_local, seg_rows,
                    lpos_ref, gate_ref, xa_ref, xb_ref, nfin_ref, y_hbm, ya_ref, yb_ref, buf, sem,
                    *, n_prompt_tiles, n_tiles):
    i = pl.program_id(0)
    slot = i % 2

    def copy_of(s):
        def make(loc, src, sz):
            return pltpu.make_async_copy(y_hbm.at[pl.ds(src, sz), :], buf.at[s, pl.ds(loc, sz), :], sem.at[s])
        return make

    @pl.when(i == 0)
    def _():
        buf[...] = jnp.zeros(buf.shape, f32)
        _segment_dmas(0, seg_rows, seg_local, seg_slot, copy_of(0), True)

    @pl.when(i + 1 < n_tiles)
    def _():
        _segment_dmas(i + 1, seg_rows, seg_local, seg_slot, copy_of(1 - slot), True)

    _segment_dmas(i, seg_rows, seg_local, seg_slot, copy_of(slot), False)

    lpos = lpos_ref[...]
    gate = gate_ref[...]
    cidx = lax.broadcasted_iota(i32, (TM_MIX, LOCAL_ROWS), 1)
    wmat = jnp.zeros((TM_MIX, LOCAL_ROWS), f32)
    for k in range(TOP_K):
        wmat = jnp.where(cidx == lpos[:, k:k + 1], gate[:, k:k + 1], wmat)
    moe = jnp.dot(wmat.astype(bf16), buf[slot].astype(bf16), preferred_element_type=f32)
    x1 = jnp.where(i < n_prompt_tiles, xa_ref[...], xb_ref[...])
    y = _rms(x1 + moe, nfin_ref[...])

    @pl.when(i < n_prompt_tiles)
    def _():
        ya_ref[...] = y

    @pl.when(i >= n_prompt_tiles)
    def _():
        yb_ref[...] = y


def _combine(y_slots, lpos_t, gate_t, x1_p, x1_s, norm_final, meta):
    n_prompt_tiles = x1_p.shape[0] // TM_MIX
    n_tiles = n_prompt_tiles + 1
    last = n_prompt_tiles - 1
    grid_spec = pltpu.PrefetchScalarGridSpec(
        num_scalar_prefetch=3,
        grid=(n_tiles,),
        in_specs=[pl.BlockSpec((TM_MIX, TOP_K), lambda i, *_: (i, 0)),
                  pl.BlockSpec((TM_MIX, TOP_K), lambda i, *_: (i, 0)),
                  pl.BlockSpec((TM_MIX, D_MODEL), lambda i, *_: (jnp.minimum(i, last), 0)),
                  pl.BlockSpec((TM_MIX, D_MODEL), lambda i, *_: (0, 0)),
                  pl.BlockSpec((1, D_MODEL), lambda i, *_: (0, 0)),
                  pl.BlockSpec(memory_space=pl.ANY)],
        out_specs=[pl.BlockSpec((TM_MIX, D_MODEL), lambda i, *_: (jnp.minimum(i, last), 0)),
                   pl.BlockSpec((TM_MIX, D_MODEL), lambda i, *_: (0, 0))],
        scratch_shapes=[pltpu.VMEM((2, LOCAL_ROWS, D_MODEL), f32), pltpu.SemaphoreType.DMA((2,))],
    )
    return pl.pallas_call(
        functools.partial(_combine_kernel, n_prompt_tiles=n_prompt_tiles, n_tiles=n_tiles),
        grid_spec=grid_spec,
        out_shape=[jax.ShapeDtypeStruct(x1_p.shape, f32), jax.ShapeDtypeStruct(x1_s.shape, f32)],
        compiler_params=_cparams(1),
        name="combine",
    )(*meta, lpos_t, gate_t, x1_p, x1_s, norm_final, y_slots)


def _experts_kernel(blk_e, n_used, xs_ref, wgu_ref, bgu_ref, wd_ref, bd_ref, y_ref, wgu_bf, wd_bf):
    b = pl.program_id(0)

    @pl.when(b < n_used[0])
    def _():
        @pl.when((b == 0) | (blk_e[b] != blk_e[jnp.maximum(b - 1, 0)]))
        def _():
            wgu_bf[...] = wgu_ref[...].astype(bf16)
            wd_bf[...] = wd_ref[...].astype(bf16)

        pk = xs_ref[...]
        lo = lax.bitcast_convert_type(lax.shift_left(pk, 16), f32).astype(bf16)
        hi = lax.bitcast_convert_type(pk & jnp.int32(-65536), f32).astype(bf16)
        x = jnp.concatenate([lo, hi], axis=1)
        gu = jnp.dot(x, wgu_bf[...], preferred_element_type=f32) + bgu_ref[...]
        g = jnp.minimum(gu[:, :D_FF], SWIGLU_LIMIT)
        up = jnp.clip(gu[:, D_FF:], -SWIGLU_LIMIT, SWIGLU_LIMIT)
        h = (up + 1.0) * (g * jax.nn.sigmoid(SWIGLU_ALPHA * g))
        y_ref[...] = jnp.dot(h.astype(bf16), wd_bf[...], preferred_element_type=f32) + bd_ref[...]


def _experts(x_slots, w_gu, b_gu, w_down, b_down, blk_e, n_used):
    n_blocks = x_slots.shape[0] // R_BLK
    half = D_MODEL // 2
    rows = lambda b, be, nu: (jnp.minimum(b, nu[0] - 1), 0)
    wsel = lambda b, be, nu: (be[b], 0, 0)
    grid_spec = pltpu.PrefetchScalarGridSpec(
        num_scalar_prefetch=2,
        grid=(n_blocks,),
        in_specs=[pl.BlockSpec((R_BLK, half), rows),
                  pl.BlockSpec((None, D_MODEL, 2 * D_FF), wsel),
                  pl.BlockSpec((None, 1, 2 * D_FF), wsel),
                  pl.BlockSpec((None, D_FF, D_MODEL), wsel),
                  pl.BlockSpec((None, 1, D_MODEL), wsel)],
        out_specs=pl.BlockSpec((R_BLK, D_MODEL), rows),
        scratch_shapes=[pltpu.VMEM((D_MODEL, 2 * D_FF), bf16), pltpu.VMEM((D_FF, D_MODEL), bf16)],
    )
    return pl.pallas_call(
        _experts_kernel,
        grid_spec=grid_spec,
        out_shape=jax.ShapeDtypeStruct((x_slots.shape[0], D_MODEL), f32),
        compiler_params=_cparams(1),
        name="experts",
    )(blk_e, n_used, x_slots, w_gu, b_gu.reshape(N_EXPERTS, 1, 2 * D_FF), w_down,
      b_down.reshape(N_EXPERTS, 1, D_MODEL))


def _slot_layout(counts, n_blocks):
    seg_rows = (counts + ROW_ALIGN - 1) // ROW_ALIGN * ROW_ALIGN
    seg_local = jnp.cumsum(seg_rows, axis=1) - seg_rows
    total = jnp.sum(seg_rows, axis=0)
    region = (total + R_BLK - 1) // R_BLK * R_BLK
    region_end = jnp.cumsum(region)
    ebase = region_end - region
    seg_slot = ebase[None, :] + jnp.cumsum(seg_rows, axis=0) - seg_rows
    n_used = region_end[-1] // R_BLK
    blk = jnp.arange(n_blocks, dtype=i32)
    owner = jnp.searchsorted(region_end, jnp.minimum(blk, n_used - 1) * R_BLK, side="right")
    blk_e = jnp.minimum(owner, N_EXPERTS - 1).astype(i32)
    flat = lambda a: a.reshape(-1).astype(i32)
    return dict(seg_slot=flat(seg_slot), seg_local=flat(seg_local), seg_rows=flat(seg_rows),
                tail_slot=flat(ebase + total), tail_rows=flat(region - total),
                blk_e=blk_e, n_used=n_used.reshape(1).astype(i32),
                lbase_v=seg_local.astype(i32)[:, :, None])


def kernel(x_prompt, x_sample, cache_k, cache_v, cache_conv, norm_mix, w_in, attn_sinks, conv_dw_w, conv_dw_b,
           conv_ln_g, conv_ln_b, w_pw, b_pw, w_out, norm_ffn, w_router, b_router, w_gu, b_gu, w_down, b_down,
           norm_final):
    depth = norm_mix.shape[0]
    assert depth == 1, "one layer per step"
    batch, seq, _ = x_prompt.shape
    dbatch, dseq, _ = x_sample.shape
    window = cache_k.shape[2]
    n_p, n_s = batch * seq, dbatch * dseq
    assert seq % TM_PROJ == 0 and seq % (ATTN_CHUNKS * CHUNK) == 0 and seq >= SPAN and n_s <= TM_MIX
    l = 0
    row = lambda a: a.reshape(1, -1)

    w_in_bf = w_in[l].astype(bf16)
    tabs_p = _rope_tables(jnp.arange(seq, dtype=f32))
    tabs_s = _rope_tables(PAST_LEN + jnp.arange(dseq, dtype=f32))
    tabs_s = tuple(jnp.tile(t, (dbatch, 1)) for t in tabs_s)
    xp2 = x_prompt.reshape(n_p, D_MODEL)
    xs2 = x_sample.reshape(n_s, D_MODEL)
    q_p, kv_p, u_p, ga_p, gc_p, ut_p = _in_proj(xp2, row(norm_mix[l]), w_in_bf, tabs_p, TM_PROJ, seq // TM_PROJ, bf16)
    q_s, kv_s, u_s, ga_s, gc_s, _ = _in_proj(xs2, row(norm_mix[l]), w_in_bf, tabs_s, n_s, 1, f32)

    sinks = attn_sinks[l].astype(f32).reshape(N_KV_HEADS, GROUP, 1)
    attn_p = _attn_prompt(q_p, kv_p, jnp.repeat(sinks, CHUNK, axis=1), batch, seq)
    qpos = PAST_LEN + np.arange(dseq)
    kpos = PAST_LEN - window + np.arange(window + dseq)
    qch, kch = qpos // CHUNK, kpos // CHUNK
    mask = (kch[None, :] >= qch[:, None] - WINDOW_CHUNKS) & (kch[None, :] <= qch[:, None]) & (kpos[None, :] >= 0)
    mask_rows = jnp.asarray(np.tile(mask.astype(np.float32), (GROUP, 1)))
    ck = cache_k[l].reshape(dbatch, window, KV_W)
    cv = cache_v[l].reshape(dbatch, window, KV_W)
    attn_s = _attn_sample(q_s, kv_s, ck, cv, mask_rows, jnp.repeat(sinks, dseq, axis=1), dbatch, dseq)

    mix_w = (jnp.pad(conv_dw_w[l], ((0, HALO - CONV_WIDTH), (0, 0))), row(conv_dw_b[l]), row(conv_ln_g[l]),
             row(conv_ln_b[l]), w_pw[l].astype(bf16), row(b_pw[l]), w_out[l].astype(bf16), row(norm_ffn[l]),
             w_router[l].T.astype(bf16), b_router[l].astype(f32).reshape(N_EXPERTS, 1))
    x1_p, xf_p, te_p, gt_p, cnt_p = _mix_prompt(u_p, attn_p, ga_p, gc_p, xp2, mix_w, seq // TM_MIX)
    x1_s, xf_s, te_s, gt_s, cnt_s = _mix_sample(u_s, cache_conv[l], attn_s, ga_s, gc_s, xs2, mix_w, dbatch, dseq)

    n_tiles = n_p // TM_MIX + 1
    n_all = n_tiles * TM_MIX
    n_blocks = -(-(TOP_K * n_all + (ROW_ALIGN - 1) * N_EXPERTS * n_tiles + N_EXPERTS * (R_BLK - ROW_ALIGN)) // R_BLK)
    lay = _slot_layout(jnp.concatenate([cnt_p, cnt_s], axis=0)[:, :, 0], n_blocks)
    top_e = jnp.concatenate([te_p, te_s], axis=1)
    gates = jnp.concatenate([gt_p, gt_s], axis=1)
    lpos, x_slots = _dispatch(xf_p, xf_s, top_e, lay["lbase_v"],
                              (lay["seg_slot"], lay["seg_local"], lay["seg_rows"], lay["tail_slot"], lay["tail_rows"]),
                              n_blocks * R_BLK)
    y_slots = _experts(x_slots, w_gu[l], b_gu[l], w_down[l], b_down[l], lay["blk_e"], lay["n_used"])
    y_p, y_s = _combine(y_slots, lpos.T, gates.T, x1_p, x1_s, row(norm_final),
                        (lay["seg_slot"], lay["seg_local"], lay["seg_rows"]))

    y_prompt = y_p.reshape(batch, seq, D_MODEL)
    y_sample = y_s[:n_s].reshape(dbatch, dseq, D_MODEL)
    kv_p4 = kv_p.reshape(batch, seq, 2, N_KV_HEADS, HEAD_DIM)
    new_k_p = kv_p4[:, seq - window:, 0][None]
    new_v_p = kv_p4[:, seq - window:, 1][None]
    tiles_per_seq = seq // TM_PROJ
    new_conv_p = ut_p.reshape(batch, tiles_per_seq, HALO, CONV_CH)[:, -1, HALO - (CONV_WIDTH - 1):][None]
    kv_s4 = kv_s.reshape(dbatch, dseq, 2, N_KV_HEADS, HEAD_DIM)
    new_k_s = jnp.concatenate([cache_k[l], kv_s4[:, :, 0]], axis=1)[:, -window:][None]
    new_v_s = jnp.concatenate([cache_v[l], kv_s4[:, :, 1]], axis=1)[:, -window:][None]
    new_conv_s = jnp.concatenate([cache_conv[l], u_s.reshape(dbatch, dseq, CONV_CH)], axis=1)[:, -(CONV_WIDTH - 1):][None]
    return (y_prompt, y_sample, new_k_p, new_v_p, new_conv_p, new_k_s, new_v_s, new_conv_s)
```

```python
import functools

import numpy as np
import jax
import jax.numpy as jnp
from jax import lax
from jax.experimental import pallas as pl
from jax.experimental.pallas import tpu as pltpu

f32 = jnp.float32
bf16 = jnp.bfloat16
i32 = jnp.int32

D_MODEL = 1024
PAST_LEN = 1024
CHUNK = 64
N_HEADS = 16
N_KV_HEADS = 2
HEAD_DIM = 64
GROUP = N_HEADS // N_KV_HEADS
ROT_DIM = HEAD_DIM // 4
ROPE_THETA = 500000.0
WINDOW = 128
WINDOW_CHUNKS = WINDOW // CHUNK
SPAN = (WINDOW_CHUNKS + 1) * CHUNK
CONV_CH = D_MODEL
CONV_WIDTH = 31
N_EXPERTS = 32
TOP_K = 4
D_FF = D_MODEL
SWIGLU_LIMIT = 7.0
SWIGLU_ALPHA = 1.702
NORM_EPS = 1e-5
NEG_INF = -1e30
Q_W = N_HEADS * HEAD_DIM
KV_W = N_KV_HEADS * HEAD_DIM
IN_COLS = Q_W + 2 * KV_W + 2 * CONV_CH + 2 * D_MODEL
Q_SCALE = HEAD_DIM ** -0.5

LANES = 128
SUBLANES = 8
VMEM_LIMIT = 56 * 1024 * 1024

HALO = 32
TM_PROJ = 512
TM_MIX = 256
ATTN_CHUNKS = 4
ROW_ALIGN = SUBLANES
R_BLK = 256
LOCAL_ROWS = -(-(TOP_K * TM_MIX + N_EXPERTS * (ROW_ALIGN - 1)) // LANES) * LANES
SEG_CHUNKS = tuple(2 ** p for p in range(int(np.log2(TM_MIX)), int(np.log2(ROW_ALIGN)) - 1, -1))
TAIL_CHUNKS = tuple(c for c in SEG_CHUNKS if c < R_BLK)


def _cparams(n_axes):
    return pltpu.CompilerParams(dimension_semantics=("arbitrary",) * n_axes,
                                vmem_limit_bytes=VMEM_LIMIT)


def _rms(x, g):
    return x * lax.rsqrt(jnp.mean(x * x, axis=-1, keepdims=True) + NORM_EPS) * g


def _inproj_kernel(x_ref, g_ref, w_ref, cos_ref, sa_ref, sb_ref,
                   q_ref, kv_ref, u_ref, ga_ref, gc_ref, ut_ref):
    xn = _rms(x_ref[...], g_ref[...]).astype(bf16)
    cos = cos_ref[...]
    sa = sa_ref[...]
    sb = sb_ref[...]

    def rope(t):
        return t * cos + pltpu.roll(t, LANES - ROT_DIM // 2, 1) * sa + pltpu.roll(t, ROT_DIM // 2, 1) * sb

    def proj(c0, n):
        return jnp.dot(xn, w_ref[:, c0:c0 + n], preferred_element_type=f32)

    q = proj(0, Q_W)
    for c in range(Q_W // LANES):
        q_ref[:, c * LANES:(c + 1) * LANES] = (rope(q[:, c * LANES:(c + 1) * LANES]) * Q_SCALE).astype(bf16)
    kv = proj(Q_W, 2 * KV_W)
    kv_ref[:, :KV_W] = rope(kv[:, :KV_W])
    kv_ref[:, KV_W:] = kv[:, KV_W:]
    c0 = Q_W + 2 * KV_W
    u = proj(c0, CONV_CH) * jax.nn.sigmoid(proj(c0 + CONV_CH, CONV_CH))
    u_ref[...] = u.astype(u_ref.dtype)
    ut_ref[...] = u[u.shape[0] - HALO:, :]
    c0 += 2 * CONV_CH
    ga_ref[...] = jax.nn.sigmoid(proj(c0, D_MODEL)).astype(ga_ref.dtype)
    gc_ref[...] = jax.nn.sigmoid(proj(c0 + D_MODEL, D_MODEL)).astype(gc_ref.dtype)


def _rope_tables(pos):
    half = ROT_DIM // 2
    inv = ROPE_THETA ** (-jnp.arange(half, dtype=f32) * 2.0 / ROT_DIM)
    ang = pos[:, None] * inv[None, :]
    cos, sin = jnp.cos(ang), jnp.sin(ang)
    t = pos.shape[0]
    ones = jnp.ones((t, HEAD_DIM - ROT_DIM), f32)
    zeros = jnp.zeros((t, HEAD_DIM - ROT_DIM), f32)
    zh = jnp.zeros((t, half), f32)
    c = jnp.concatenate([cos, cos, ones], axis=1)
    a = jnp.concatenate([-sin, zh, zeros], axis=1)
    b = jnp.concatenate([zh, sin, zeros], axis=1)
    rep = LANES // HEAD_DIM
    return jnp.tile(c, (1, rep)), jnp.tile(a, (1, rep)), jnp.tile(b, (1, rep))


def _in_proj(x2d, norm_g, w_bf, tables, tm, tiles_per_seq, act_dtype):
    n = x2d.shape[0]
    nt = n // tm
    row = lambda i: (i, 0)
    const = lambda i: (0, 0)
    tab = lambda i: (i % tiles_per_seq, 0)
    return pl.pallas_call(
        _inproj_kernel,
        grid=(nt,),
        in_specs=[pl.BlockSpec((tm, D_MODEL), row),
                  pl.BlockSpec((1, D_MODEL), const),
                  pl.BlockSpec((D_MODEL, IN_COLS), const, pipeline_mode=pl.Buffered(1)),
                  pl.BlockSpec((tm, LANES), tab),
                  pl.BlockSpec((tm, LANES), tab),
                  pl.BlockSpec((tm, LANES), tab)],
        out_specs=[pl.BlockSpec((tm, Q_W), row),
                   pl.BlockSpec((tm, 2 * KV_W), row),
                   pl.BlockSpec((tm, CONV_CH), row),
                   pl.BlockSpec((tm, D_MODEL), row),
                   pl.BlockSpec((tm, D_MODEL), row),
                   pl.BlockSpec((None, HALO, CONV_CH), lambda i: (i, 0, 0))],
        out_shape=[jax.ShapeDtypeStruct((n, Q_W), bf16),
                   jax.ShapeDtypeStruct((n, 2 * KV_W), f32),
                   jax.ShapeDtypeStruct((n, CONV_CH), act_dtype),
                   jax.ShapeDtypeStruct((n, D_MODEL), act_dtype),
                   jax.ShapeDtypeStruct((n, D_MODEL), act_dtype),
                   jax.ShapeDtypeStruct((nt, HALO, CONV_CH), f32)],
        compiler_params=_cparams(1),
        name="in_proj",
    )(x2d, norm_g, w_bf, *tables)


def _attn_group(qg, kg, vg, mask_t, sink):
    s = lax.dot_general(kg, qg, (((1,), (1,)), ((), ())), preferred_element_type=f32)
    s = jnp.where(mask_t, s, NEG_INF)
    m = jnp.maximum(jnp.max(s, axis=0, keepdims=True), sink)
    p = jnp.exp(s - m)
    denom = jnp.sum(p, axis=0, keepdims=True) + jnp.exp(sink - m)
    pn = (p / denom).astype(bf16)
    return lax.dot_general(pn, vg, (((0,), (0,)), ((), ())), preferred_element_type=f32)


def _heads_to_rows(q, g):
    base = g * GROUP * HEAD_DIM
    return jnp.concatenate([q[:, base + h * HEAD_DIM: base + (h + 1) * HEAD_DIM] for h in range(GROUP)], axis=0)


def _rows_to_heads(o, tq):
    return [o[h * tq:(h + 1) * tq, :] for h in range(GROUP)]


def _attn_prompt_kernel(q_ref, kv_ref, sink_ref, o_ref):
    j = pl.program_id(1)
    for c in range(ATTN_CHUNKS):
        n = j * ATTN_CHUNKS + c
        first = jnp.maximum(n - WINDOW_CHUNKS, 0)
        win = kv_ref[pl.ds(pl.multiple_of(first * CHUNK, CHUNK), SPAN), :]
        mask = lax.broadcasted_iota(i32, (SPAN, GROUP * CHUNK), 0) < (n - first + 1) * CHUNK
        q = q_ref[c * CHUNK:(c + 1) * CHUNK, :]
        outs = []
        for g in range(N_KV_HEADS):
            kg = win[:, g * HEAD_DIM:(g + 1) * HEAD_DIM].astype(bf16)
            vg = win[:, KV_W + g * HEAD_DIM: KV_W + (g + 1) * HEAD_DIM].astype(bf16)
            o = _attn_group(_heads_to_rows(q, g), kg, vg, mask, sink_ref[g])
            outs += _rows_to_heads(o, CHUNK)
        o_ref[c * CHUNK:(c + 1) * CHUNK, :] = jnp.concatenate(outs, axis=1).astype(bf16)


def _attn_prompt(q, kv, sink_rows, batch, seq):
    qrows = ATTN_CHUNKS * CHUNK
    steps = seq // qrows
    return pl.pallas_call(
        _attn_prompt_kernel,
        grid=(batch, steps),
        in_specs=[pl.BlockSpec((qrows, Q_W), lambda b, j: (b * steps + j, 0)),
                  pl.BlockSpec((seq, 2 * KV_W), lambda b, j: (b, 0)),
                  pl.BlockSpec((N_KV_HEADS, 1, GROUP * CHUNK), lambda b, j: (0, 0, 0))],
        out_specs=pl.BlockSpec((qrows, Q_W), lambda b, j: (b * steps + j, 0)),
        out_shape=jax.ShapeDtypeStruct((batch * seq, Q_W), bf16),
        compiler_params=_cparams(2),
        name="attn_prompt",
    )(q, kv, sink_rows)


def _attn_sample_kernel(q_ref, kvn_ref, ck_ref, cv_ref, mask_ref, sink_ref, o_ref):
    tq = q_ref.shape[0]
    q = q_ref[...]
    kvn = kvn_ref[...]
    ck = ck_ref[...]
    cv = cv_ref[...]
    mask = mask_ref[...] > 0.5
    outs = []
    for g in range(N_KV_HEADS):
        sl = slice(g * HEAD_DIM, (g + 1) * HEAD_DIM)
        kg = jnp.concatenate([ck[:, sl], kvn[:, sl]], axis=0).astype(bf16)
        vg = jnp.concatenate([cv[:, sl], kvn[:, KV_W + g * HEAD_DIM: KV_W + (g + 1) * HEAD_DIM]], axis=0).astype(bf16)
        o = _attn_group(_heads_to_rows(q, g), kg, vg, mask, sink_ref[g])
        outs += _rows_to_heads(o, tq)
    o_ref[...] = jnp.concatenate(outs, axis=1)


def _attn_sample(q, kv_new, cache_k, cache_v, mask_rows, sink_rows, batch, tq):
    w = cache_k.shape[1]
    return pl.pallas_call(
        _attn_sample_kernel,
        grid=(batch,),
        in_specs=[pl.BlockSpec((tq, Q_W), lambda b: (b, 0)),
                  pl.BlockSpec((tq, 2 * KV_W), lambda b: (b, 0)),
                  pl.BlockSpec((None, w, KV_W), lambda b: (b, 0, 0)),
                  pl.BlockSpec((None, w, KV_W), lambda b: (b, 0, 0)),
                  pl.BlockSpec((w + tq, GROUP * tq), lambda b: (0, 0)),
                  pl.BlockSpec((N_KV_HEADS, 1, GROUP * tq), lambda b: (0, 0, 0))],
        out_specs=pl.BlockSpec((tq, Q_W), lambda b: (b, 0)),
        out_shape=jax.ShapeDtypeStruct((batch * tq, Q_W), f32),
        compiler_params=_cparams(1),
        name="attn_sample",
    )(q, kv_new, cache_k, cache_v, mask_rows, sink_rows)


def _dwconv(s_ref, row0, n_rows, w_ref, b_ref, y_ref, yrow0, rc):
    lead = HALO - (CONV_WIDTH - 1)
    offsets = range(lead, lead + CONV_WIDTH)

    def lane_body(lc, carry):
        c0 = pl.multiple_of(lc * LANES, LANES)
        cols = pl.ds(c0, LANES)
        bias = b_ref[:, cols]
        groups = rc // SUBLANES
        for t0 in range(0, n_rows, rc):
            acc = jnp.broadcast_to(bias[None], (groups, SUBLANES, LANES))
            for r in range(SUBLANES):
                offs = [o for o in offsets if o % SUBLANES == r]
                if not offs:
                    continue
                amax = max(offs) // SUBLANES
                z = s_ref[pl.ds(row0 + t0 + r, rc + SUBLANES * amax), cols].reshape(groups + amax, SUBLANES, LANES)
                for o in offs:
                    a = o // SUBLANES
                    acc = acc + z[a: a + groups] * w_ref[o - lead, :, cols][None]
            y_ref[pl.ds(yrow0 + t0, rc), cols] = acc.reshape(rc, LANES)
        return carry

    lax.fori_loop(0, CONV_CH // LANES, lane_body, 0)


def _mix_tail(y, attn, ga, gc, x, valid, lg_ref, lb_ref, wpw_ref, bpw_ref, wout_ref, nf_ref, wr_ref, br_ref):
    tm = y.shape[0]
    mu = jnp.mean(y, axis=-1, keepdims=True)
    d = y - mu
    var = jnp.mean(d * d, axis=-1, keepdims=True)
    yn = d * lax.rsqrt(var + NORM_EPS) * lg_ref[...] + lb_ref[...]
    act = (yn * jax.nn.sigmoid(yn)).astype(bf16)
    conv_out = jnp.dot(act, wpw_ref[...], preferred_element_type=f32) + bpw_ref[...]
    h = (ga.astype(f32) * attn.astype(f32) + gc.astype(f32) * conv_out).astype(bf16)
    x1 = x + jnp.dot(h, wout_ref[...], preferred_element_type=f32)
    xf = _rms(x1, nf_ref[...]).astype(bf16)
    logits = lax.dot_general(wr_ref[...], xf, (((1,), (1,)), ((), ())), preferred_element_type=f32) + br_ref[...]
    eidx = lax.broadcasted_iota(i32, (N_EXPERTS, tm), 0)
    routed = jnp.zeros((N_EXPERTS, tm), jnp.bool_)
    top_l, top_e = [], []
    l = logits
    for _ in range(TOP_K):
        m = jnp.max(l, axis=0, keepdims=True)
        idx = jnp.min(jnp.where(l == m, eidx, N_EXPERTS), axis=0, keepdims=True)
        sel = eidx == idx
        routed = routed | sel
        l = jnp.where(sel, -jnp.inf, l)
        top_l.append(m)
        top_e.append(idx)
    ex = [jnp.exp(t - top_l[0]) for t in top_l]
    tot = ex[0] + ex[1] + ex[2] + ex[3]
    gates = jnp.concatenate([e / tot for e in ex], axis=0)
    top_e = jnp.concatenate(top_e, axis=0)
    if valid is not None:
        routed = routed & valid
        top_e = jnp.where(valid, top_e, -1)
        gates = jnp.where(valid, gates, 0.0)
    counts = jnp.sum(jnp.where(routed, 1, 0).astype(i32), axis=1, keepdims=True)
    return x1, xf, top_e, gates, counts


def _mix_prompt_kernel(u_ref, halo_ref, attn_ref, ga_ref, gc_ref, x_ref, cw_ref, cb_ref, lg_ref, lb_ref,
                       wpw_ref, bpw_ref, wout_ref, nf_ref, wr_ref, br_ref,
                       x1_ref, xf_ref, te_ref, gt_ref, cnt_ref, s_scr, y_scr, *, tiles_per_seq):
    i = pl.program_id(0)
    first = (i % tiles_per_seq) == 0
    s_scr[0:HALO, :] = jnp.where(first, 0.0, halo_ref[...].astype(f32))
    s_scr[HALO:, :] = u_ref[...].astype(f32)
    _dwconv(s_scr, 0, TM_MIX, cw_ref, cb_ref, y_scr, 0, 128)
    x1, xf, top_e, gates, counts = _mix_tail(
        y_scr[...], attn_ref[...], ga_ref[...], gc_ref[...], x_ref[...], None,
        lg_ref, lb_ref, wpw_ref, bpw_ref, wout_ref, nf_ref, wr_ref, br_ref)
    x1_ref[...] = x1
    xf_ref[...] = xf
    te_ref[...] = top_e
    gt_ref[...] = gates
    cnt_ref[...] = counts


def _mix_sample_kernel(u_ref, hist_ref, attn_ref, ga_ref, gc_ref, x_ref, cw_ref, cb_ref, lg_ref, lb_ref,
                       wpw_ref, bpw_ref, wout_ref, nf_ref, wr_ref, br_ref,
                       x1_ref, xf_ref, te_ref, gt_ref, cnt_ref, s_scr, y_scr, *, batch, tq):
    n = batch * tq
    hist_rows = CONV_WIDTH - 1
    lead = HALO - hist_rows
    stride = HALO + tq
    s_scr[...] = jnp.zeros(s_scr.shape, f32)
    y_scr[...] = jnp.zeros(y_scr.shape, f32)
    as_conv_input = lambda a: a.astype(bf16).astype(f32)
    for b in range(batch):
        s_scr[b * stride + lead: b * stride + HALO, :] = as_conv_input(hist_ref[b])
        s_scr[b * stride + HALO: (b + 1) * stride, :] = as_conv_input(u_ref[b * tq:(b + 1) * tq, :])
        _dwconv(s_scr, b * stride, tq, cw_ref, cb_ref, y_scr, b * tq, tq)
    valid = lax.broadcasted_iota(i32, (1, TM_MIX), 1) < n
    x1, xf, top_e, gates, counts = _mix_tail(
        y_scr[...], attn_ref[...], ga_ref[...], gc_ref[...], x_ref[...], valid,
        lg_ref, lb_ref, wpw_ref, bpw_ref, wout_ref, nf_ref, wr_ref, br_ref)
    x1_ref[...] = x1
    xf_ref[...] = xf
    te_ref[...] = top_e
    gt_ref[...] = gates
    cnt_ref[0] = counts


def _mix_weight_specs(nidx):
    shapes = [(CONV_WIDTH, SUBLANES, CONV_CH), (1, CONV_CH), (1, CONV_CH), (1, CONV_CH), (CONV_CH, D_MODEL),
              (1, D_MODEL), (D_MODEL, D_MODEL), (1, D_MODEL), (N_EXPERTS, D_MODEL), (N_EXPERTS, 1)]
    return [pl.BlockSpec(s, functools.partial(lambda nd, i: (0,) * nd, len(s))) for s in shapes]


def _mix_out(n, nt):
    shapes = [jax.ShapeDtypeStruct((n, D_MODEL), f32), jax.ShapeDtypeStruct((n, D_MODEL), bf16),
              jax.ShapeDtypeStruct((TOP_K, n), i32), jax.ShapeDtypeStruct((TOP_K, n), f32),
              jax.ShapeDtypeStruct((nt, N_EXPERTS, 1), i32)]
    return shapes


def _mix_prompt(u, attn, ga, gc, x2d, weights, tiles_per_seq):
    n = x2d.shape[0]
    nt = n // TM_MIX
    row = lambda i: (i, 0)
    halo = lambda i: (jnp.maximum(i * (TM_MIX // HALO) - 1, 0), 0)
    tok = lambda i: (0, i)
    return pl.pallas_call(
        functools.partial(_mix_prompt_kernel, tiles_per_seq=tiles_per_seq),
        grid=(nt,),
        in_specs=[pl.BlockSpec((TM_MIX, CONV_CH), row), pl.BlockSpec((HALO, CONV_CH), halo),
                  pl.BlockSpec((TM_MIX, Q_W), row), pl.BlockSpec((TM_MIX, D_MODEL), row),
                  pl.BlockSpec((TM_MIX, D_MODEL), row), pl.BlockSpec((TM_MIX, D_MODEL), row)] + _mix_weight_specs(1),
        out_specs=[pl.BlockSpec((TM_MIX, D_MODEL), row), pl.BlockSpec((TM_MIX, D_MODEL), row),
                   pl.BlockSpec((TOP_K, TM_MIX), tok), pl.BlockSpec((TOP_K, TM_MIX), tok),
                   pl.BlockSpec((None, N_EXPERTS, 1), lambda i: (i, 0, 0))],
        out_shape=_mix_out(n, nt),
        scratch_shapes=[pltpu.VMEM((TM_MIX + HALO, CONV_CH), f32), pltpu.VMEM((TM_MIX, CONV_CH), f32)],
        compiler_params=_cparams(1),
        name="mix_prompt",
    )(u, u, attn, ga, gc, x2d, *weights)


def _mix_sample(u, hist, attn, ga, gc, x2d, weights, batch, tq):
    pad = lambda a: jnp.pad(a, ((0, TM_MIX - a.shape[0]), (0, 0)))
    return pl.pallas_call(
        functools.partial(_mix_sample_kernel, batch=batch, tq=tq),
        out_shape=_mix_out(TM_MIX, 1),
        scratch_shapes=[pltpu.VMEM((batch * (HALO + tq), CONV_CH), f32), pltpu.VMEM((TM_MIX, CONV_CH), f32)],
        compiler_params=pltpu.CompilerParams(vmem_limit_bytes=VMEM_LIMIT),
        name="mix_sample",
    )(u, hist, pad(attn), pad(ga), pad(gc), pad(x2d), *weights)


def _segment_dmas(tile, seg_rows, seg_local, seg_slot, make_copy, start):
    def body(e, carry):
        idx = tile * N_EXPERTS + e
        n = seg_rows[idx]
        loc = seg_local[idx]
        dst = seg_slot[idx]
        for sz in SEG_CHUNKS:
            @pl.when((n & sz) != 0)
            def _():
                off = pl.multiple_of(n & ~(2 * sz - 1), ROW_ALIGN)
                cp = make_copy(pl.multiple_of(loc + off, ROW_ALIGN), pl.multiple_of(dst + off, ROW_ALIGN), sz)
                if start:
                    cp.start()
                else:
                    cp.wait()
        return carry

    lax.fori_loop(0, N_EXPERTS, body, 0)


def _dispatch_kernel(seg_slot, seg_local, seg_rows, tail_slot, tail_rows,
                     xa_ref, xb_ref, te_ref, lb_ref, lpos_ref, slots_hbm, buf, zbuf, sem, zsem,
                     *, n_prompt_tiles, n_tiles):
    i = pl.program_id(0)
    slot = i % 2
    x = jnp.where(i < n_prompt_tiles, xa_ref[...], xb_ref[...])
    te = te_ref[...]
    eidx = lax.broadcasted_iota(i32, (N_EXPERTS, TM_MIX), 0)
    hits = [te[k:k + 1, :] == eidx for k in range(TOP_K)]
    routed = hits[0] | hits[1] | hits[2] | hits[3]
    before = lax.broadcasted_iota(i32, (TM_MIX, TM_MIX), 0) < lax.broadcasted_iota(i32, (TM_MIX, TM_MIX), 1)
    rank = jnp.dot(jnp.where(routed, 1.0, 0.0).astype(bf16), jnp.where(before, 1.0, 0.0).astype(bf16),
                   preferred_element_type=f32)
    pos = lb_ref[...] + rank.astype(i32)
    lpos = []
    for k in range(TOP_K):
        p = jnp.sum(jnp.where(hits[k], pos, 0), axis=0, keepdims=True)
        lpos.append(jnp.where(te[k:k + 1, :] >= 0, p, -1))
    lpos_ref[...] = jnp.concatenate(lpos, axis=0)
    ridx = lax.broadcasted_iota(i32, (LOCAL_ROWS, TM_MIX), 0)
    onehot = (ridx == lpos[0]) | (ridx == lpos[1]) | (ridx == lpos[2]) | (ridx == lpos[3])
    srt = jnp.dot(jnp.where(onehot, 1.0, 0.0).astype(bf16), x, preferred_element_type=f32)
    half = D_MODEL // 2
    lo = lax.shift_right_logical(lax.bitcast_convert_type(srt[:, :half], i32), 16)
    hi = lax.bitcast_convert_type(srt[:, half:], i32) & jnp.int32(-65536)
    buf[slot] = hi | lo

    def copy_of(s):
        def make(loc, dst, sz):
            return pltpu.make_async_copy(buf.at[s, pl.ds(loc, sz), :], slots_hbm.at[pl.ds(dst, sz), :], sem.at[s])
        return make

    _segment_dmas(i, seg_rows, seg_local, seg_slot, copy_of(slot), True)

    @pl.when(i > 0)
    def _():
        _segment_dmas(i - 1, seg_rows, seg_local, seg_slot, copy_of(1 - slot), False)

    @pl.when(i == n_tiles - 1)
    def _():
        _segment_dmas(i, seg_rows, seg_local, seg_slot, copy_of(slot), False)
        zbuf[...] = jnp.zeros(zbuf.shape, i32)
        for start in (True, False):
            def body(e, carry):
                n = tail_rows[e]
                dst = tail_slot[e]
                for sz in TAIL_CHUNKS:
                    @pl.when((n & sz) != 0)
                    def _():
                        off = pl.multiple_of(n & ~(2 * sz - 1), ROW_ALIGN)
                        cp = pltpu.make_async_copy(zbuf.at[pl.ds(0, sz), :],
                                                   slots_hbm.at[pl.ds(pl.multiple_of(dst + off, ROW_ALIGN), sz), :], zsem)
                        if start:
                            cp.start()
                        else:
                            cp.wait()
                return carry
            lax.fori_loop(0, N_EXPERTS, body, 0)


def _dispatch(xf_p, xf_s, top_e, lbase_v, meta, n_slots):
    n_prompt_tiles = xf_p.shape[0] // TM_MIX
    n_tiles = n_prompt_tiles + 1
    half = D_MODEL // 2
    grid_spec = pltpu.PrefetchScalarGridSpec(
        num_scalar_prefetch=5,
        grid=(n_tiles,),
        in_specs=[pl.BlockSpec((TM_MIX, D_MODEL), lambda i, *_: (jnp.minimum(i, n_prompt_tiles - 1), 0)),
                  pl.BlockSpec((TM_MIX, D_MODEL), lambda i, *_: (0, 0)),
                  pl.BlockSpec((TOP_K, TM_MIX), lambda i, *_: (0, i)),
                  pl.BlockSpec((None, N_EXPERTS, 1), lambda i, *_: (i, 0, 0))],
        out_specs=[pl.BlockSpec((TOP_K, TM_MIX), lambda i, *_: (0, i)),
                   pl.BlockSpec(memory_space=pl.ANY)],
        scratch_shapes=[pltpu.VMEM((2, LOCAL_ROWS, half), i32), pltpu.VMEM((max(TAIL_CHUNKS), half), i32),
                        pltpu.SemaphoreType.DMA((2,)), pltpu.SemaphoreType.DMA(())],
    )
    return pl.pallas_call(
        functools.partial(_dispatch_kernel, n_prompt_tiles=n_prompt_tiles, n_tiles=n_tiles),
        grid_spec=grid_spec,
        out_shape=[jax.ShapeDtypeStruct((TOP_K, n_tiles * TM_MIX), i32),
                   jax.ShapeDtypeStruct((n_slots, half), i32)],
        compiler_params=_cparams(1),
        name="dispatch",
    )(*meta, xf_p, xf_s, top_e, lbase_v)


def _combine_kernel(seg_slot, seg_local, seg_rows,
                    lpos_ref, gate_ref, xa_ref, xb_ref, nfin_ref, y_hbm, ya_ref, yb_ref, buf, sem,
                    *, n_prompt_tiles, n_tiles):
    i = pl.program_id(0)
    slot = i % 2

    def copy_of(s):
        def make(loc, src, sz):
            return pltpu.make_async_copy(y_hbm.at[pl.ds(src, sz), :], buf.at[s, pl.ds(loc, sz), :], sem.at[s])
        return make

    @pl.when(i == 0)
    def _():
        buf[...] = jnp.zeros(buf.shape, f32)
        _segment_dmas(0, seg_rows, seg_local, seg_slot, copy_of(0), True)

    @pl.when(i + 1 < n_tiles)
    def _():
        _segment_dmas(i + 1, seg_rows, seg_local, seg_slot, copy_of(1 - slot), True)

    _segment_dmas(i, seg_rows, seg_local, seg_slot, copy_of(slot), False)

    lpos = lpos_ref[...]
    gate = gate_ref[...]
    cidx = lax.broadcasted_iota(i32, (TM_MIX, LOCAL_ROWS), 1)
    wmat = jnp.zeros((TM_MIX, LOCAL_ROWS), f32)
    for k in range(TOP_K):
        wmat = jnp.where(cidx == lpos[:, k:k + 1], gate[:, k:k + 1], wmat)
    moe = jnp.dot(wmat.astype(bf16), buf[slot].astype(bf16), preferred_element_type=f32)
    x1 = jnp.where(i < n_prompt_tiles, xa_ref[...], xb_ref[...])
    y = _rms(x1 + moe, nfin_ref[...])

    @pl.when(i < n_prompt_tiles)
    def _():
        ya_ref[...] = y

    @pl.when(i >= n_prompt_tiles)
    def _():
        yb_ref[...] = y


def _combine(y_slots, lpos_t, gate_t, x1_p, x1_s, norm_final, meta):
    n_prompt_tiles = x1_p.shape[0] // TM_MIX
    n_tiles = n_prompt_tiles + 1
    last = n_prompt_tiles - 1
    grid_spec = pltpu.PrefetchScalarGridSpec(
        num_scalar_prefetch=3,
        grid=(n_tiles,),
        in_specs=[pl.BlockSpec((TM_MIX, TOP_K), lambda i, *_: (i, 0)),
                  pl.BlockSpec((TM_MIX, TOP_K), lambda i, *_: (i, 0)),
                  pl.BlockSpec((TM_MIX, D_MODEL), lambda i, *_: (jnp.minimum(i, last), 0)),
                  pl.BlockSpec((TM_MIX, D_MODEL), lambda i, *_: (0, 0)),
                  pl.BlockSpec((1, D_MODEL), lambda i, *_: (0, 0)),
                  pl.BlockSpec(memory_space=pl.ANY)],
        out_specs=[pl.BlockSpec((TM_MIX, D_MODEL), lambda i, *_: (jnp.minimum(i, last), 0)),
                   pl.BlockSpec((TM_MIX, D_MODEL), lambda i, *_: (0, 0))],
        scratch_shapes=[pltpu.VMEM((2, LOCAL_ROWS, D_MODEL), f32), pltpu.SemaphoreType.DMA((2,))],
    )
    return pl.pallas_call(
        functools.partial(_combine_kernel, n_prompt_tiles=n_prompt_tiles, n_tiles=n_tiles),
        grid_spec=grid_spec,
        out_shape=[jax.ShapeDtypeStruct(x1_p.shape, f32), jax.ShapeDtypeStruct(x1_s.shape, f32)],
        compiler_params=_cparams(1),
        name="combine",
    )(*meta, lpos_t, gate_t, x1_p, x1_s, norm_final, y_slots)


def _experts_kernel(first_row, n_blk, xs_hbm, wgu_ref, bgu_ref, wd_ref, bd_ref, y_hbm,
                    wgu_bf, wd_bf, xbuf, ybuf, xsem, ysem):
    e = pl.program_id(0)
    nb = n_blk[e]
    row0 = first_row[e]

    def rows(j):
        return pl.ds(pl.multiple_of(row0 + j * R_BLK, R_BLK), R_BLK)

    def x_copy(j, slot):
        return pltpu.make_async_copy(xs_hbm.at[rows(j), :], xbuf.at[slot], xsem.at[slot])

    def y_copy(j, slot):
        return pltpu.make_async_copy(ybuf.at[slot], y_hbm.at[rows(j), :], ysem.at[slot])

    @pl.when(nb > 0)
    def _():
        x_copy(0, 0).start()
        wgu_bf[...] = wgu_ref[...].astype(bf16)
        wd_bf[...] = wd_ref[...].astype(bf16)

        def body(j, carry):
            slot = j % 2

            @pl.when(j + 1 < nb)
            def _():
                x_copy(j + 1, 1 - slot).start()

            x_copy(j, slot).wait()

            @pl.when(j >= 2)
            def _():
                y_copy(j - 2, slot).wait()

            pk = xbuf[slot]
            lo = lax.bitcast_convert_type(lax.shift_left(pk, 16), f32).astype(bf16)
            hi = lax.bitcast_convert_type(pk & jnp.int32(-65536), f32).astype(bf16)
            x = jnp.concatenate([lo, hi], axis=1)
            gu = jnp.dot(x, wgu_bf[...], preferred_element_type=f32) + bgu_ref[...]
            g = jnp.minimum(gu[:, :D_FF], SWIGLU_LIMIT)
            up = jnp.clip(gu[:, D_FF:], -SWIGLU_LIMIT, SWIGLU_LIMIT)
            h = (up + 1.0) * (g * jax.nn.sigmoid(SWIGLU_ALPHA * g))
            ybuf[slot] = jnp.dot(h.astype(bf16), wd_bf[...], preferred_element_type=f32) + bd_ref[...]
            y_copy(j, slot).start()
            return carry

        lax.fori_loop(0, nb, body, 0)

        @pl.when(nb >= 2)
        def _():
            y_copy(nb - 2, nb % 2).wait()

        y_copy(nb - 1, (nb - 1) % 2).wait()


def _experts(x_slots, w_gu, b_gu, w_down, b_down, first_row, n_blk):
    half = D_MODEL // 2
    wsel = lambda e, *_: (e, 0, 0)
    grid_spec = pltpu.PrefetchScalarGridSpec(
        num_scalar_prefetch=2,
        grid=(N_EXPERTS,),
        in_specs=[pl.BlockSpec(memory_space=pl.ANY),
                  pl.BlockSpec((None, D_MODEL, 2 * D_FF), wsel),
                  pl.BlockSpec((None, 1, 2 * D_FF), wsel),
                  pl.BlockSpec((None, D_FF, D_MODEL), wsel),
                  pl.BlockSpec((None, 1, D_MODEL), wsel)],
        out_specs=pl.BlockSpec(memory_space=pl.ANY),
        scratch_shapes=[pltpu.VMEM((D_MODEL, 2 * D_FF), bf16), pltpu.VMEM((D_FF, D_MODEL), bf16),
                        pltpu.VMEM((2, R_BLK, half), i32), pltpu.VMEM((2, R_BLK, D_MODEL), f32),
                        pltpu.SemaphoreType.DMA((2,)), pltpu.SemaphoreType.DMA((2,))],
    )
    return pl.pallas_call(
        _experts_kernel,
        grid_spec=grid_spec,
        out_shape=jax.ShapeDtypeStruct((x_slots.shape[0], D_MODEL), f32),
        compiler_params=_cparams(1),
        name="experts",
    )(first_row, n_blk, x_slots, w_gu, b_gu.reshape(N_EXPERTS, 1, 2 * D_FF), w_down,
      b_down.reshape(N_EXPERTS, 1, D_MODEL))


def _slot_layout(counts):
    seg_rows = (counts + ROW_ALIGN - 1) // ROW_ALIGN * ROW_ALIGN
    seg_local = jnp.cumsum(seg_rows, axis=1) - seg_rows
    total = jnp.sum(seg_rows, axis=0)
    region = (total + R_BLK - 1) // R_BLK * R_BLK
    region_end = jnp.cumsum(region)
    ebase = region_end - region
    seg_slot = ebase[None, :] + jnp.cumsum(seg_rows, axis=0) - seg_rows
    flat = lambda a: a.reshape(-1).astype(i32)
    return dict(seg_slot=flat(seg_slot), seg_local=flat(seg_local), seg_rows=flat(seg_rows),
                tail_slot=flat(ebase + total), tail_rows=flat(region - total),
                first_row=flat(ebase), n_blk=flat(region // R_BLK),
                lbase_v=seg_local.astype(i32)[:, :, None])


def kernel(x_prompt, x_sample, cache_k, cache_v, cache_conv, norm_mix, w_in, attn_sinks, conv_dw_w, conv_dw_b,
           conv_ln_g, conv_ln_b, w_pw, b_pw, w_out, norm_ffn, w_router, b_router, w_gu, b_gu, w_down, b_down,
           norm_final):
    depth = norm_mix.shape[0]
    assert depth == 1, "one layer per step"
    batch, seq, _ = x_prompt.shape
    dbatch, dseq, _ = x_sample.shape
    window = cache_k.shape[2]
    n_p, n_s = batch * seq, dbatch * dseq
    assert seq % TM_PROJ == 0 and seq % (ATTN_CHUNKS * CHUNK) == 0 and seq >= SPAN and n_s <= TM_MIX
    l = 0
    row = lambda a: a.reshape(1, -1)

    w_in_bf = w_in[l].astype(bf16)
    tabs_p = _rope_tables(jnp.arange(seq, dtype=f32))
    tabs_s = _rope_tables(PAST_LEN + jnp.arange(dseq, dtype=f32))
    tabs_s = tuple(jnp.tile(t, (dbatch, 1)) for t in tabs_s)
    xp2 = x_prompt.reshape(n_p, D_MODEL)
    xs2 = x_sample.reshape(n_s, D_MODEL)
    q_p, kv_p, u_p, ga_p, gc_p, ut_p = _in_proj(xp2, row(norm_mix[l]), w_in_bf, tabs_p, TM_PROJ, seq // TM_PROJ, bf16)
    q_s, kv_s, u_s, ga_s, gc_s, _ = _in_proj(xs2, row(norm_mix[l]), w_in_bf, tabs_s, n_s, 1, f32)

    sinks = attn_sinks[l].astype(f32).reshape(N_KV_HEADS, 1, GROUP)
    attn_p = _attn_prompt(q_p, kv_p, jnp.repeat(sinks, CHUNK, axis=2), batch, seq)
    qpos = PAST_LEN + np.arange(dseq)
    kpos = PAST_LEN - window + np.arange(window + dseq)
    qch, kch = qpos // CHUNK, kpos // CHUNK
    mask = (kch[None, :] >= qch[:, None] - WINDOW_CHUNKS) & (kch[None, :] <= qch[:, None]) & (kpos[None, :] >= 0)
    mask_rows = jnp.asarray(np.tile(mask.astype(np.float32).T, (1, GROUP)))
    ck = cache_k[l].reshape(dbatch, window, KV_W)
    cv = cache_v[l].reshape(dbatch, window, KV_W)
    attn_s = _attn_sample(q_s, kv_s, ck, cv, mask_rows, jnp.repeat(sinks, dseq, axis=2), dbatch, dseq)

    conv_w = jnp.broadcast_to(conv_dw_w[l][:, None, :], (CONV_WIDTH, SUBLANES, CONV_CH))
    mix_w = (conv_w, row(conv_dw_b[l]), row(conv_ln_g[l]),
             row(conv_ln_b[l]), w_pw[l].astype(bf16), row(b_pw[l]), w_out[l].astype(bf16), row(norm_ffn[l]),
             w_router[l].T.astype(bf16), b_router[l].astype(f32).reshape(N_EXPERTS, 1))
    x1_p, xf_p, te_p, gt_p, cnt_p = _mix_prompt(u_p, attn_p, ga_p, gc_p, xp2, mix_w, seq // TM_MIX)
    x1_s, xf_s, te_s, gt_s, cnt_s = _mix_sample(u_s, cache_conv[l], attn_s, ga_s, gc_s, xs2, mix_w, dbatch, dseq)

    n_tiles = n_p // TM_MIX + 1
    n_all = n_tiles * TM_MIX
    n_blocks = -(-(TOP_K * n_all + (ROW_ALIGN - 1) * N_EXPERTS * n_tiles + N_EXPERTS * (R_BLK - ROW_ALIGN)) // R_BLK)
    lay = _slot_layout(jnp.concatenate([cnt_p, cnt_s], axis=0)[:, :, 0])
    top_e = jnp.concatenate([te_p, te_s], axis=1)
    gates = jnp.concatenate([gt_p, gt_s], axis=1)
    lpos, x_slots = _dispatch(xf_p, xf_s, top_e, lay["lbase_v"],
                              (lay["seg_slot"], lay["seg_local"], lay["seg_rows"], lay["tail_slot"], lay["tail_rows"]),
                              n_blocks * R_BLK)
    y_slots = _experts(x_slots, w_gu[l], b_gu[l], w_down[l], b_down[l], lay["first_row"], lay["n_blk"])
    y_p, y_s = _combine(y_slots, lpos.T, gates.T, x1_p, x1_s, row(norm_final),
                        (lay["seg_slot"], lay["seg_local"], lay["seg_rows"]))

    y_prompt = y_p.reshape(batch, seq, D_MODEL)
    y_sample = y_s[:n_s].reshape(dbatch, dseq, D_MODEL)
    kv_p4 = kv_p.reshape(batch, seq, 2, N_KV_HEADS, HEAD_DIM)
    new_k_p = kv_p4[:, seq - window:, 0][None]
    new_v_p = kv_p4[:, seq - window:, 1][None]
    tiles_per_seq = seq // TM_PROJ
    new_conv_p = ut_p.reshape(batch, tiles_per_seq, HALO, CONV_CH)[:, -1, HALO - (CONV_WIDTH - 1):][None]
    kv_s4 = kv_s.reshape(dbatch, dseq, 2, N_KV_HEADS, HEAD_DIM)
    new_k_s = jnp.concatenate([cache_k[l], kv_s4[:, :, 0]], axis=1)[:, -window:][None]
    new_v_s = jnp.concatenate([cache_v[l], kv_s4[:, :, 1]], axis=1)[:, -window:][None]
    new_conv_s = jnp.concatenate([cache_conv[l], u_s.reshape(dbatch, dseq, CONV_CH)], axis=1)[:, -(CONV_WIDTH - 1):][None]
    return (y_prompt, y_sample, new_k_p, new_v_p, new_conv_p, new_k_s, new_v_s, new_conv_s)
```

```python
import functools

import numpy as np
import jax
import jax.numpy as jnp
from jax import lax
from jax.experimental import pallas as pl
from jax.experimental.pallas import tpu as pltpu

f32 = jnp.float32
bf16 = jnp.bfloat16
i32 = jnp.int32

D_MODEL = 1024
PAST_LEN = 1024
CHUNK = 64
N_HEADS = 16
N_KV_HEADS = 2
HEAD_DIM = 64
GROUP = N_HEADS // N_KV_HEADS
ROT_DIM = HEAD_DIM // 4
ROPE_THETA = 500000.0
WINDOW = 128
WINDOW_CHUNKS = WINDOW // CHUNK
SPAN = (WINDOW_CHUNKS + 1) * CHUNK
CONV_CH = D_MODEL
CONV_WIDTH = 31
N_EXPERTS = 32
TOP_K = 4
D_FF = D_MODEL
SWIGLU_LIMIT = 7.0
SWIGLU_ALPHA = 1.702
NORM_EPS = 1e-5
NEG_INF = -1e30
Q_W = N_HEADS * HEAD_DIM
KV_W = N_KV_HEADS * HEAD_DIM
IN_COLS = Q_W + 2 * KV_W + 2 * CONV_CH + 2 * D_MODEL
Q_SCALE = HEAD_DIM ** -0.5

LANES = 128
SUBLANES = 8
VMEM_LIMIT = 56 * 1024 * 1024

HALO = 32
TM_PROJ = 512
TM_MIX = 256
TM_DISP = 512
ATTN_CHUNKS = 4
ROW_ALIGN = SUBLANES
R_BLK = 256
LOCAL_ROWS = -(-(TOP_K * TM_DISP + N_EXPERTS * (ROW_ALIGN - 1)) // LANES) * LANES
SEG_CHUNKS = tuple(2 ** p for p in range(int(np.log2(TM_DISP)), int(np.log2(ROW_ALIGN)) - 1, -1))
TAIL_CHUNKS = tuple(c for c in SEG_CHUNKS if c < R_BLK)


def _cparams(n_axes):
    return pltpu.CompilerParams(dimension_semantics=("arbitrary",) * n_axes,
                                vmem_limit_bytes=VMEM_LIMIT)


def _rms(x, g):
    return x * lax.rsqrt(jnp.mean(x * x, axis=-1, keepdims=True) + NORM_EPS) * g


def _inproj_kernel(x_ref, g_ref, w_ref, cos_ref, sa_ref, sb_ref,
                   q_ref, kv_ref, u_ref, ga_ref, gc_ref, ut_ref):
    xn = _rms(x_ref[...], g_ref[...]).astype(bf16)
    cos = cos_ref[...]
    sa = sa_ref[...]
    sb = sb_ref[...]

    def rope(t):
        return t * cos + pltpu.roll(t, LANES - ROT_DIM // 2, 1) * sa + pltpu.roll(t, ROT_DIM // 2, 1) * sb

    def proj(c0, n):
        return jnp.dot(xn, w_ref[:, c0:c0 + n], preferred_element_type=f32)

    q = proj(0, Q_W)
    for c in range(Q_W // LANES):
        q_ref[:, c * LANES:(c + 1) * LANES] = (rope(q[:, c * LANES:(c + 1) * LANES]) * Q_SCALE).astype(bf16)
    kv = proj(Q_W, 2 * KV_W)
    kv_ref[:, :KV_W] = rope(kv[:, :KV_W])
    kv_ref[:, KV_W:] = kv[:, KV_W:]
    c0 = Q_W + 2 * KV_W
    u = proj(c0, CONV_CH) * jax.nn.sigmoid(proj(c0 + CONV_CH, CONV_CH))
    u_ref[...] = u.astype(u_ref.dtype)
    ut_ref[...] = u[u.shape[0] - HALO:, :]
    c0 += 2 * CONV_CH
    ga_ref[...] = jax.nn.sigmoid(proj(c0, D_MODEL)).astype(ga_ref.dtype)
    gc_ref[...] = jax.nn.sigmoid(proj(c0 + D_MODEL, D_MODEL)).astype(gc_ref.dtype)


def _rope_tables(pos):
    half = ROT_DIM // 2
    inv = ROPE_THETA ** (-jnp.arange(half, dtype=f32) * 2.0 / ROT_DIM)
    ang = pos[:, None] * inv[None, :]
    cos, sin = jnp.cos(ang), jnp.sin(ang)
    t = pos.shape[0]
    ones = jnp.ones((t, HEAD_DIM - ROT_DIM), f32)
    zeros = jnp.zeros((t, HEAD_DIM - ROT_DIM), f32)
    zh = jnp.zeros((t, half), f32)
    c = jnp.concatenate([cos, cos, ones], axis=1)
    a = jnp.concatenate([-sin, zh, zeros], axis=1)
    b = jnp.concatenate([zh, sin, zeros], axis=1)
    rep = LANES // HEAD_DIM
    return jnp.tile(c, (1, rep)), jnp.tile(a, (1, rep)), jnp.tile(b, (1, rep))


def _in_proj(x2d, norm_g, w_bf, tables, tm, tiles_per_seq, act_dtype):
    n = x2d.shape[0]
    nt = n // tm
    row = lambda i: (i, 0)
    const = lambda i: (0, 0)
    tab = lambda i: (i % tiles_per_seq, 0)
    return pl.pallas_call(
        _inproj_kernel,
        grid=(nt,),
        in_specs=[pl.BlockSpec((tm, D_MODEL), row),
                  pl.BlockSpec((1, D_MODEL), const),
                  pl.BlockSpec((D_MODEL, IN_COLS), const, pipeline_mode=pl.Buffered(1)),
                  pl.BlockSpec((tm, LANES), tab),
                  pl.BlockSpec((tm, LANES), tab),
                  pl.BlockSpec((tm, LANES), tab)],
        out_specs=[pl.BlockSpec((tm, Q_W), row),
                   pl.BlockSpec((tm, 2 * KV_W), row),
                   pl.BlockSpec((tm, CONV_CH), row),
                   pl.BlockSpec((tm, D_MODEL), row),
                   pl.BlockSpec((tm, D_MODEL), row),
                   pl.BlockSpec((None, HALO, CONV_CH), lambda i: (i, 0, 0))],
        out_shape=[jax.ShapeDtypeStruct((n, Q_W), bf16),
                   jax.ShapeDtypeStruct((n, 2 * KV_W), f32),
                   jax.ShapeDtypeStruct((n, CONV_CH), act_dtype),
                   jax.ShapeDtypeStruct((n, D_MODEL), act_dtype),
                   jax.ShapeDtypeStruct((n, D_MODEL), act_dtype),
                   jax.ShapeDtypeStruct((nt, HALO, CONV_CH), f32)],
        compiler_params=_cparams(1),
        name="in_proj",
    )(x2d, norm_g, w_bf, *tables)


def _attn_group(qg, kg, vg, mask_t, sink):
    s = lax.dot_general(kg, qg, (((1,), (1,)), ((), ())), preferred_element_type=f32)
    s = jnp.where(mask_t, s, NEG_INF)
    m = jnp.maximum(jnp.max(s, axis=0, keepdims=True), sink)
    p = jnp.exp(s - m)
    denom = jnp.sum(p, axis=0, keepdims=True) + jnp.exp(sink - m)
    pn = (p / denom).astype(bf16)
    return lax.dot_general(pn, vg, (((0,), (0,)), ((), ())), preferred_element_type=f32)


def _heads_to_rows(q, g):
    base = g * GROUP * HEAD_DIM
    return jnp.concatenate([q[:, base + h * HEAD_DIM: base + (h + 1) * HEAD_DIM] for h in range(GROUP)], axis=0)


def _rows_to_heads(o, tq):
    return [o[h * tq:(h + 1) * tq, :] for h in range(GROUP)]


def _attn_prompt_kernel(q_ref, kv_ref, sink_ref, o_ref):
    j = pl.program_id(1)
    for c in range(ATTN_CHUNKS):
        n = j * ATTN_CHUNKS + c
        first = jnp.maximum(n - WINDOW_CHUNKS, 0)
        win = kv_ref[pl.ds(pl.multiple_of(first * CHUNK, CHUNK), SPAN), :]
        mask = lax.broadcasted_iota(i32, (SPAN, GROUP * CHUNK), 0) < (n - first + 1) * CHUNK
        q = q_ref[c * CHUNK:(c + 1) * CHUNK, :]
        outs = []
        for g in range(N_KV_HEADS):
            kg = win[:, g * HEAD_DIM:(g + 1) * HEAD_DIM].astype(bf16)
            vg = win[:, KV_W + g * HEAD_DIM: KV_W + (g + 1) * HEAD_DIM].astype(bf16)
            o = _attn_group(_heads_to_rows(q, g), kg, vg, mask, sink_ref[g])
            outs += _rows_to_heads(o, CHUNK)
        o_ref[c * CHUNK:(c + 1) * CHUNK, :] = jnp.concatenate(outs, axis=1).astype(bf16)


def _attn_prompt(q, kv, sink_rows, batch, seq):
    qrows = ATTN_CHUNKS * CHUNK
    steps = seq // qrows
    return pl.pallas_call(
        _attn_prompt_kernel,
        grid=(batch, steps),
        in_specs=[pl.BlockSpec((qrows, Q_W), lambda b, j: (b * steps + j, 0)),
                  pl.BlockSpec((seq, 2 * KV_W), lambda b, j: (b, 0)),
                  pl.BlockSpec((N_KV_HEADS, 1, GROUP * CHUNK), lambda b, j: (0, 0, 0))],
        out_specs=pl.BlockSpec((qrows, Q_W), lambda b, j: (b * steps + j, 0)),
        out_shape=jax.ShapeDtypeStruct((batch * seq, Q_W), bf16),
        compiler_params=_cparams(2),
        name="attn_prompt",
    )(q, kv, sink_rows)


def _attn_sample_kernel(q_ref, kvn_ref, ck_ref, cv_ref, mask_ref, sink_ref, o_ref):
    tq = q_ref.shape[0]
    q = q_ref[...]
    kvn = kvn_ref[...]
    ck = ck_ref[...]
    cv = cv_ref[...]
    mask = mask_ref[...] > 0.5
    outs = []
    for g in range(N_KV_HEADS):
        sl = slice(g * HEAD_DIM, (g + 1) * HEAD_DIM)
        kg = jnp.concatenate([ck[:, sl], kvn[:, sl]], axis=0).astype(bf16)
        vg = jnp.concatenate([cv[:, sl], kvn[:, KV_W + g * HEAD_DIM: KV_W + (g + 1) * HEAD_DIM]], axis=0).astype(bf16)
        o = _attn_group(_heads_to_rows(q, g), kg, vg, mask, sink_ref[g])
        outs += _rows_to_heads(o, tq)
    o_ref[...] = jnp.concatenate(outs, axis=1)


def _attn_sample(q, kv_new, cache_k, cache_v, mask_rows, sink_rows, batch, tq):
    w = cache_k.shape[1]
    return pl.pallas_call(
        _attn_sample_kernel,
        grid=(batch,),
        in_specs=[pl.BlockSpec((tq, Q_W), lambda b: (b, 0)),
                  pl.BlockSpec((tq, 2 * KV_W), lambda b: (b, 0)),
                  pl.BlockSpec((None, w, KV_W), lambda b: (b, 0, 0)),
                  pl.BlockSpec((None, w, KV_W), lambda b: (b, 0, 0)),
                  pl.BlockSpec((w + tq, GROUP * tq), lambda b: (0, 0)),
                  pl.BlockSpec((N_KV_HEADS, 1, GROUP * tq), lambda b: (0, 0, 0))],
        out_specs=pl.BlockSpec((tq, Q_W), lambda b: (b, 0)),
        out_shape=jax.ShapeDtypeStruct((batch * tq, Q_W), f32),
        compiler_params=_cparams(1),
        name="attn_sample",
    )(q, kv_new, cache_k, cache_v, mask_rows, sink_rows)


def _dwconv(s_ref, row0, n_rows, w_ref, b_ref, y_ref, yrow0, rc):
    lead = HALO - (CONV_WIDTH - 1)
    offsets = range(lead, lead + CONV_WIDTH)

    def lane_body(lc, carry):
        c0 = pl.multiple_of(lc * LANES, LANES)
        cols = pl.ds(c0, LANES)
        bias = b_ref[:, cols]
        groups = rc // SUBLANES
        for t0 in range(0, n_rows, rc):
            acc = jnp.broadcast_to(bias[None], (groups, SUBLANES, LANES))
            for r in range(SUBLANES):
                offs = [o for o in offsets if o % SUBLANES == r]
                if not offs:
                    continue
                amax = max(offs) // SUBLANES
                z = s_ref[pl.ds(row0 + t0 + r, rc + SUBLANES * amax), cols].reshape(groups + amax, SUBLANES, LANES)
                for o in offs:
                    a = o // SUBLANES
                    acc = acc + z[a: a + groups] * w_ref[o - lead, :, cols][None]
            y_ref[pl.ds(yrow0 + t0, rc), cols] = acc.reshape(rc, LANES)
        return carry

    lax.fori_loop(0, CONV_CH // LANES, lane_body, 0)


def _mix_tail(y, attn, ga, gc, x, valid, lg_ref, lb_ref, wpw_ref, bpw_ref, wout_ref, nf_ref, wr_ref, br_ref):
    tm = y.shape[0]
    mu = jnp.mean(y, axis=-1, keepdims=True)
    d = y - mu
    var = jnp.mean(d * d, axis=-1, keepdims=True)
    yn = d * lax.rsqrt(var + NORM_EPS) * lg_ref[...] + lb_ref[...]
    act = (yn * jax.nn.sigmoid(yn)).astype(bf16)
    conv_out = jnp.dot(act, wpw_ref[...], preferred_element_type=f32) + bpw_ref[...]
    h = (ga.astype(f32) * attn.astype(f32) + gc.astype(f32) * conv_out).astype(bf16)
    x1 = x + jnp.dot(h, wout_ref[...], preferred_element_type=f32)
    xf = _rms(x1, nf_ref[...]).astype(bf16)
    logits = lax.dot_general(wr_ref[...], xf, (((1,), (1,)), ((), ())), preferred_element_type=f32) + br_ref[...]
    eidx = lax.broadcasted_iota(i32, (N_EXPERTS, tm), 0)
    routed = jnp.zeros((N_EXPERTS, tm), jnp.bool_)
    top_l, top_e = [], []
    l = logits
    for _ in range(TOP_K):
        m = jnp.max(l, axis=0, keepdims=True)
        idx = jnp.min(jnp.where(l == m, eidx, N_EXPERTS), axis=0, keepdims=True)
        sel = eidx == idx
        routed = routed | sel
        l = jnp.where(sel, -jnp.inf, l)
        top_l.append(m)
        top_e.append(idx)
    ex = [jnp.exp(t - top_l[0]) for t in top_l]
    tot = ex[0] + ex[1] + ex[2] + ex[3]
    gates = jnp.concatenate([e / tot for e in ex], axis=0)
    top_e = jnp.concatenate(top_e, axis=0)
    if valid is not None:
        routed = routed & valid
        top_e = jnp.where(valid, top_e, -1)
        gates = jnp.where(valid, gates, 0.0)
    counts = jnp.sum(jnp.where(routed, 1, 0).astype(i32), axis=1, keepdims=True)
    return x1, xf, top_e, gates, counts


def _mix_prompt_kernel(u_ref, halo_ref, attn_ref, ga_ref, gc_ref, x_ref, cw_ref, cb_ref, lg_ref, lb_ref,
                       wpw_ref, bpw_ref, wout_ref, nf_ref, wr_ref, br_ref,
                       x1_ref, xf_ref, te_ref, gt_ref, cnt_ref, s_scr, y_scr, *, tiles_per_seq):
    i = pl.program_id(0)
    first = (i % tiles_per_seq) == 0
    s_scr[0:HALO, :] = jnp.where(first, 0.0, halo_ref[...].astype(f32))
    s_scr[HALO:, :] = u_ref[...].astype(f32)
    _dwconv(s_scr, 0, TM_MIX, cw_ref, cb_ref, y_scr, 0, 128)
    x1, xf, top_e, gates, counts = _mix_tail(
        y_scr[...], attn_ref[...], ga_ref[...], gc_ref[...], x_ref[...], None,
        lg_ref, lb_ref, wpw_ref, bpw_ref, wout_ref, nf_ref, wr_ref, br_ref)
    x1_ref[...] = x1
    xf_ref[...] = xf
    te_ref[...] = top_e
    gt_ref[...] = gates
    cnt_ref[...] = counts


def _mix_sample_kernel(u_ref, hist_ref, attn_ref, ga_ref, gc_ref, x_ref, cw_ref, cb_ref, lg_ref, lb_ref,
                       wpw_ref, bpw_ref, wout_ref, nf_ref, wr_ref, br_ref,
                       x1_ref, xf_ref, te_ref, gt_ref, cnt_ref, s_scr, y_scr, *, batch, tq):
    n = batch * tq
    hist_rows = CONV_WIDTH - 1
    lead = HALO - hist_rows
    stride = HALO + tq
    s_scr[...] = jnp.zeros(s_scr.shape, f32)
    y_scr[...] = jnp.zeros(y_scr.shape, f32)
    as_conv_input = lambda a: a.astype(bf16).astype(f32)
    for b in range(batch):
        s_scr[b * stride + lead: b * stride + HALO, :] = as_conv_input(hist_ref[b])
        s_scr[b * stride + HALO: (b + 1) * stride, :] = as_conv_input(u_ref[b * tq:(b + 1) * tq, :])
        _dwconv(s_scr, b * stride, tq, cw_ref, cb_ref, y_scr, b * tq, tq)
    valid = lax.broadcasted_iota(i32, (1, TM_DISP), 1) < n
    x1, xf, top_e, gates, counts = _mix_tail(
        y_scr[...], attn_ref[...], ga_ref[...], gc_ref[...], x_ref[...], valid,
        lg_ref, lb_ref, wpw_ref, bpw_ref, wout_ref, nf_ref, wr_ref, br_ref)
    x1_ref[...] = x1
    xf_ref[...] = xf
    te_ref[...] = top_e
    gt_ref[...] = gates
    cnt_ref[0] = counts


def _mix_weight_specs(nidx):
    shapes = [(CONV_WIDTH, SUBLANES, CONV_CH), (1, CONV_CH), (1, CONV_CH), (1, CONV_CH), (CONV_CH, D_MODEL),
              (1, D_MODEL), (D_MODEL, D_MODEL), (1, D_MODEL), (N_EXPERTS, D_MODEL), (N_EXPERTS, 1)]
    return [pl.BlockSpec(s, functools.partial(lambda nd, i: (0,) * nd, len(s))) for s in shapes]


def _mix_out(n, nt):
    shapes = [jax.ShapeDtypeStruct((n, D_MODEL), f32), jax.ShapeDtypeStruct((n, D_MODEL), bf16),
              jax.ShapeDtypeStruct((TOP_K, n), i32), jax.ShapeDtypeStruct((TOP_K, n), f32),
              jax.ShapeDtypeStruct((nt, N_EXPERTS, 1), i32)]
    return shapes


def _mix_prompt(u, attn, ga, gc, x2d, weights, tiles_per_seq):
    n = x2d.shape[0]
    nt = n // TM_MIX
    row = lambda i: (i, 0)
    halo = lambda i: (jnp.maximum(i * (TM_MIX // HALO) - 1, 0), 0)
    tok = lambda i: (0, i)
    return pl.pallas_call(
        functools.partial(_mix_prompt_kernel, tiles_per_seq=tiles_per_seq),
        grid=(nt,),
        in_specs=[pl.BlockSpec((TM_MIX, CONV_CH), row), pl.BlockSpec((HALO, CONV_CH), halo),
                  pl.BlockSpec((TM_MIX, Q_W), row), pl.BlockSpec((TM_MIX, D_MODEL), row),
                  pl.BlockSpec((TM_MIX, D_MODEL), row), pl.BlockSpec((TM_MIX, D_MODEL), row)] + _mix_weight_specs(1),
        out_specs=[pl.BlockSpec((TM_MIX, D_MODEL), row), pl.BlockSpec((TM_MIX, D_MODEL), row),
                   pl.BlockSpec((TOP_K, TM_MIX), tok), pl.BlockSpec((TOP_K, TM_MIX), tok),
                   pl.BlockSpec((None, N_EXPERTS, 1), lambda i: (i, 0, 0))],
        out_shape=_mix_out(n, nt),
        scratch_shapes=[pltpu.VMEM((TM_MIX + HALO, CONV_CH), f32), pltpu.VMEM((TM_MIX, CONV_CH), f32)],
        compiler_params=_cparams(1),
        name="mix_prompt",
    )(u, u, attn, ga, gc, x2d, *weights)


def _mix_sample(u, hist, attn, ga, gc, x2d, weights, batch, tq):
    pad = lambda a: jnp.pad(a, ((0, TM_DISP - a.shape[0]), (0, 0)))
    return pl.pallas_call(
        functools.partial(_mix_sample_kernel, batch=batch, tq=tq),
        out_shape=_mix_out(TM_DISP, 1),
        scratch_shapes=[pltpu.VMEM((batch * (HALO + tq), CONV_CH), f32), pltpu.VMEM((TM_DISP, CONV_CH), f32)],
        compiler_params=pltpu.CompilerParams(vmem_limit_bytes=VMEM_LIMIT),
        name="mix_sample",
    )(u, hist, pad(attn), pad(ga), pad(gc), pad(x2d), *weights)


def _segment_dmas(tile, seg_rows, seg_local, seg_slot, make_copy, start):
    def body(e, carry):
        idx = tile * N_EXPERTS + e
        n = seg_rows[idx]
        loc = seg_local[idx]
        dst = seg_slot[idx]
        for sz in SEG_CHUNKS:
            @pl.when((n & sz) != 0)
            def _():
                off = pl.multiple_of(n & ~(2 * sz - 1), ROW_ALIGN)
                cp = make_copy(pl.multiple_of(loc + off, ROW_ALIGN), pl.multiple_of(dst + off, ROW_ALIGN), sz)
                if start:
                    cp.start()
                else:
                    cp.wait()
        return carry

    lax.fori_loop(0, N_EXPERTS, body, 0)


def _dispatch_kernel(seg_slot, seg_local, seg_rows, tail_slot, tail_rows,
                     xa_ref, xb_ref, te_ref, lb_ref, lpos_ref, slots_hbm, buf, zbuf, sem, zsem,
                     *, n_prompt_tiles, n_tiles):
    i = pl.program_id(0)
    slot = i % 2
    x = jnp.where(i < n_prompt_tiles, xa_ref[...], xb_ref[...])
    te = te_ref[...]
    eidx = lax.broadcasted_iota(i32, (N_EXPERTS, TM_DISP), 0)
    hits = [te[k:k + 1, :] == eidx for k in range(TOP_K)]
    routed = hits[0] | hits[1] | hits[2] | hits[3]
    before = lax.broadcasted_iota(i32, (TM_DISP, TM_DISP), 0) < lax.broadcasted_iota(i32, (TM_DISP, TM_DISP), 1)
    rank = jnp.dot(jnp.where(routed, 1.0, 0.0).astype(bf16), jnp.where(before, 1.0, 0.0).astype(bf16),
                   preferred_element_type=f32)
    pos = lb_ref[...] + rank.astype(i32)
    lpos = []
    for k in range(TOP_K):
        p = jnp.sum(jnp.where(hits[k], pos, 0), axis=0, keepdims=True)
        lpos.append(jnp.where(te[k:k + 1, :] >= 0, p, -1))
    lpos_ref[...] = jnp.concatenate(lpos, axis=0)
    ridx = lax.broadcasted_iota(i32, (LOCAL_ROWS, TM_DISP), 0)
    onehot = (ridx == lpos[0]) | (ridx == lpos[1]) | (ridx == lpos[2]) | (ridx == lpos[3])
    srt = jnp.dot(jnp.where(onehot, 1.0, 0.0).astype(bf16), x, preferred_element_type=f32)
    half = D_MODEL // 2
    lo = lax.shift_right_logical(lax.bitcast_convert_type(srt[:, :half], i32), 16)
    hi = lax.bitcast_convert_type(srt[:, half:], i32) & jnp.int32(-65536)
    buf[slot] = hi | lo

    def copy_of(s):
        def make(loc, dst, sz):
            return pltpu.make_async_copy(buf.at[s, pl.ds(loc, sz), :], slots_hbm.at[pl.ds(dst, sz), :], sem.at[s])
        return make

    _segment_dmas(i, seg_rows, seg_local, seg_slot, copy_of(slot), True)

    @pl.when(i > 0)
    def _():
        _segment_dmas(i - 1, seg_rows, seg_local, seg_slot, copy_of(1 - slot), False)

    @pl.when(i == n_tiles - 1)
    def _():
        _segment_dmas(i, seg_rows, seg_local, seg_slot, copy_of(slot), False)
        zbuf[...] = jnp.zeros(zbuf.shape, i32)
        for start in (True, False):
            def body(e, carry):
                n = tail_rows[e]
                dst = tail_slot[e]
                for sz in TAIL_CHUNKS:
                    @pl.when((n & sz) != 0)
                    def _():
                        off = pl.multiple_of(n & ~(2 * sz - 1), ROW_ALIGN)
                        cp = pltpu.make_async_copy(zbuf.at[pl.ds(0, sz), :],
                                                   slots_hbm.at[pl.ds(pl.multiple_of(dst + off, ROW_ALIGN), sz), :], zsem)
                        if start:
                            cp.start()
                        else:
                            cp.wait()
                return carry
            lax.fori_loop(0, N_EXPERTS, body, 0)


def _dispatch(xf_p, xf_s, top_e, lbase_v, meta, n_slots):
    n_prompt_tiles = xf_p.shape[0] // TM_DISP
    n_tiles = n_prompt_tiles + 1
    half = D_MODEL // 2
    grid_spec = pltpu.PrefetchScalarGridSpec(
        num_scalar_prefetch=5,
        grid=(n_tiles,),
        in_specs=[pl.BlockSpec((TM_DISP, D_MODEL), lambda i, *_: (jnp.minimum(i, n_prompt_tiles - 1), 0)),
                  pl.BlockSpec((TM_DISP, D_MODEL), lambda i, *_: (0, 0)),
                  pl.BlockSpec((TOP_K, TM_DISP), lambda i, *_: (0, i)),
                  pl.BlockSpec((None, N_EXPERTS, 1), lambda i, *_: (i, 0, 0))],
        out_specs=[pl.BlockSpec((TOP_K, TM_DISP), lambda i, *_: (0, i)),
                   pl.BlockSpec(memory_space=pl.ANY)],
        scratch_shapes=[pltpu.VMEM((2, LOCAL_ROWS, half), i32), pltpu.VMEM((max(TAIL_CHUNKS), half), i32),
                        pltpu.SemaphoreType.DMA((2,)), pltpu.SemaphoreType.DMA(())],
    )
    return pl.pallas_call(
        functools.partial(_dispatch_kernel, n_prompt_tiles=n_prompt_tiles, n_tiles=n_tiles),
        grid_spec=grid_spec,
        out_shape=[jax.ShapeDtypeStruct((TOP_K, n_tiles * TM_DISP), i32),
                   jax.ShapeDtypeStruct((n_slots, half), i32)],
        compiler_params=_cparams(1),
        name="dispatch",
    )(*meta, xf_p, xf_s, top_e, lbase_v)


def _combine_kernel(seg_slot, seg_local, seg_rows,
                    lpos_ref, gate_ref, xa_ref, xb_ref, nfin_ref, y_hbm, ya_ref, yb_ref, buf, sem,
                    *, n_prompt_tiles, n_tiles):
    i = pl.program_id(0)
    slot = i % 2

    def copy_of(s):
        def make(loc, src, sz):
            return pltpu.make_async_copy(y_hbm.at[pl.ds(src, sz), :], buf.at[s, pl.ds(loc, sz), :], sem.at[s])
        return make

    @pl.when(i == 0)
    def _():
        buf[...] = jnp.zeros(buf.shape, f32)
        _segment_dmas(0, seg_rows, seg_local, seg_slot, copy_of(0), True)

    @pl.when(i + 1 < n_tiles)
    def _():
        _segment_dmas(i + 1, seg_rows, seg_local, seg_slot, copy_of(1 - slot), True)

    _segment_dmas(i, seg_rows, seg_local, seg_slot, copy_of(slot), False)

    lpos = lpos_ref[...]
    gate = gate_ref[...]
    cidx = lax.broadcasted_iota(i32, (TM_DISP, LOCAL_ROWS), 1)
    wmat = jnp.zeros((TM_DISP, LOCAL_ROWS), f32)
    for k in range(TOP_K):
        wmat = jnp.where(cidx == lpos[:, k:k + 1], gate[:, k:k + 1], wmat)
    moe = jnp.dot(wmat.astype(bf16), buf[slot].astype(bf16), preferred_element_type=f32)
    x1 = jnp.where(i < n_prompt_tiles, xa_ref[...], xb_ref[...])
    y = _rms(x1 + moe, nfin_ref[...])

    @pl.when(i < n_prompt_tiles)
    def _():
        ya_ref[...] = y

    @pl.when(i >= n_prompt_tiles)
    def _():
        yb_ref[...] = y


def _combine(y_slots, lpos_t, gate_t, x1_p, x1_s, norm_final, meta):
    n_prompt_tiles = x1_p.shape[0] // TM_DISP
    n_tiles = n_prompt_tiles + 1
    last = n_prompt_tiles - 1
    grid_spec = pltpu.PrefetchScalarGridSpec(
        num_scalar_prefetch=3,
        grid=(n_tiles,),
        in_specs=[pl.BlockSpec((TM_DISP, TOP_K), lambda i, *_: (i, 0)),
                  pl.BlockSpec((TM_DISP, TOP_K), lambda i, *_: (i, 0)),
                  pl.BlockSpec((TM_DISP, D_MODEL), lambda i, *_: (jnp.minimum(i, last), 0)),
                  pl.BlockSpec((TM_DISP, D_MODEL), lambda i, *_: (0, 0)),
                  pl.BlockSpec((1, D_MODEL), lambda i, *_: (0, 0)),
                  pl.BlockSpec(memory_space=pl.ANY)],
        out_specs=[pl.BlockSpec((TM_DISP, D_MODEL), lambda i, *_: (jnp.minimum(i, last), 0)),
                   pl.BlockSpec((TM_DISP, D_MODEL), lambda i, *_: (0, 0))],
        scratch_shapes=[pltpu.VMEM((2, LOCAL_ROWS, D_MODEL), f32), pltpu.SemaphoreType.DMA((2,))],
    )
    return pl.pallas_call(
        functools.partial(_combine_kernel, n_prompt_tiles=n_prompt_tiles, n_tiles=n_tiles),
        grid_spec=grid_spec,
        out_shape=[jax.ShapeDtypeStruct(x1_p.shape, f32), jax.ShapeDtypeStruct(x1_s.shape, f32)],
        compiler_params=_cparams(1),
        name="combine",
    )(*meta, lpos_t, gate_t, x1_p, x1_s, norm_final, y_slots)


ROW_DMA_PRIORITY = 1


def _experts_kernel(first_row, n_blk, xs_hbm, wgu_ref, bgu_ref, wd_ref, bd_ref, y_hbm,
                    wgu_bf, wd_bf, xbuf, ybuf, xsem, ysem):
    e = pl.program_id(0)
    nb = n_blk[e]
    row0 = first_row[e]

    def rows(j):
        return pl.ds(pl.multiple_of(row0 + j * R_BLK, R_BLK), R_BLK)

    def x_copy(j, slot):
        return pltpu.make_async_copy(xs_hbm.at[rows(j), :], xbuf.at[slot], xsem.at[slot])

    def y_copy(j, slot):
        return pltpu.make_async_copy(ybuf.at[slot], y_hbm.at[rows(j), :], ysem.at[slot])

    @pl.when(nb > 0)
    def _():
        x_copy(0, 0).start(priority=ROW_DMA_PRIORITY)
        wgu_bf[...] = wgu_ref[...].astype(bf16)
        wd_bf[...] = wd_ref[...].astype(bf16)

        def body(j, carry):
            slot = j % 2

            @pl.when(j + 1 < nb)
            def _():
                x_copy(j + 1, 1 - slot).start(priority=ROW_DMA_PRIORITY)

            x_copy(j, slot).wait()

            @pl.when(j >= 2)
            def _():
                y_copy(j - 2, slot).wait()

            pk = xbuf[slot]
            lo = lax.bitcast_convert_type(lax.shift_left(pk, 16), f32).astype(bf16)
            hi = lax.bitcast_convert_type(pk & jnp.int32(-65536), f32).astype(bf16)
            x = jnp.concatenate([lo, hi], axis=1)
            gu = jnp.dot(x, wgu_bf[...], preferred_element_type=f32) + bgu_ref[...]
            g = jnp.minimum(gu[:, :D_FF], SWIGLU_LIMIT)
            up = jnp.clip(gu[:, D_FF:], -SWIGLU_LIMIT, SWIGLU_LIMIT)
            h = (up + 1.0) * (g * jax.nn.sigmoid(SWIGLU_ALPHA * g))
            ybuf[slot] = jnp.dot(h.astype(bf16), wd_bf[...], preferred_element_type=f32) + bd_ref[...]
            y_copy(j, slot).start(priority=ROW_DMA_PRIORITY)
            return carry

        lax.fori_loop(0, nb, body, 0)

        @pl.when(nb >= 2)
        def _():
            y_copy(nb - 2, nb % 2).wait()

        y_copy(nb - 1, (nb - 1) % 2).wait()


def _experts(x_slots, w_gu, b_gu, w_down, b_down, first_row, n_blk):
    half = D_MODEL // 2
    wsel = lambda e, *_: (e, 0, 0)
    grid_spec = pltpu.PrefetchScalarGridSpec(
        num_scalar_prefetch=2,
        grid=(N_EXPERTS,),
        in_specs=[pl.BlockSpec(memory_space=pl.ANY),
                  pl.BlockSpec((None, D_MODEL, 2 * D_FF), wsel),
                  pl.BlockSpec((None, 1, 2 * D_FF), wsel),
                  pl.BlockSpec((None, D_FF, D_MODEL), wsel),
                  pl.BlockSpec((None, 1, D_MODEL), wsel)],
        out_specs=pl.BlockSpec(memory_space=pl.ANY),
        scratch_shapes=[pltpu.VMEM((D_MODEL, 2 * D_FF), bf16), pltpu.VMEM((D_FF, D_MODEL), bf16),
                        pltpu.VMEM((2, R_BLK, half), i32), pltpu.VMEM((2, R_BLK, D_MODEL), f32),
                        pltpu.SemaphoreType.DMA((2,)), pltpu.SemaphoreType.DMA((2,))],
    )
    return pl.pallas_call(
        _experts_kernel,
        grid_spec=grid_spec,
        out_shape=jax.ShapeDtypeStruct((x_slots.shape[0], D_MODEL), f32),
        compiler_params=_cparams(1),
        name="experts",
    )(first_row, n_blk, x_slots, w_gu, b_gu.reshape(N_EXPERTS, 1, 2 * D_FF), w_down,
      b_down.reshape(N_EXPERTS, 1, D_MODEL))


def _slot_layout(counts):
    seg_rows = (counts + ROW_ALIGN - 1) // ROW_ALIGN * ROW_ALIGN
    seg_local = jnp.cumsum(seg_rows, axis=1) - seg_rows
    total = jnp.sum(seg_rows, axis=0)
    region = (total + R_BLK - 1) // R_BLK * R_BLK
    region_end = jnp.cumsum(region)
    ebase = region_end - region
    seg_slot = ebase[None, :] + jnp.cumsum(seg_rows, axis=0) - seg_rows
    flat = lambda a: a.reshape(-1).astype(i32)
    return dict(seg_slot=flat(seg_slot), seg_local=flat(seg_local), seg_rows=flat(seg_rows),
                tail_slot=flat(ebase + total), tail_rows=flat(region - total),
                first_row=flat(ebase), n_blk=flat(region // R_BLK),
                lbase_v=seg_local.astype(i32)[:, :, None])


def kernel(x_prompt, x_sample, cache_k, cache_v, cache_conv, norm_mix, w_in, attn_sinks, conv_dw_w, conv_dw_b,
           conv_ln_g, conv_ln_b, w_pw, b_pw, w_out, norm_ffn, w_router, b_router, w_gu, b_gu, w_down, b_down,
           norm_final):
    depth = norm_mix.shape[0]
    assert depth == 1, "one layer per step"
    batch, seq, _ = x_prompt.shape
    dbatch, dseq, _ = x_sample.shape
    window = cache_k.shape[2]
    n_p, n_s = batch * seq, dbatch * dseq
    assert seq % TM_PROJ == 0 and seq % (ATTN_CHUNKS * CHUNK) == 0 and seq >= SPAN and n_s <= TM_DISP
    assert n_p % TM_DISP == 0 and TM_DISP % TM_MIX == 0 and seq % TM_MIX == 0
    l = 0
    row = lambda a: a.reshape(1, -1)

    w_in_bf = w_in[l].astype(bf16)
    tabs_p = _rope_tables(jnp.arange(seq, dtype=f32))
    tabs_s = _rope_tables(PAST_LEN + jnp.arange(dseq, dtype=f32))
    tabs_s = tuple(jnp.tile(t, (dbatch, 1)) for t in tabs_s)
    xp2 = x_prompt.reshape(n_p, D_MODEL)
    xs2 = x_sample.reshape(n_s, D_MODEL)
    q_p, kv_p, u_p, ga_p, gc_p, ut_p = _in_proj(xp2, row(norm_mix[l]), w_in_bf, tabs_p, TM_PROJ, seq // TM_PROJ, bf16)
    q_s, kv_s, u_s, ga_s, gc_s, _ = _in_proj(xs2, row(norm_mix[l]), w_in_bf, tabs_s, n_s, 1, f32)

    sinks = attn_sinks[l].astype(f32).reshape(N_KV_HEADS, 1, GROUP)
    attn_p = _attn_prompt(q_p, kv_p, jnp.repeat(sinks, CHUNK, axis=2), batch, seq)
    qpos = PAST_LEN + np.arange(dseq)
    kpos = PAST_LEN - window + np.arange(window + dseq)
    qch, kch = qpos // CHUNK, kpos // CHUNK
    mask = (kch[None, :] >= qch[:, None] - WINDOW_CHUNKS) & (kch[None, :] <= qch[:, None]) & (kpos[None, :] >= 0)
    mask_rows = jnp.asarray(np.tile(mask.astype(np.float32).T, (1, GROUP)))
    ck = cache_k[l].reshape(dbatch, window, KV_W)
    cv = cache_v[l].reshape(dbatch, window, KV_W)
    attn_s = _attn_sample(q_s, kv_s, ck, cv, mask_rows, jnp.repeat(sinks, dseq, axis=2), dbatch, dseq)

    conv_w = jnp.broadcast_to(conv_dw_w[l][:, None, :], (CONV_WIDTH, SUBLANES, CONV_CH))
    mix_w = (conv_w, row(conv_dw_b[l]), row(conv_ln_g[l]),
             row(conv_ln_b[l]), w_pw[l].astype(bf16), row(b_pw[l]), w_out[l].astype(bf16), row(norm_ffn[l]),
             w_router[l].T.astype(bf16), b_router[l].astype(f32).reshape(N_EXPERTS, 1))
    x1_p, xf_p, te_p, gt_p, cnt_p = _mix_prompt(u_p, attn_p, ga_p, gc_p, xp2, mix_w, seq // TM_MIX)
    x1_s, xf_s, te_s, gt_s, cnt_s = _mix_sample(u_s, cache_conv[l], attn_s, ga_s, gc_s, xs2, mix_w, dbatch, dseq)

    n_tiles = n_p // TM_DISP + 1
    n_all = n_tiles * TM_DISP
    n_blocks = -(-(TOP_K * n_all + (ROW_ALIGN - 1) * N_EXPERTS * n_tiles + N_EXPERTS * (R_BLK - ROW_ALIGN)) // R_BLK)
    cnt_p = cnt_p.reshape(n_p // TM_DISP, TM_DISP // TM_MIX, N_EXPERTS).sum(axis=1)
    lay = _slot_layout(jnp.concatenate([cnt_p, cnt_s[:, :, 0]], axis=0))
    top_e = jnp.concatenate([te_p, te_s], axis=1)
    gates = jnp.concatenate([gt_p, gt_s], axis=1)
    lpos, x_slots = _dispatch(xf_p, xf_s, top_e, lay["lbase_v"],
                              (lay["seg_slot"], lay["seg_local"], lay["seg_rows"], lay["tail_slot"], lay["tail_rows"]),
                              n_blocks * R_BLK)
    y_slots = _experts(x_slots, w_gu[l], b_gu[l], w_down[l], b_down[l], lay["first_row"], lay["n_blk"])
    y_p, y_s = _combine(y_slots, lpos.T, gates.T, x1_p, x1_s, row(norm_final),
                        (lay["seg_slot"], lay["seg_local"], lay["seg_rows"]))

    y_prompt = y_p.reshape(batch, seq, D_MODEL)
    y_sample = y_s[:n_s].reshape(dbatch, dseq, D_MODEL)
    kv_tail = kv_p.reshape(batch, seq, 2 * KV_W)[:, seq - window:]
    new_k_p = kv_tail[:, :, :KV_W].reshape(1, batch, window, N_KV_HEADS, HEAD_DIM)
    new_v_p = kv_tail[:, :, KV_W:].reshape(1, batch, window, N_KV_HEADS, HEAD_DIM)
    tiles_per_seq = seq // TM_PROJ
    new_conv_p = ut_p.reshape(batch, tiles_per_seq, HALO, CONV_CH)[:, -1, HALO - (CONV_WIDTH - 1):][None]
    kv_s4 = kv_s.reshape(dbatch, dseq, 2, N_KV_HEADS, HEAD_DIM)
    new_k_s = jnp.concatenate([cache_k[l], kv_s4[:, :, 0]], axis=1)[:, -window:][None]
    new_v_s = jnp.concatenate([cache_v[l], kv_s4[:, :, 1]], axis=1)[:, -window:][None]
    new_conv_s = jnp.concatenate([cache_conv[l], u_s.reshape(dbatch, dseq, CONV_CH)], axis=1)[:, -(CONV_WIDTH - 1):][None]
    return (y_prompt, y_sample, new_k_p, new_v_p, new_conv_p, new_k_s, new_v_s, new_conv_s)
```

```python
import functools

import numpy as np
import jax
import jax.numpy as jnp
from jax import lax
from jax.experimental import pallas as pl
from jax.experimental.pallas import tpu as pltpu

f32 = jnp.float32
bf16 = jnp.bfloat16
i32 = jnp.int32

D_MODEL = 1024
PAST_LEN = 1024
CHUNK = 64
N_HEADS = 16
N_KV_HEADS = 2
HEAD_DIM = 64
GROUP = N_HEADS // N_KV_HEADS
ROT_DIM = HEAD_DIM // 4
ROPE_THETA = 500000.0
WINDOW = 128
WINDOW_CHUNKS = WINDOW // CHUNK
SPAN = (WINDOW_CHUNKS + 1) * CHUNK
CONV_CH = D_MODEL
CONV_WIDTH = 31
N_EXPERTS = 32
TOP_K = 4
D_FF = D_MODEL
SWIGLU_LIMIT = 7.0
SWIGLU_ALPHA = 1.702
NORM_EPS = 1e-5
NEG_INF = -1e30
Q_W = N_HEADS * HEAD_DIM
KV_W = N_KV_HEADS * HEAD_DIM
IN_COLS = Q_W + 2 * KV_W + 2 * CONV_CH + 2 * D_MODEL
Q_SCALE = HEAD_DIM ** -0.5

LANES = 128
SUBLANES = 8
VMEM_LIMIT = 56 * 1024 * 1024

HALO = 32
TM_PROJ = 512
TM_MIX = 256
TM_DISP = 512
ATTN_CHUNKS = 4
ROW_ALIGN = SUBLANES
R_BLK = 512
LOCAL_ROWS = -(-(TOP_K * TM_DISP + N_EXPERTS * (ROW_ALIGN - 1)) // LANES) * LANES
SEG_CHUNKS = tuple(2 ** p for p in range(int(np.log2(TM_DISP)), int(np.log2(ROW_ALIGN)) - 1, -1))
SEG_SPLIT = 2 * TOP_K * TM_DISP // N_EXPERTS
TAIL_CHUNKS = tuple(c for c in SEG_CHUNKS if c < R_BLK)


def _cparams(n_axes):
    return pltpu.CompilerParams(dimension_semantics=("arbitrary",) * n_axes,
                                vmem_limit_bytes=VMEM_LIMIT)


def _rms(x, g):
    return x * lax.rsqrt(jnp.mean(x * x, axis=-1, keepdims=True) + NORM_EPS) * g


def _inproj_kernel(x_ref, g_ref, w_ref, cos_ref, sa_ref, sb_ref,
                   q_ref, kv_ref, u_ref, ga_ref, gc_ref, ut_ref):
    xn = _rms(x_ref[...], g_ref[...]).astype(bf16)
    cos = cos_ref[...]
    sa = sa_ref[...]
    sb = sb_ref[...]

    def rope(t):
        return t * cos + pltpu.roll(t, LANES - ROT_DIM // 2, 1) * sa + pltpu.roll(t, ROT_DIM // 2, 1) * sb

    def proj(c0, n):
        return jnp.dot(xn, w_ref[:, c0:c0 + n], preferred_element_type=f32)

    q = proj(0, Q_W)
    for c in range(Q_W // LANES):
        q_ref[:, c * LANES:(c + 1) * LANES] = (rope(q[:, c * LANES:(c + 1) * LANES]) * Q_SCALE).astype(bf16)
    kv = proj(Q_W, 2 * KV_W)
    kv_ref[:, :KV_W] = rope(kv[:, :KV_W])
    kv_ref[:, KV_W:] = kv[:, KV_W:]
    c0 = Q_W + 2 * KV_W
    u = proj(c0, CONV_CH) * jax.nn.sigmoid(proj(c0 + CONV_CH, CONV_CH))
    u_ref[...] = u.astype(u_ref.dtype)
    ut_ref[...] = u[u.shape[0] - HALO:, :]
    c0 += 2 * CONV_CH
    ga_ref[...] = jax.nn.sigmoid(proj(c0, D_MODEL)).astype(ga_ref.dtype)
    gc_ref[...] = jax.nn.sigmoid(proj(c0 + D_MODEL, D_MODEL)).astype(gc_ref.dtype)


def _rope_tables(pos):
    half = ROT_DIM // 2
    inv = ROPE_THETA ** (-jnp.arange(half, dtype=f32) * 2.0 / ROT_DIM)
    ang = pos[:, None] * inv[None, :]
    cos, sin = jnp.cos(ang), jnp.sin(ang)
    t = pos.shape[0]
    ones = jnp.ones((t, HEAD_DIM - ROT_DIM), f32)
    zeros = jnp.zeros((t, HEAD_DIM - ROT_DIM), f32)
    zh = jnp.zeros((t, half), f32)
    c = jnp.concatenate([cos, cos, ones], axis=1)
    a = jnp.concatenate([-sin, zh, zeros], axis=1)
    b = jnp.concatenate([zh, sin, zeros], axis=1)
    rep = LANES // HEAD_DIM
    return jnp.tile(c, (1, rep)), jnp.tile(a, (1, rep)), jnp.tile(b, (1, rep))


def _in_proj(x2d, norm_g, w_bf, tables, tm, tiles_per_seq, act_dtype):
    n = x2d.shape[0]
    nt = n // tm
    row = lambda i: (i, 0)
    const = lambda i: (0, 0)
    tab = lambda i: (i % tiles_per_seq, 0)
    return pl.pallas_call(
        _inproj_kernel,
        grid=(nt,),
        in_specs=[pl.BlockSpec((tm, D_MODEL), row),
                  pl.BlockSpec((1, D_MODEL), const),
                  pl.BlockSpec((D_MODEL, IN_COLS), const, pipeline_mode=pl.Buffered(1)),
                  pl.BlockSpec((tm, LANES), tab),
                  pl.BlockSpec((tm, LANES), tab),
                  pl.BlockSpec((tm, LANES), tab)],
        out_specs=[pl.BlockSpec((tm, Q_W), row),
                   pl.BlockSpec((tm, 2 * KV_W), row),
                   pl.BlockSpec((tm, CONV_CH), row),
                   pl.BlockSpec((tm, D_MODEL), row),
                   pl.BlockSpec((tm, D_MODEL), row),
                   pl.BlockSpec((None, HALO, CONV_CH), lambda i: (i, 0, 0))],
        out_shape=[jax.ShapeDtypeStruct((n, Q_W), bf16),
                   jax.ShapeDtypeStruct((n, 2 * KV_W), f32),
                   jax.ShapeDtypeStruct((n, CONV_CH), act_dtype),
                   jax.ShapeDtypeStruct((n, D_MODEL), act_dtype),
                   jax.ShapeDtypeStruct((n, D_MODEL), act_dtype),
                   jax.ShapeDtypeStruct((nt, HALO, CONV_CH), f32)],
        compiler_params=_cparams(1),
        name="in_proj",
    )(x2d, norm_g, w_bf, *tables)


def _attn_group(qg, kg, vg, mask_t, sink):
    s = lax.dot_general(kg, qg, (((1,), (1,)), ((), ())), preferred_element_type=f32)
    s = jnp.where(mask_t, s, NEG_INF)
    m = jnp.maximum(jnp.max(s, axis=0, keepdims=True), sink)
    p = jnp.exp(s - m)
    denom = jnp.sum(p, axis=0, keepdims=True) + jnp.exp(sink - m)
    pn = (p / denom).astype(bf16)
    return lax.dot_general(pn, vg, (((0,), (0,)), ((), ())), preferred_element_type=f32)


def _heads_to_rows(q, g):
    base = g * GROUP * HEAD_DIM
    return jnp.concatenate([q[:, base + h * HEAD_DIM: base + (h + 1) * HEAD_DIM] for h in range(GROUP)], axis=0)


def _rows_to_heads(o, tq):
    return [o[h * tq:(h + 1) * tq, :] for h in range(GROUP)]


def _attn_prompt_kernel(q_ref, kv_ref, sink_ref, o_ref):
    j = pl.program_id(1)
    for c in range(ATTN_CHUNKS):
        n = j * ATTN_CHUNKS + c
        first = jnp.maximum(n - WINDOW_CHUNKS, 0)
        win = kv_ref[pl.ds(pl.multiple_of(first * CHUNK, CHUNK), SPAN), :]
        mask = lax.broadcasted_iota(i32, (SPAN, GROUP * CHUNK), 0) < (n - first + 1) * CHUNK
        q = q_ref[c * CHUNK:(c + 1) * CHUNK, :]
        outs = []
        for g in range(N_KV_HEADS):
            kg = win[:, g * HEAD_DIM:(g + 1) * HEAD_DIM].astype(bf16)
            vg = win[:, KV_W + g * HEAD_DIM: KV_W + (g + 1) * HEAD_DIM].astype(bf16)
            o = _attn_group(_heads_to_rows(q, g), kg, vg, mask, sink_ref[g])
            outs += _rows_to_heads(o, CHUNK)
        o_ref[c * CHUNK:(c + 1) * CHUNK, :] = jnp.concatenate(outs, axis=1).astype(bf16)


def _attn_prompt(q, kv, sink_rows, batch, seq):
    qrows = ATTN_CHUNKS * CHUNK
    steps = seq // qrows
    return pl.pallas_call(
        _attn_prompt_kernel,
        grid=(batch, steps),
        in_specs=[pl.BlockSpec((qrows, Q_W), lambda b, j: (b * steps + j, 0)),
                  pl.BlockSpec((seq, 2 * KV_W), lambda b, j: (b, 0)),
                  pl.BlockSpec((N_KV_HEADS, 1, GROUP * CHUNK), lambda b, j: (0, 0, 0))],
        out_specs=pl.BlockSpec((qrows, Q_W), lambda b, j: (b * steps + j, 0)),
        out_shape=jax.ShapeDtypeStruct((batch * seq, Q_W), bf16),
        compiler_params=_cparams(2),
        name="attn_prompt",
    )(q, kv, sink_rows)


def _attn_sample_kernel(q_ref, kvn_ref, ck_ref, cv_ref, mask_ref, sink_ref, o_ref):
    tq = q_ref.shape[0]
    q = q_ref[...]
    kvn = kvn_ref[...]
    ck = ck_ref[...]
    cv = cv_ref[...]
    mask = mask_ref[...] > 0.5
    outs = []
    for g in range(N_KV_HEADS):
        sl = slice(g * HEAD_DIM, (g + 1) * HEAD_DIM)
        kg = jnp.concatenate([ck[:, sl], kvn[:, sl]], axis=0).astype(bf16)
        vg = jnp.concatenate([cv[:, sl], kvn[:, KV_W + g * HEAD_DIM: KV_W + (g + 1) * HEAD_DIM]], axis=0).astype(bf16)
        o = _attn_group(_heads_to_rows(q, g), kg, vg, mask, sink_ref[g])
        outs += _rows_to_heads(o, tq)
    o_ref[...] = jnp.concatenate(outs, axis=1)


def _attn_sample(q, kv_new, cache_k, cache_v, mask_rows, sink_rows, batch, tq):
    w = cache_k.shape[1]
    return pl.pallas_call(
        _attn_sample_kernel,
        grid=(batch,),
        in_specs=[pl.BlockSpec((tq, Q_W), lambda b: (b, 0)),
                  pl.BlockSpec((tq, 2 * KV_W), lambda b: (b, 0)),
                  pl.BlockSpec((None, w, KV_W), lambda b: (b, 0, 0)),
                  pl.BlockSpec((None, w, KV_W), lambda b: (b, 0, 0)),
                  pl.BlockSpec((w + tq, GROUP * tq), lambda b: (0, 0)),
                  pl.BlockSpec((N_KV_HEADS, 1, GROUP * tq), lambda b: (0, 0, 0))],
        out_specs=pl.BlockSpec((tq, Q_W), lambda b: (b, 0)),
        out_shape=jax.ShapeDtypeStruct((batch * tq, Q_W), f32),
        compiler_params=_cparams(1),
        name="attn_sample",
    )(q, kv_new, cache_k, cache_v, mask_rows, sink_rows)


def _dwconv(s_ref, row0, n_rows, w_ref, b_ref, y_ref, yrow0, rc):
    lead = HALO - (CONV_WIDTH - 1)
    offsets = range(lead, lead + CONV_WIDTH)

    def lane_body(lc, carry):
        c0 = pl.multiple_of(lc * LANES, LANES)
        cols = pl.ds(c0, LANES)
        bias = b_ref[:, cols]
        groups = rc // SUBLANES
        for t0 in range(0, n_rows, rc):
            acc = jnp.broadcast_to(bias[None], (groups, SUBLANES, LANES))
            for r in range(SUBLANES):
                offs = [o for o in offsets if o % SUBLANES == r]
                if not offs:
                    continue
                amax = max(offs) // SUBLANES
                z = s_ref[pl.ds(row0 + t0 + r, rc + SUBLANES * amax), cols].reshape(groups + amax, SUBLANES, LANES)
                for o in offs:
                    a = o // SUBLANES
                    acc = acc + z[a: a + groups] * w_ref[o - lead, :, cols][None]
            y_ref[pl.ds(yrow0 + t0, rc), cols] = acc.reshape(rc, LANES)
        return carry

    lax.fori_loop(0, CONV_CH // LANES, lane_body, 0)


def _mix_tail(y, attn, ga, gc, x, valid, lg_ref, lb_ref, wpw_ref, bpw_ref, wout_ref, nf_ref, wr_ref, br_ref):
    tm = y.shape[0]
    mu = jnp.mean(y, axis=-1, keepdims=True)
    d = y - mu
    var = jnp.mean(d * d, axis=-1, keepdims=True)
    yn = d * lax.rsqrt(var + NORM_EPS) * lg_ref[...] + lb_ref[...]
    act = (yn * jax.nn.sigmoid(yn)).astype(bf16)
    conv_out = jnp.dot(act, wpw_ref[...], preferred_element_type=f32) + bpw_ref[...]
    h = (ga.astype(f32) * attn.astype(f32) + gc.astype(f32) * conv_out).astype(bf16)
    x1 = x + jnp.dot(h, wout_ref[...], preferred_element_type=f32)
    xf = _rms(x1, nf_ref[...]).astype(bf16)
    logits = lax.dot_general(wr_ref[...], xf, (((1,), (1,)), ((), ())), preferred_element_type=f32) + br_ref[...]
    eidx = lax.broadcasted_iota(i32, (N_EXPERTS, tm), 0)
    routed = jnp.zeros((N_EXPERTS, tm), jnp.bool_)
    top_l, top_e = [], []
    l = logits
    for _ in range(TOP_K):
        m = jnp.max(l, axis=0, keepdims=True)
        idx = jnp.min(jnp.where(l == m, eidx, N_EXPERTS), axis=0, keepdims=True)
        sel = eidx == idx
        routed = routed | sel
        l = jnp.where(sel, -jnp.inf, l)
        top_l.append(m)
        top_e.append(idx)
    ex = [jnp.exp(t - top_l[0]) for t in top_l]
    tot = ex[0] + ex[1] + ex[2] + ex[3]
    gates = jnp.concatenate([e / tot for e in ex], axis=0)
    top_e = jnp.concatenate(top_e, axis=0)
    if valid is not None:
        routed = routed & valid
        top_e = jnp.where(valid, top_e, -1)
        gates = jnp.where(valid, gates, 0.0)
    counts = jnp.sum(jnp.where(routed, 1, 0).astype(i32), axis=1, keepdims=True)
    return x1, xf, top_e, gates, counts


def _mix_prompt_kernel(u_ref, halo_ref, attn_ref, ga_ref, gc_ref, x_ref, cw_ref, cb_ref, lg_ref, lb_ref,
                       wpw_ref, bpw_ref, wout_ref, nf_ref, wr_ref, br_ref,
                       x1_ref, xf_ref, te_ref, gt_ref, cnt_ref, s_scr, y_scr, *, tiles_per_seq):
    i = pl.program_id(0)
    first = (i % tiles_per_seq) == 0
    s_scr[0:HALO, :] = jnp.where(first, 0.0, halo_ref[...].astype(f32))
    s_scr[HALO:, :] = u_ref[...].astype(f32)
    _dwconv(s_scr, 0, TM_MIX, cw_ref, cb_ref, y_scr, 0, 128)
    x1, xf, top_e, gates, counts = _mix_tail(
        y_scr[...], attn_ref[...], ga_ref[...], gc_ref[...], x_ref[...], None,
        lg_ref, lb_ref, wpw_ref, bpw_ref, wout_ref, nf_ref, wr_ref, br_ref)
    x1_ref[...] = x1
    xf_ref[...] = xf
    te_ref[...] = top_e
    gt_ref[...] = gates
    cnt_ref[...] = counts


def _mix_sample_kernel(u_ref, hist_ref, attn_ref, ga_ref, gc_ref, x_ref, cw_ref, cb_ref, lg_ref, lb_ref,
                       wpw_ref, bpw_ref, wout_ref, nf_ref, wr_ref, br_ref,
                       x1_ref, xf_ref, te_ref, gt_ref, cnt_ref, s_scr, y_scr, *, batch, tq):
    n = batch * tq
    hist_rows = CONV_WIDTH - 1
    lead = HALO - hist_rows
    stride = HALO + tq
    s_scr[...] = jnp.zeros(s_scr.shape, f32)
    y_scr[...] = jnp.zeros(y_scr.shape, f32)
    as_conv_input = lambda a: a.astype(bf16).astype(f32)
    for b in range(batch):
        s_scr[b * stride + lead: b * stride + HALO, :] = as_conv_input(hist_ref[b])
        s_scr[b * stride + HALO: (b + 1) * stride, :] = as_conv_input(u_ref[b * tq:(b + 1) * tq, :])
        _dwconv(s_scr, b * stride, tq, cw_ref, cb_ref, y_scr, b * tq, tq)
    valid = lax.broadcasted_iota(i32, (1, TM_DISP), 1) < n
    x1, xf, top_e, gates, counts = _mix_tail(
        y_scr[...], attn_ref[...], ga_ref[...], gc_ref[...], x_ref[...], valid,
        lg_ref, lb_ref, wpw_ref, bpw_ref, wout_ref, nf_ref, wr_ref, br_ref)
    x1_ref[...] = x1
    xf_ref[...] = xf
    te_ref[...] = top_e
    gt_ref[...] = gates
    cnt_ref[0] = counts


def _mix_weight_specs(nidx):
    shapes = [(CONV_WIDTH, SUBLANES, CONV_CH), (1, CONV_CH), (1, CONV_CH), (1, CONV_CH), (CONV_CH, D_MODEL),
              (1, D_MODEL), (D_MODEL, D_MODEL), (1, D_MODEL), (N_EXPERTS, D_MODEL), (N_EXPERTS, 1)]
    return [pl.BlockSpec(s, functools.partial(lambda nd, i: (0,) * nd, len(s))) for s in shapes]


def _mix_out(n, nt):
    shapes = [jax.ShapeDtypeStruct((n, D_MODEL), f32), jax.ShapeDtypeStruct((n, D_MODEL), bf16),
              jax.ShapeDtypeStruct((TOP_K, n), i32), jax.ShapeDtypeStruct((TOP_K, n), f32),
              jax.ShapeDtypeStruct((nt, N_EXPERTS, 1), i32)]
    return shapes


def _mix_prompt(u, attn, ga, gc, x2d, weights, tiles_per_seq):
    n = x2d.shape[0]
    nt = n // TM_MIX
    row = lambda i: (i, 0)
    halo = lambda i: (jnp.maximum(i * (TM_MIX // HALO) - 1, 0), 0)
    tok = lambda i: (0, i)
    return pl.pallas_call(
        functools.partial(_mix_prompt_kernel, tiles_per_seq=tiles_per_seq),
        grid=(nt,),
        in_specs=[pl.BlockSpec((TM_MIX, CONV_CH), row), pl.BlockSpec((HALO, CONV_CH), halo),
                  pl.BlockSpec((TM_MIX, Q_W), row), pl.BlockSpec((TM_MIX, D_MODEL), row),
                  pl.BlockSpec((TM_MIX, D_MODEL), row), pl.BlockSpec((TM_MIX, D_MODEL), row)] + _mix_weight_specs(1),
        out_specs=[pl.BlockSpec((TM_MIX, D_MODEL), row), pl.BlockSpec((TM_MIX, D_MODEL), row),
                   pl.BlockSpec((TOP_K, TM_MIX), tok), pl.BlockSpec((TOP_K, TM_MIX), tok),
                   pl.BlockSpec((None, N_EXPERTS, 1), lambda i: (i, 0, 0))],
        out_shape=_mix_out(n, nt),
        scratch_shapes=[pltpu.VMEM((TM_MIX + HALO, CONV_CH), f32), pltpu.VMEM((TM_MIX, CONV_CH), f32)],
        compiler_params=_cparams(1),
        name="mix_prompt",
    )(u, u, attn, ga, gc, x2d, *weights)


def _mix_sample(u, hist, attn, ga, gc, x2d, weights, batch, tq):
    pad = lambda a: jnp.pad(a, ((0, TM_DISP - a.shape[0]), (0, 0)))
    return pl.pallas_call(
        functools.partial(_mix_sample_kernel, batch=batch, tq=tq),
        out_shape=_mix_out(TM_DISP, 1),
        scratch_shapes=[pltpu.VMEM((batch * (HALO + tq), CONV_CH), f32), pltpu.VMEM((TM_DISP, CONV_CH), f32)],
        compiler_params=pltpu.CompilerParams(vmem_limit_bytes=VMEM_LIMIT),
        name="mix_sample",
    )(u, hist, pad(attn), pad(ga), pad(gc), pad(x2d), *weights)


def _segment_dmas(tile, seg_rows, seg_local, seg_slot, make_copy, start):
    def body(e, carry):
        idx = tile * N_EXPERTS + e
        n = seg_rows[idx]
        loc = seg_local[idx]
        dst = seg_slot[idx]
        def chunk(sz):
            @pl.when((n & sz) != 0)
            def _():
                off = pl.multiple_of(n & ~(2 * sz - 1), ROW_ALIGN)
                cp = make_copy(pl.multiple_of(loc + off, ROW_ALIGN), pl.multiple_of(dst + off, ROW_ALIGN), sz)
                if start:
                    cp.start()
                else:
                    cp.wait()

        @pl.when(n >= SEG_SPLIT)
        def _():
            for sz in SEG_CHUNKS:
                if sz >= SEG_SPLIT:
                    chunk(sz)

        for sz in SEG_CHUNKS:
            if sz < SEG_SPLIT:
                chunk(sz)
        return carry

    lax.fori_loop(0, N_EXPERTS, body, 0)


def _dispatch_kernel(seg_slot, seg_local, seg_rows, tail_slot, tail_rows,
                     xa_ref, xb_ref, te_ref, lb_ref, lpos_ref, slots_hbm, buf, zbuf, sem, zsem,
                     *, n_prompt_tiles, n_tiles):
    i = pl.program_id(0)
    slot = i % 2
    x = jnp.where(i < n_prompt_tiles, xa_ref[...], xb_ref[...])
    te = te_ref[...]
    eidx = lax.broadcasted_iota(i32, (N_EXPERTS, TM_DISP), 0)
    hits = [te[k:k + 1, :] == eidx for k in range(TOP_K)]
    routed = hits[0] | hits[1] | hits[2] | hits[3]
    before = lax.broadcasted_iota(i32, (TM_DISP, TM_DISP), 0) < lax.broadcasted_iota(i32, (TM_DISP, TM_DISP), 1)
    rank = jnp.dot(jnp.where(routed, 1.0, 0.0).astype(bf16), jnp.where(before, 1.0, 0.0).astype(bf16),
                   preferred_element_type=f32)
    pos = lb_ref[...] + rank.astype(i32)
    lpos = []
    for k in range(TOP_K):
        p = jnp.sum(jnp.where(hits[k], pos, 0), axis=0, keepdims=True)
        lpos.append(jnp.where(te[k:k + 1, :] >= 0, p, -1))
    lpos_ref[...] = jnp.concatenate(lpos, axis=0)
    ridx = lax.broadcasted_iota(i32, (LOCAL_ROWS, TM_DISP), 0)
    onehot = (ridx == lpos[0]) | (ridx == lpos[1]) | (ridx == lpos[2]) | (ridx == lpos[3])
    srt = jnp.dot(jnp.where(onehot, 1.0, 0.0).astype(bf16), x, preferred_element_type=f32)
    half = D_MODEL // 2
    lo = lax.shift_right_logical(lax.bitcast_convert_type(srt[:, :half], i32), 16)
    hi = lax.bitcast_convert_type(srt[:, half:], i32) & jnp.int32(-65536)
    buf[slot] = hi | lo

    def copy_of(s):
        def make(loc, dst, sz):
            return pltpu.make_async_copy(buf.at[s, pl.ds(loc, sz), :], slots_hbm.at[pl.ds(dst, sz), :], sem.at[s])
        return make

    _segment_dmas(i, seg_rows, seg_local, seg_slot, copy_of(slot), True)

    @pl.when(i > 0)
    def _():
        _segment_dmas(i - 1, seg_rows, seg_local, seg_slot, copy_of(1 - slot), False)

    @pl.when(i == n_tiles - 1)
    def _():
        _segment_dmas(i, seg_rows, seg_local, seg_slot, copy_of(slot), False)
        zbuf[...] = jnp.zeros(zbuf.shape, i32)
        for start in (True, False):
            def body(e, carry):
                n = tail_rows[e]
                dst = tail_slot[e]
                for sz in TAIL_CHUNKS:
                    @pl.when((n & sz) != 0)
                    def _():
                        off = pl.multiple_of(n & ~(2 * sz - 1), ROW_ALIGN)
                        cp = pltpu.make_async_copy(zbuf.at[pl.ds(0, sz), :],
                                                   slots_hbm.at[pl.ds(pl.multiple_of(dst + off, ROW_ALIGN), sz), :], zsem)
                        if start:
                            cp.start()
                        else:
                            cp.wait()
                return carry
            lax.fori_loop(0, N_EXPERTS, body, 0)


def _dispatch(xf_p, xf_s, top_e, lbase_v, meta, n_slots):
    n_prompt_tiles = xf_p.shape[0] // TM_DISP
    n_tiles = n_prompt_tiles + 1
    half = D_MODEL // 2
    grid_spec = pltpu.PrefetchScalarGridSpec(
        num_scalar_prefetch=5,
        grid=(n_tiles,),
        in_specs=[pl.BlockSpec((TM_DISP, D_MODEL), lambda i, *_: (jnp.minimum(i, n_prompt_tiles - 1), 0)),
                  pl.BlockSpec((TM_DISP, D_MODEL), lambda i, *_: (0, 0)),
                  pl.BlockSpec((TOP_K, TM_DISP), lambda i, *_: (0, i)),
                  pl.BlockSpec((None, N_EXPERTS, 1), lambda i, *_: (i, 0, 0))],
        out_specs=[pl.BlockSpec((TOP_K, TM_DISP), lambda i, *_: (0, i)),
                   pl.BlockSpec(memory_space=pl.ANY)],
        scratch_shapes=[pltpu.VMEM((2, LOCAL_ROWS, half), i32), pltpu.VMEM((max(TAIL_CHUNKS), half), i32),
                        pltpu.SemaphoreType.DMA((2,)), pltpu.SemaphoreType.DMA(())],
    )
    return pl.pallas_call(
        functools.partial(_dispatch_kernel, n_prompt_tiles=n_prompt_tiles, n_tiles=n_tiles),
        grid_spec=grid_spec,
        out_shape=[jax.ShapeDtypeStruct((TOP_K, n_tiles * TM_DISP), i32),
                   jax.ShapeDtypeStruct((n_slots, half), i32)],
        compiler_params=_cparams(1),
        name="dispatch",
    )(*meta, xf_p, xf_s, top_e, lbase_v)


def _combine_kernel(seg_slot, seg_local, seg_rows,
                    lpos_ref, gate_ref, xa_ref, xb_ref, nfin_ref, y_hbm, ya_ref, yb_ref, buf, sem,
                    *, n_prompt_tiles, n_tiles):
    i = pl.program_id(0)
    slot = i % 2

    def copy_of(s):
        def make(loc, src, sz):
            return pltpu.make_async_copy(y_hbm.at[pl.ds(src, sz), :], buf.at[s, pl.ds(loc, sz), :], sem.at[s])
        return make

    @pl.when(i == 0)
    def _():
        buf[...] = jnp.zeros(buf.shape, f32)
        _segment_dmas(0, seg_rows, seg_local, seg_slot, copy_of(0), True)

    @pl.when(i + 1 < n_tiles)
    def _():
        _segment_dmas(i + 1, seg_rows, seg_local, seg_slot, copy_of(1 - slot), True)

    _segment_dmas(i, seg_rows, seg_local, seg_slot, copy_of(slot), False)

    lpos = lpos_ref[...]
    gate = gate_ref[...]
    cidx = lax.broadcasted_iota(i32, (TM_DISP, LOCAL_ROWS), 1)
    wmat = jnp.zeros((TM_DISP, LOCAL_ROWS), f32)
    for k in range(TOP_K):
        wmat = jnp.where(cidx == lpos[:, k:k + 1], gate[:, k:k + 1], wmat)
    moe = jnp.dot(wmat.astype(bf16), buf[slot].astype(bf16), preferred_element_type=f32)
    x1 = jnp.where(i < n_prompt_tiles, xa_ref[...], xb_ref[...])
    y = _rms(x1 + moe, nfin_ref[...])

    @pl.when(i < n_prompt_tiles)
    def _():
        ya_ref[...] = y

    @pl.when(i >= n_prompt_tiles)
    def _():
        yb_ref[...] = y


def _combine(y_slots, lpos_t, gate_t, x1_p, x1_s, norm_final, meta):
    n_prompt_tiles = x1_p.shape[0] // TM_DISP
    n_tiles = n_prompt_tiles + 1
    last = n_prompt_tiles - 1
    grid_spec = pltpu.PrefetchScalarGridSpec(
        num_scalar_prefetch=3,
        grid=(n_tiles,),
        in_specs=[pl.BlockSpec((TM_DISP, TOP_K), lambda i, *_: (i, 0)),
                  pl.BlockSpec((TM_DISP, TOP_K), lambda i, *_: (i, 0)),
                  pl.BlockSpec((TM_DISP, D_MODEL), lambda i, *_: (jnp.minimum(i, last), 0)),
                  pl.BlockSpec((TM_DISP, D_MODEL), lambda i, *_: (0, 0)),
                  pl.BlockSpec((1, D_MODEL), lambda i, *_: (0, 0)),
                  pl.BlockSpec(memory_space=pl.ANY)],
        out_specs=[pl.BlockSpec((TM_DISP, D_MODEL), lambda i, *_: (jnp.minimum(i, last), 0)),
                   pl.BlockSpec((TM_DISP, D_MODEL), lambda i, *_: (0, 0))],
        scratch_shapes=[pltpu.VMEM((2, LOCAL_ROWS, D_MODEL), f32), pltpu.SemaphoreType.DMA((2,))],
    )
    return pl.pallas_call(
        functools.partial(_combine_kernel, n_prompt_tiles=n_prompt_tiles, n_tiles=n_tiles),
        grid_spec=grid_spec,
        out_shape=[jax.ShapeDtypeStruct(x1_p.shape, f32), jax.ShapeDtypeStruct(x1_s.shape, f32)],
        compiler_params=_cparams(1),
        name="combine",
    )(*meta, lpos_t, gate_t, x1_p, x1_s, norm_final, y_slots)


ROW_DMA_PRIORITY = 1


def _experts_kernel(first_row, n_blk, xs_hbm, wgu_ref, bgu_ref, wd_ref, bd_ref, y_hbm,
                    wgu_bf, wd_bf, xbuf, ybuf, xsem, ysem):
    e = pl.program_id(0)
    nb = n_blk[e]
    row0 = first_row[e]

    def rows(j):
        return pl.ds(pl.multiple_of(row0 + j * R_BLK, R_BLK), R_BLK)

    def x_copy(j, slot):
        return pltpu.make_async_copy(xs_hbm.at[rows(j), :], xbuf.at[slot], xsem.at[slot])

    def y_copy(j, slot):
        return pltpu.make_async_copy(ybuf.at[slot], y_hbm.at[rows(j), :], ysem.at[slot])

    @pl.when(nb > 0)
    def _():
        x_copy(0, 0).start(priority=ROW_DMA_PRIORITY)
        wgu_bf[...] = wgu_ref[...].astype(bf16)
        wd_bf[...] = wd_ref[...].astype(bf16)

        def body(j, carry):
            slot = j % 2

            @pl.when(j + 1 < nb)
            def _():
                x_copy(j + 1, 1 - slot).start(priority=ROW_DMA_PRIORITY)

            x_copy(j, slot).wait()

            @pl.when(j >= 2)
            def _():
                y_copy(j - 2, slot).wait()

            pk = xbuf[slot]
            lo = lax.bitcast_convert_type(lax.shift_left(pk, 16), f32).astype(bf16)
            hi = lax.bitcast_convert_type(pk & jnp.int32(-65536), f32).astype(bf16)
            x = jnp.concatenate([lo, hi], axis=1)
            gu = jnp.dot(x, wgu_bf[...], preferred_element_type=f32) + bgu_ref[...]
            g = jnp.minimum(gu[:, :D_FF], SWIGLU_LIMIT)
            up = jnp.clip(gu[:, D_FF:], -SWIGLU_LIMIT, SWIGLU_LIMIT)
            h = (up + 1.0) * (g * jax.nn.sigmoid(SWIGLU_ALPHA * g))
            ybuf[slot] = jnp.dot(h.astype(bf16), wd_bf[...], preferred_element_type=f32) + bd_ref[...]
            y_copy(j, slot).start(priority=ROW_DMA_PRIORITY)
            return carry

        lax.fori_loop(0, nb, body, 0)

        @pl.when(nb >= 2)
        def _():
            y_copy(nb - 2, nb % 2).wait()

        y_copy(nb - 1, (nb - 1) % 2).wait()


def _experts(x_slots, w_gu, b_gu, w_down, b_down, first_row, n_blk):
    half = D_MODEL // 2
    wsel = lambda e, *_: (e, 0, 0)
    grid_spec = pltpu.PrefetchScalarGridSpec(
        num_scalar_prefetch=2,
        grid=(N_EXPERTS,),
        in_specs=[pl.BlockSpec(memory_space=pl.ANY),
                  pl.BlockSpec((None, D_MODEL, 2 * D_FF), wsel),
                  pl.BlockSpec((None, 1, 2 * D_FF), wsel),
                  pl.BlockSpec((None, D_FF, D_MODEL), wsel),
                  pl.BlockSpec((None, 1, D_MODEL), wsel)],
        out_specs=pl.BlockSpec(memory_space=pl.ANY),
        scratch_shapes=[pltpu.VMEM((D_MODEL, 2 * D_FF), bf16), pltpu.VMEM((D_FF, D_MODEL), bf16),
                        pltpu.VMEM((2, R_BLK, half), i32), pltpu.VMEM((2, R_BLK, D_MODEL), f32),
                        pltpu.SemaphoreType.DMA((2,)), pltpu.SemaphoreType.DMA((2,))],
    )
    return pl.pallas_call(
        _experts_kernel,
        grid_spec=grid_spec,
        out_shape=jax.ShapeDtypeStruct((x_slots.shape[0], D_MODEL), f32),
        compiler_params=_cparams(1),
        name="experts",
    )(first_row, n_blk, x_slots, w_gu, b_gu.reshape(N_EXPERTS, 1, 2 * D_FF), w_down,
      b_down.reshape(N_EXPERTS, 1, D_MODEL))


def _slot_layout(counts):
    seg_rows = (counts + ROW_ALIGN - 1) // ROW_ALIGN * ROW_ALIGN
    seg_local = jnp.cumsum(seg_rows, axis=1) - seg_rows
    total = jnp.sum(seg_rows, axis=0)
    region = (total + R_BLK - 1) // R_BLK * R_BLK
    region_end = jnp.cumsum(region)
    ebase = region_end - region
    seg_slot = ebase[None, :] + jnp.cumsum(seg_rows, axis=0) - seg_rows
    flat = lambda a: a.reshape(-1).astype(i32)
    return dict(seg_slot=flat(seg_slot), seg_local=flat(seg_local), seg_rows=flat(seg_rows),
                tail_slot=flat(ebase + total), tail_rows=flat(region - total),
                first_row=flat(ebase), n_blk=flat(region // R_BLK),
                lbase_v=seg_local.astype(i32)[:, :, None])


def kernel(x_prompt, x_sample, cache_k, cache_v, cache_conv, norm_mix, w_in, attn_sinks, conv_dw_w, conv_dw_b,
           conv_ln_g, conv_ln_b, w_pw, b_pw, w_out, norm_ffn, w_router, b_router, w_gu, b_gu, w_down, b_down,
           norm_final):
    depth = norm_mix.shape[0]
    assert depth == 1, "one layer per step"
    batch, seq, _ = x_prompt.shape
    dbatch, dseq, _ = x_sample.shape
    window = cache_k.shape[2]
    n_p, n_s = batch * seq, dbatch * dseq
    assert seq % TM_PROJ == 0 and seq % (ATTN_CHUNKS * CHUNK) == 0 and seq >= SPAN and n_s <= TM_DISP
    assert n_p % TM_DISP == 0 and TM_DISP % TM_MIX == 0 and seq % TM_MIX == 0
    l = 0
    row = lambda a: a.reshape(1, -1)

    w_in_bf = w_in[l].astype(bf16)
    tabs_p = _rope_tables(jnp.arange(seq, dtype=f32))
    tabs_s = _rope_tables(PAST_LEN + jnp.arange(dseq, dtype=f32))
    tabs_s = tuple(jnp.tile(t, (dbatch, 1)) for t in tabs_s)
    xp2 = x_prompt.reshape(n_p, D_MODEL)
    xs2 = x_sample.reshape(n_s, D_MODEL)
    q_p, kv_p, u_p, ga_p, gc_p, ut_p = _in_proj(xp2, row(norm_mix[l]), w_in_bf, tabs_p, TM_PROJ, seq // TM_PROJ, bf16)
    q_s, kv_s, u_s, ga_s, gc_s, _ = _in_proj(xs2, row(norm_mix[l]), w_in_bf, tabs_s, n_s, 1, f32)

    sinks = attn_sinks[l].astype(f32).reshape(N_KV_HEADS, 1, GROUP)
    attn_p = _attn_prompt(q_p, kv_p, jnp.repeat(sinks, CHUNK, axis=2), batch, seq)
    qpos = PAST_LEN + np.arange(dseq)
    kpos = PAST_LEN - window + np.arange(window + dseq)
    qch, kch = qpos // CHUNK, kpos // CHUNK
    mask = (kch[None, :] >= qch[:, None] - WINDOW_CHUNKS) & (kch[None, :] <= qch[:, None]) & (kpos[None, :] >= 0)
    mask_rows = jnp.asarray(np.tile(mask.astype(np.float32).T, (1, GROUP)))
    ck = cache_k[l].reshape(dbatch, window, KV_W)
    cv = cache_v[l].reshape(dbatch, window, KV_W)
    attn_s = _attn_sample(q_s, kv_s, ck, cv, mask_rows, jnp.repeat(sinks, dseq, axis=2), dbatch, dseq)

    conv_w = jnp.broadcast_to(conv_dw_w[l][:, None, :], (CONV_WIDTH, SUBLANES, CONV_CH))
    mix_w = (conv_w, row(conv_dw_b[l]), row(conv_ln_g[l]),
             row(conv_ln_b[l]), w_pw[l].astype(bf16), row(b_pw[l]), w_out[l].astype(bf16), row(norm_ffn[l]),
             w_router[l].T.astype(bf16), b_router[l].astype(f32).reshape(N_EXPERTS, 1))
    x1_p, xf_p, te_p, gt_p, cnt_p = _mix_prompt(u_p, attn_p, ga_p, gc_p, xp2, mix_w, seq // TM_MIX)
    x1_s, xf_s, te_s, gt_s, cnt_s = _mix_sample(u_s, cache_conv[l], attn_s, ga_s, gc_s, xs2, mix_w, dbatch, dseq)

    n_tiles = n_p // TM_DISP + 1
    n_all = n_tiles * TM_DISP
    n_blocks = -(-(TOP_K * n_all + (ROW_ALIGN - 1) * N_EXPERTS * n_tiles + N_EXPERTS * (R_BLK - ROW_ALIGN)) // R_BLK)
    cnt_p = cnt_p.reshape(n_p // TM_DISP, TM_DISP // TM_MIX, N_EXPERTS).sum(axis=1)
    lay = _slot_layout(jnp.concatenate([cnt_p, cnt_s[:, :, 0]], axis=0))
    top_e = jnp.concatenate([te_p, te_s], axis=1)
    gates = jnp.concatenate([gt_p, gt_s], axis=1)
    lpos, x_slots = _dispatch(xf_p, xf_s, top_e, lay["lbase_v"],
                              (lay["seg_slot"], lay["seg_local"], lay["seg_rows"], lay["tail_slot"], lay["tail_rows"]),
                              n_blocks * R_BLK)
    y_slots = _experts(x_slots, w_gu[l], b_gu[l], w_down[l], b_down[l], lay["first_row"], lay["n_blk"])
    y_p, y_s = _combine(y_slots, lpos.T, gates.T, x1_p, x1_s, row(norm_final),
                        (lay["seg_slot"], lay["seg_local"], lay["seg_rows"]))

    y_prompt = y_p.reshape(batch, seq, D_MODEL)
    y_sample = y_s[:n_s].reshape(dbatch, dseq, D_MODEL)
    kv_tail = kv_p.reshape(batch, seq, 2 * KV_W)[:, seq - window:]
    new_k_p = kv_tail[:, :, :KV_W].reshape(1, batch, window, N_KV_HEADS, HEAD_DIM)
    new_v_p = kv_tail[:, :, KV_W:].reshape(1, batch, window, N_KV_HEADS, HEAD_DIM)
    tiles_per_seq = seq // TM_PROJ
    new_conv_p = ut_p.reshape(batch, tiles_per_seq, HALO, CONV_CH)[:, -1, HALO - (CONV_WIDTH - 1):][None]
    kv_s4 = kv_s.reshape(dbatch, dseq, 2, N_KV_HEADS, HEAD_DIM)
    new_k_s = jnp.concatenate([cache_k[l], kv_s4[:, :, 0]], axis=1)[:, -window:][None]
    new_v_s = jnp.concatenate([cache_v[l], kv_s4[:, :, 1]], axis=1)[:, -window:][None]
    new_conv_s = jnp.concatenate([cache_conv[l], u_s.reshape(dbatch, dseq, CONV_CH)], axis=1)[:, -(CONV_WIDTH - 1):][None]
    return (y_prompt, y_sample, new_k_p, new_v_p, new_conv_p, new_k_s, new_v_s, new_conv_s)
```

```python
import functools

import numpy as np
import jax
import jax.numpy as jnp
from jax import lax
from jax.experimental import pallas as pl
from jax.experimental.pallas import tpu as pltpu

f32 = jnp.float32
bf16 = jnp.bfloat16
i32 = jnp.int32

D_MODEL = 1024
PAST_LEN = 1024
CHUNK = 64
N_HEADS = 16
N_KV_HEADS = 2
HEAD_DIM = 64
GROUP = N_HEADS // N_KV_HEADS
ROT_DIM = HEAD_DIM // 4
ROPE_THETA = 500000.0
WINDOW = 128
WINDOW_CHUNKS = WINDOW // CHUNK
SPAN = (WINDOW_CHUNKS + 1) * CHUNK
CONV_CH = D_MODEL
CONV_WIDTH = 31
N_EXPERTS = 32
TOP_K = 4
D_FF = D_MODEL
SWIGLU_LIMIT = 7.0
SWIGLU_ALPHA = 1.702
NORM_EPS = 1e-5
NEG_INF = -1e30
Q_W = N_HEADS * HEAD_DIM
KV_W = N_KV_HEADS * HEAD_DIM
IN_COLS = Q_W + 2 * KV_W + 2 * CONV_CH + 2 * D_MODEL
Q_SCALE = HEAD_DIM ** -0.5

LANES = 128
SUBLANES = 8
VMEM_LIMIT = 56 * 1024 * 1024

HALO = 32
TM_PROJ = 512
TM_MIX = 256
TM_DISP = 512
ATTN_CHUNKS = 4
ROW_ALIGN = SUBLANES
R_BLK = 512
LOCAL_ROWS = -(-(TOP_K * TM_DISP + N_EXPERTS * (ROW_ALIGN - 1)) // LANES) * LANES
SEG_CHUNKS = tuple(2 ** p for p in range(int(np.log2(TM_DISP)), int(np.log2(ROW_ALIGN)) - 1, -1))
SEG_SPLIT = 2 * TOP_K * TM_DISP // N_EXPERTS
WAIT_CHUNKS = tuple(2 ** p for p in range(int(np.log2(LOCAL_ROWS)), int(np.log2(ROW_ALIGN)) - 1, -1))
TAIL_CHUNKS = tuple(c for c in SEG_CHUNKS if c < R_BLK)


def _cparams(n_axes):
    return pltpu.CompilerParams(dimension_semantics=("arbitrary",) * n_axes,
                                vmem_limit_bytes=VMEM_LIMIT)


def _rms(x, g):
    return x * lax.rsqrt(jnp.mean(x * x, axis=-1, keepdims=True) + NORM_EPS) * g


def _inproj_kernel(x_ref, g_ref, w_ref, cos_ref, sa_ref, sb_ref,
                   q_ref, kv_ref, u_ref, ga_ref, gc_ref, ut_ref):
    xn = _rms(x_ref[...], g_ref[...]).astype(bf16)
    cos = cos_ref[...]
    sa = sa_ref[...]
    sb = sb_ref[...]

    def rope(t):
        return t * cos + pltpu.roll(t, LANES - ROT_DIM // 2, 1) * sa + pltpu.roll(t, ROT_DIM // 2, 1) * sb

    def proj(c0, n):
        return jnp.dot(xn, w_ref[:, c0:c0 + n], preferred_element_type=f32)

    q = proj(0, Q_W)
    for c in range(Q_W // LANES):
        q_ref[:, c * LANES:(c + 1) * LANES] = (rope(q[:, c * LANES:(c + 1) * LANES]) * Q_SCALE).astype(bf16)
    kv = proj(Q_W, 2 * KV_W)
    kv_ref[:, :KV_W] = rope(kv[:, :KV_W])
    kv_ref[:, KV_W:] = kv[:, KV_W:]
    c0 = Q_W + 2 * KV_W
    u = proj(c0, CONV_CH) * jax.nn.sigmoid(proj(c0 + CONV_CH, CONV_CH))
    u_ref[...] = u.astype(u_ref.dtype)
    ut_ref[...] = u[u.shape[0] - HALO:, :]
    c0 += 2 * CONV_CH
    ga_ref[...] = jax.nn.sigmoid(proj(c0, D_MODEL)).astype(ga_ref.dtype)
    gc_ref[...] = jax.nn.sigmoid(proj(c0 + D_MODEL, D_MODEL)).astype(gc_ref.dtype)


def _rope_tables(pos):
    half = ROT_DIM // 2
    inv = ROPE_THETA ** (-jnp.arange(half, dtype=f32) * 2.0 / ROT_DIM)
    ang = pos[:, None] * inv[None, :]
    cos, sin = jnp.cos(ang), jnp.sin(ang)
    t = pos.shape[0]
    ones = jnp.ones((t, HEAD_DIM - ROT_DIM), f32)
    zeros = jnp.zeros((t, HEAD_DIM - ROT_DIM), f32)
    zh = jnp.zeros((t, half), f32)
    c = jnp.concatenate([cos, cos, ones], axis=1)
    a = jnp.concatenate([-sin, zh, zeros], axis=1)
    b = jnp.concatenate([zh, sin, zeros], axis=1)
    rep = LANES // HEAD_DIM
    return jnp.tile(c, (1, rep)), jnp.tile(a, (1, rep)), jnp.tile(b, (1, rep))


def _in_proj(x2d, norm_g, w_bf, tables, tm, tiles_per_seq, act_dtype):
    n = x2d.shape[0]
    nt = n // tm
    row = lambda i: (i, 0)
    const = lambda i: (0, 0)
    tab = lambda i: (i % tiles_per_seq, 0)
    return pl.pallas_call(
        _inproj_kernel,
        grid=(nt,),
        in_specs=[pl.BlockSpec((tm, D_MODEL), row),
                  pl.BlockSpec((1, D_MODEL), const),
                  pl.BlockSpec((D_MODEL, IN_COLS), const, pipeline_mode=pl.Buffered(1)),
                  pl.BlockSpec((tm, LANES), tab),
                  pl.BlockSpec((tm, LANES), tab),
                  pl.BlockSpec((tm, LANES), tab)],
        out_specs=[pl.BlockSpec((tm, Q_W), row),
                   pl.BlockSpec((tm, 2 * KV_W), row),
                   pl.BlockSpec((tm, CONV_CH), row),
                   pl.BlockSpec((tm, D_MODEL), row),
                   pl.BlockSpec((tm, D_MODEL), row),
                   pl.BlockSpec((None, HALO, CONV_CH), lambda i: (i, 0, 0))],
        out_shape=[jax.ShapeDtypeStruct((n, Q_W), bf16),
                   jax.ShapeDtypeStruct((n, 2 * KV_W), f32),
                   jax.ShapeDtypeStruct((n, CONV_CH), act_dtype),
                   jax.ShapeDtypeStruct((n, D_MODEL), act_dtype),
                   jax.ShapeDtypeStruct((n, D_MODEL), act_dtype),
                   jax.ShapeDtypeStruct((nt, HALO, CONV_CH), f32)],
        compiler_params=_cparams(1),
        name="in_proj",
    )(x2d, norm_g, w_bf, *tables)


def _attn_group(qg, kg, vg, mask_t, sink):
    s = lax.dot_general(kg, qg, (((1,), (1,)), ((), ())), preferred_element_type=f32)
    s = jnp.where(mask_t, s, NEG_INF)
    m = jnp.maximum(jnp.max(s, axis=0, keepdims=True), sink)
    p = jnp.exp(s - m)
    denom = jnp.sum(p, axis=0, keepdims=True) + jnp.exp(sink - m)
    pn = (p / denom).astype(bf16)
    return lax.dot_general(pn, vg, (((0,), (0,)), ((), ())), preferred_element_type=f32)


def _heads_to_rows(q, g):
    base = g * GROUP * HEAD_DIM
    return jnp.concatenate([q[:, base + h * HEAD_DIM: base + (h + 1) * HEAD_DIM] for h in range(GROUP)], axis=0)


def _rows_to_heads(o, tq):
    return [o[h * tq:(h + 1) * tq, :] for h in range(GROUP)]


def _attn_prompt_kernel(q_ref, kv_ref, sink_ref, o_ref):
    j = pl.program_id(1)
    for c in range(ATTN_CHUNKS):
        n = j * ATTN_CHUNKS + c
        first = jnp.maximum(n - WINDOW_CHUNKS, 0)
        win = kv_ref[pl.ds(pl.multiple_of(first * CHUNK, CHUNK), SPAN), :]
        mask = lax.broadcasted_iota(i32, (SPAN, GROUP * CHUNK), 0) < (n - first + 1) * CHUNK
        q = q_ref[c * CHUNK:(c + 1) * CHUNK, :]
        outs = []
        for g in range(N_KV_HEADS):
            kg = win[:, g * HEAD_DIM:(g + 1) * HEAD_DIM].astype(bf16)
            vg = win[:, KV_W + g * HEAD_DIM: KV_W + (g + 1) * HEAD_DIM].astype(bf16)
            o = _attn_group(_heads_to_rows(q, g), kg, vg, mask, sink_ref[g])
            outs += _rows_to_heads(o, CHUNK)
        o_ref[c * CHUNK:(c + 1) * CHUNK, :] = jnp.concatenate(outs, axis=1).astype(bf16)


def _attn_prompt(q, kv, sink_rows, batch, seq):
    qrows = ATTN_CHUNKS * CHUNK
    steps = seq // qrows
    return pl.pallas_call(
        _attn_prompt_kernel,
        grid=(batch, steps),
        in_specs=[pl.BlockSpec((qrows, Q_W), lambda b, j: (b * steps + j, 0)),
                  pl.BlockSpec((seq, 2 * KV_W), lambda b, j: (b, 0)),
                  pl.BlockSpec((N_KV_HEADS, 1, GROUP * CHUNK), lambda b, j: (0, 0, 0))],
        out_specs=pl.BlockSpec((qrows, Q_W), lambda b, j: (b * steps + j, 0)),
        out_shape=jax.ShapeDtypeStruct((batch * seq, Q_W), bf16),
        compiler_params=_cparams(2),
        name="attn_prompt",
    )(q, kv, sink_rows)


def _attn_sample_kernel(q_ref, kvn_ref, ck_ref, cv_ref, mask_ref, sink_ref, o_ref):
    tq = q_ref.shape[0]
    q = q_ref[...]
    kvn = kvn_ref[...]
    ck = ck_ref[...]
    cv = cv_ref[...]
    mask = mask_ref[...] > 0.5
    outs = []
    for g in range(N_KV_HEADS):
        sl = slice(g * HEAD_DIM, (g + 1) * HEAD_DIM)
        kg = jnp.concatenate([ck[:, sl], kvn[:, sl]], axis=0).astype(bf16)
        vg = jnp.concatenate([cv[:, sl], kvn[:, KV_W + g * HEAD_DIM: KV_W + (g + 1) * HEAD_DIM]], axis=0).astype(bf16)
        o = _attn_group(_heads_to_rows(q, g), kg, vg, mask, sink_ref[g])
        outs += _rows_to_heads(o, tq)
    o_ref[...] = jnp.concatenate(outs, axis=1)


def _attn_sample(q, kv_new, cache_k, cache_v, mask_rows, sink_rows, batch, tq):
    w = cache_k.shape[1]
    return pl.pallas_call(
        _attn_sample_kernel,
        grid=(batch,),
        in_specs=[pl.BlockSpec((tq, Q_W), lambda b: (b, 0)),
                  pl.BlockSpec((tq, 2 * KV_W), lambda b: (b, 0)),
                  pl.BlockSpec((None, w, KV_W), lambda b: (b, 0, 0)),
                  pl.BlockSpec((None, w, KV_W), lambda b: (b, 0, 0)),
                  pl.BlockSpec((w + tq, GROUP * tq), lambda b: (0, 0)),
                  pl.BlockSpec((N_KV_HEADS, 1, GROUP * tq), lambda b: (0, 0, 0))],
        out_specs=pl.BlockSpec((tq, Q_W), lambda b: (b, 0)),
        out_shape=jax.ShapeDtypeStruct((batch * tq, Q_W), f32),
        compiler_params=_cparams(1),
        name="attn_sample",
    )(q, kv_new, cache_k, cache_v, mask_rows, sink_rows)


def _dwconv(s_ref, row0, n_rows, w_ref, b_ref, y_ref, yrow0, rc):
    lead = HALO - (CONV_WIDTH - 1)
    offsets = range(lead, lead + CONV_WIDTH)

    def lane_body(lc, carry):
        c0 = pl.multiple_of(lc * LANES, LANES)
        cols = pl.ds(c0, LANES)
        bias = b_ref[:, cols]
        groups = rc // SUBLANES
        for t0 in range(0, n_rows, rc):
            acc = jnp.broadcast_to(bias[None], (groups, SUBLANES, LANES))
            for r in range(SUBLANES):
                offs = [o for o in offsets if o % SUBLANES == r]
                if not offs:
                    continue
                amax = max(offs) // SUBLANES
                z = s_ref[pl.ds(row0 + t0 + r, rc + SUBLANES * amax), cols].reshape(groups + amax, SUBLANES, LANES)
                for o in offs:
                    a = o // SUBLANES
                    acc = acc + z[a: a + groups] * w_ref[o - lead, :, cols][None]
            y_ref[pl.ds(yrow0 + t0, rc), cols] = acc.reshape(rc, LANES)
        return carry

    lax.fori_loop(0, CONV_CH // LANES, lane_body, 0)


def _mix_tail(y, attn, ga, gc, x, valid, lg_ref, lb_ref, wpw_ref, bpw_ref, wout_ref, nf_ref, wr_ref, br_ref):
    tm = y.shape[0]
    mu = jnp.mean(y, axis=-1, keepdims=True)
    d = y - mu
    var = jnp.mean(d * d, axis=-1, keepdims=True)
    yn = d * lax.rsqrt(var + NORM_EPS) * lg_ref[...] + lb_ref[...]
    act = (yn * jax.nn.sigmoid(yn)).astype(bf16)
    conv_out = jnp.dot(act, wpw_ref[...], preferred_element_type=f32) + bpw_ref[...]
    h = (ga.astype(f32) * attn.astype(f32) + gc.astype(f32) * conv_out).astype(bf16)
    x1 = x + jnp.dot(h, wout_ref[...], preferred_element_type=f32)
    xf = _rms(x1, nf_ref[...]).astype(bf16)
    logits = lax.dot_general(wr_ref[...], xf, (((1,), (1,)), ((), ())), preferred_element_type=f32) + br_ref[...]
    eidx = lax.broadcasted_iota(i32, (N_EXPERTS, tm), 0)
    routed = jnp.zeros((N_EXPERTS, tm), jnp.bool_)
    top_l, top_e = [], []
    l = logits
    for _ in range(TOP_K):
        m = jnp.max(l, axis=0, keepdims=True)
        idx = jnp.min(jnp.where(l == m, eidx, N_EXPERTS), axis=0, keepdims=True)
        sel = eidx == idx
        routed = routed | sel
        l = jnp.where(sel, -jnp.inf, l)
        top_l.append(m)
        top_e.append(idx)
    ex = [jnp.exp(t - top_l[0]) for t in top_l]
    tot = ex[0] + ex[1] + ex[2] + ex[3]
    gates = jnp.concatenate([e / tot for e in ex], axis=0)
    top_e = jnp.concatenate(top_e, axis=0)
    if valid is not None:
        routed = routed & valid
        top_e = jnp.where(valid, top_e, -1)
        gates = jnp.where(valid, gates, 0.0)
    counts = jnp.sum(jnp.where(routed, 1, 0).astype(i32), axis=1, keepdims=True)
    return x1, xf, top_e, gates, counts


def _mix_prompt_kernel(u_ref, halo_ref, attn_ref, ga_ref, gc_ref, x_ref, cw_ref, cb_ref, lg_ref, lb_ref,
                       wpw_ref, bpw_ref, wout_ref, nf_ref, wr_ref, br_ref,
                       x1_ref, xf_ref, te_ref, gt_ref, cnt_ref, s_scr, y_scr, *, tiles_per_seq):
    i = pl.program_id(0)
    first = (i % tiles_per_seq) == 0
    s_scr[0:HALO, :] = jnp.where(first, 0.0, halo_ref[...].astype(f32))
    s_scr[HALO:, :] = u_ref[...].astype(f32)
    _dwconv(s_scr, 0, TM_MIX, cw_ref, cb_ref, y_scr, 0, 128)
    x1, xf, top_e, gates, counts = _mix_tail(
        y_scr[...], attn_ref[...], ga_ref[...], gc_ref[...], x_ref[...], None,
        lg_ref, lb_ref, wpw_ref, bpw_ref, wout_ref, nf_ref, wr_ref, br_ref)
    x1_ref[...] = x1
    xf_ref[...] = xf
    te_ref[...] = top_e
    gt_ref[...] = gates
    cnt_ref[...] = counts


def _mix_sample_kernel(u_ref, hist_ref, attn_ref, ga_ref, gc_ref, x_ref, cw_ref, cb_ref, lg_ref, lb_ref,
                       wpw_ref, bpw_ref, wout_ref, nf_ref, wr_ref, br_ref,
                       x1_ref, xf_ref, te_ref, gt_ref, cnt_ref, s_scr, y_scr, *, batch, tq):
    n = batch * tq
    hist_rows = CONV_WIDTH - 1
    lead = HALO - hist_rows
    stride = HALO + tq
    s_scr[...] = jnp.zeros(s_scr.shape, f32)
    y_scr[...] = jnp.zeros(y_scr.shape, f32)
    as_conv_input = lambda a: a.astype(bf16).astype(f32)
    for b in range(batch):
        s_scr[b * stride + lead: b * stride + HALO, :] = as_conv_input(hist_ref[b])
        s_scr[b * stride + HALO: (b + 1) * stride, :] = as_conv_input(u_ref[b * tq:(b + 1) * tq, :])
        _dwconv(s_scr, b * stride, tq, cw_ref, cb_ref, y_scr, b * tq, tq)
    valid = lax.broadcasted_iota(i32, (1, TM_DISP), 1) < n
    x1, xf, top_e, gates, counts = _mix_tail(
        y_scr[...], attn_ref[...], ga_ref[...], gc_ref[...], x_ref[...], valid,
        lg_ref, lb_ref, wpw_ref, bpw_ref, wout_ref, nf_ref, wr_ref, br_ref)
    x1_ref[...] = x1
    xf_ref[...] = xf
    te_ref[...] = top_e
    gt_ref[...] = gates
    cnt_ref[0] = counts


def _mix_weight_specs(nidx):
    shapes = [(CONV_WIDTH, SUBLANES, CONV_CH), (1, CONV_CH), (1, CONV_CH), (1, CONV_CH), (CONV_CH, D_MODEL),
              (1, D_MODEL), (D_MODEL, D_MODEL), (1, D_MODEL), (N_EXPERTS, D_MODEL), (N_EXPERTS, 1)]
    return [pl.BlockSpec(s, functools.partial(lambda nd, i: (0,) * nd, len(s))) for s in shapes]


def _mix_out(n, nt):
    shapes = [jax.ShapeDtypeStruct((n, D_MODEL), f32), jax.ShapeDtypeStruct((n, D_MODEL), bf16),
              jax.ShapeDtypeStruct((TOP_K, n), i32), jax.ShapeDtypeStruct((TOP_K, n), f32),
              jax.ShapeDtypeStruct((nt, N_EXPERTS, 1), i32)]
    return shapes


def _mix_prompt(u, attn, ga, gc, x2d, weights, tiles_per_seq):
    n = x2d.shape[0]
    nt = n // TM_MIX
    row = lambda i: (i, 0)
    halo = lambda i: (jnp.maximum(i * (TM_MIX // HALO) - 1, 0), 0)
    tok = lambda i: (0, i)
    return pl.pallas_call(
        functools.partial(_mix_prompt_kernel, tiles_per_seq=tiles_per_seq),
        grid=(nt,),
        in_specs=[pl.BlockSpec((TM_MIX, CONV_CH), row), pl.BlockSpec((HALO, CONV_CH), halo),
                  pl.BlockSpec((TM_MIX, Q_W), row), pl.BlockSpec((TM_MIX, D_MODEL), row),
                  pl.BlockSpec((TM_MIX, D_MODEL), row), pl.BlockSpec((TM_MIX, D_MODEL), row)] + _mix_weight_specs(1),
        out_specs=[pl.BlockSpec((TM_MIX, D_MODEL), row), pl.BlockSpec((TM_MIX, D_MODEL), row),
                   pl.BlockSpec((TOP_K, TM_MIX), tok), pl.BlockSpec((TOP_K, TM_MIX), tok),
                   pl.BlockSpec((None, N_EXPERTS, 1), lambda i: (i, 0, 0))],
        out_shape=_mix_out(n, nt),
        scratch_shapes=[pltpu.VMEM((TM_MIX + HALO, CONV_CH), f32), pltpu.VMEM((TM_MIX, CONV_CH), f32)],
        compiler_params=_cparams(1),
        name="mix_prompt",
    )(u, u, attn, ga, gc, x2d, *weights)


def _mix_sample(u, hist, attn, ga, gc, x2d, weights, batch, tq):
    pad = lambda a: jnp.pad(a, ((0, TM_DISP - a.shape[0]), (0, 0)))
    return pl.pallas_call(
        functools.partial(_mix_sample_kernel, batch=batch, tq=tq),
        out_shape=_mix_out(TM_DISP, 1),
        scratch_shapes=[pltpu.VMEM((batch * (HALO + tq), CONV_CH), f32), pltpu.VMEM((TM_DISP, CONV_CH), f32)],
        compiler_params=pltpu.CompilerParams(vmem_limit_bytes=VMEM_LIMIT),
        name="mix_sample",
    )(u, hist, pad(attn), pad(ga), pad(gc), pad(x2d), *weights)


def _segment_dmas(tile, seg_rows, seg_local, seg_slot, make_copy, start):
    def body(e, carry):
        idx = tile * N_EXPERTS + e
        n = seg_rows[idx]
        loc = seg_local[idx]
        dst = seg_slot[idx]
        def chunk(sz):
            @pl.when((n & sz) != 0)
            def _():
                off = pl.multiple_of(n & ~(2 * sz - 1), ROW_ALIGN)
                cp = make_copy(pl.multiple_of(loc + off, ROW_ALIGN), pl.multiple_of(dst + off, ROW_ALIGN), sz)
                if start:
                    cp.start()
                else:
                    cp.wait()

        @pl.when(n >= SEG_SPLIT)
        def _():
            for sz in SEG_CHUNKS:
                if sz >= SEG_SPLIT:
                    chunk(sz)

        for sz in SEG_CHUNKS:
            if sz < SEG_SPLIT:
                chunk(sz)
        return carry

    lax.fori_loop(0, N_EXPERTS, body, 0)


def _tile_wait(n_rows, make_copy):
    for sz in WAIT_CHUNKS:
        @pl.when((n_rows & sz) != 0)
        def _():
            make_copy(0, 0, sz).wait()


def _dispatch_kernel(seg_slot, seg_local, seg_rows, tile_rows, tail_slot, tail_rows,
                     xa_ref, xb_ref, te_ref, lb_ref, lpos_ref, slots_hbm, buf, zbuf, sem, zsem,
                     *, n_prompt_tiles, n_tiles):
    i = pl.program_id(0)
    slot = i % 2
    x = jnp.where(i < n_prompt_tiles, xa_ref[...], xb_ref[...])
    te = te_ref[...]
    eidx = lax.broadcasted_iota(i32, (N_EXPERTS, TM_DISP), 0)
    hits = [te[k:k + 1, :] == eidx for k in range(TOP_K)]
    routed = hits[0] | hits[1] | hits[2] | hits[3]
    before = lax.broadcasted_iota(i32, (TM_DISP, TM_DISP), 0) < lax.broadcasted_iota(i32, (TM_DISP, TM_DISP), 1)
    rank = jnp.dot(jnp.where(routed, 1.0, 0.0).astype(bf16), jnp.where(before, 1.0, 0.0).astype(bf16),
                   preferred_element_type=f32)
    pos = lb_ref[...] + rank.astype(i32)
    lpos = []
    for k in range(TOP_K):
        p = jnp.sum(jnp.where(hits[k], pos, 0), axis=0, keepdims=True)
        lpos.append(jnp.where(te[k:k + 1, :] >= 0, p, -1))
    lpos_ref[...] = jnp.concatenate(lpos, axis=0)
    ridx = lax.broadcasted_iota(i32, (LOCAL_ROWS, TM_DISP), 0)
    onehot = (ridx == lpos[0]) | (ridx == lpos[1]) | (ridx == lpos[2]) | (ridx == lpos[3])
    srt = jnp.dot(jnp.where(onehot, 1.0, 0.0).astype(bf16), x, preferred_element_type=f32)
    half = D_MODEL // 2
    lo = lax.shift_right_logical(lax.bitcast_convert_type(srt[:, :half], i32), 16)
    hi = lax.bitcast_convert_type(srt[:, half:], i32) & jnp.int32(-65536)
    buf[slot] = hi | lo

    def copy_of(s):
        def make(loc, dst, sz):
            return pltpu.make_async_copy(buf.at[s, pl.ds(loc, sz), :], slots_hbm.at[pl.ds(dst, sz), :], sem.at[s])
        return make

    _segment_dmas(i, seg_rows, seg_local, seg_slot, copy_of(slot), True)

    @pl.when(i > 0)
    def _():
        _tile_wait(tile_rows[i - 1], copy_of(1 - slot))

    @pl.when(i == n_tiles - 1)
    def _():
        _tile_wait(tile_rows[i], copy_of(slot))
        zbuf[...] = jnp.zeros(zbuf.shape, i32)
        for start in (True, False):
            def body(e, carry):
                n = tail_rows[e]
                dst = tail_slot[e]
                for sz in TAIL_CHUNKS:
                    @pl.when((n & sz) != 0)
                    def _():
                        off = pl.multiple_of(n & ~(2 * sz - 1), ROW_ALIGN)
                        cp = pltpu.make_async_copy(zbuf.at[pl.ds(0, sz), :],
                                                   slots_hbm.at[pl.ds(pl.multiple_of(dst + off, ROW_ALIGN), sz), :], zsem)
                        if start:
                            cp.start()
                        else:
                            cp.wait()
                return carry
            lax.fori_loop(0, N_EXPERTS, body, 0)


def _dispatch(xf_p, xf_s, top_e, lbase_v, meta, n_slots):
    n_prompt_tiles = xf_p.shape[0] // TM_DISP
    n_tiles = n_prompt_tiles + 1
    half = D_MODEL // 2
    grid_spec = pltpu.PrefetchScalarGridSpec(
        num_scalar_prefetch=6,
        grid=(n_tiles,),
        in_specs=[pl.BlockSpec((TM_DISP, D_MODEL), lambda i, *_: (jnp.minimum(i, n_prompt_tiles - 1), 0)),
                  pl.BlockSpec((TM_DISP, D_MODEL), lambda i, *_: (0, 0)),
                  pl.BlockSpec((TOP_K, TM_DISP), lambda i, *_: (0, i)),
                  pl.BlockSpec((None, N_EXPERTS, 1), lambda i, *_: (i, 0, 0))],
        out_specs=[pl.BlockSpec((TOP_K, TM_DISP), lambda i, *_: (0, i)),
                   pl.BlockSpec(memory_space=pl.ANY)],
        scratch_shapes=[pltpu.VMEM((2, LOCAL_ROWS, half), i32), pltpu.VMEM((max(TAIL_CHUNKS), half), i32),
                        pltpu.SemaphoreType.DMA((2,)), pltpu.SemaphoreType.DMA(())],
    )
    return pl.pallas_call(
        functools.partial(_dispatch_kernel, n_prompt_tiles=n_prompt_tiles, n_tiles=n_tiles),
        grid_spec=grid_spec,
        out_shape=[jax.ShapeDtypeStruct((TOP_K, n_tiles * TM_DISP), i32),
                   jax.ShapeDtypeStruct((n_slots, half), i32)],
        compiler_params=_cparams(1),
        name="dispatch",
    )(*meta, xf_p, xf_s, top_e, lbase_v)


def _combine_kernel(seg_slot, seg_local, seg_rows, tile_rows,
                    lpos_ref, gate_ref, xa_ref, xb_ref, nfin_ref, y_hbm, ya_ref, yb_ref, buf, sem,
                    *, n_prompt_tiles, n_tiles):
    i = pl.program_id(0)
    slot = i % 2

    def copy_of(s):
        def make(loc, src, sz):
            return pltpu.make_async_copy(y_hbm.at[pl.ds(src, sz), :], buf.at[s, pl.ds(loc, sz), :], sem.at[s])
        return make

    @pl.when(i == 0)
    def _():
        buf[...] = jnp.zeros(buf.shape, f32)
        _segment_dmas(0, seg_rows, seg_local, seg_slot, copy_of(0), True)

    @pl.when(i + 1 < n_tiles)
    def _():
        _segment_dmas(i + 1, seg_rows, seg_local, seg_slot, copy_of(1 - slot), True)

    _tile_wait(tile_rows[i], copy_of(slot))

    lpos = lpos_ref[...]
    gate = gate_ref[...]
    cidx = lax.broadcasted_iota(i32, (TM_DISP, LOCAL_ROWS), 1)
    wmat = jnp.zeros((TM_DISP, LOCAL_ROWS), f32)
    for k in range(TOP_K):
        wmat = jnp.where(cidx == lpos[:, k:k + 1], gate[:, k:k + 1], wmat)
    moe = jnp.dot(wmat.astype(bf16), buf[slot].astype(bf16), preferred_element_type=f32)
    x1 = jnp.where(i < n_prompt_tiles, xa_ref[...], xb_ref[...])
    y = _rms(x1 + moe, nfin_ref[...])

    @pl.when(i < n_prompt_tiles)
    def _():
        ya_ref[...] = y

    @pl.when(i >= n_prompt_tiles)
    def _():
        yb_ref[...] = y


def _combine(y_slots, lpos_t, gate_t, x1_p, x1_s, norm_final, meta):
    n_prompt_tiles = x1_p.shape[0] // TM_DISP
    n_tiles = n_prompt_tiles + 1
    last = n_prompt_tiles - 1
    grid_spec = pltpu.PrefetchScalarGridSpec(
        num_scalar_prefetch=4,
        grid=(n_tiles,),
        in_specs=[pl.BlockSpec((TM_DISP, TOP_K), lambda i, *_: (i, 0)),
                  pl.BlockSpec((TM_DISP, TOP_K), lambda i, *_: (i, 0)),
                  pl.BlockSpec((TM_DISP, D_MODEL), lambda i, *_: (jnp.minimum(i, last), 0)),
                  pl.BlockSpec((TM_DISP, D_MODEL), lambda i, *_: (0, 0)),
                  pl.BlockSpec((1, D_MODEL), lambda i, *_: (0, 0)),
                  pl.BlockSpec(memory_space=pl.ANY)],
        out_specs=[pl.BlockSpec((TM_DISP, D_MODEL), lambda i, *_: (jnp.minimum(i, last), 0)),
                   pl.BlockSpec((TM_DISP, D_MODEL), lambda i, *_: (0, 0))],
        scratch_shapes=[pltpu.VMEM((2, LOCAL_ROWS, D_MODEL), f32), pltpu.SemaphoreType.DMA((2,))],
    )
    return pl.pallas_call(
        functools.partial(_combine_kernel, n_prompt_tiles=n_prompt_tiles, n_tiles=n_tiles),
        grid_spec=grid_spec,
        out_shape=[jax.ShapeDtypeStruct(x1_p.shape, f32), jax.ShapeDtypeStruct(x1_s.shape, f32)],
        compiler_params=_cparams(1),
        name="combine",
    )(*meta, lpos_t, gate_t, x1_p, x1_s, norm_final, y_slots)


ROW_DMA_PRIORITY = 1


def _experts_kernel(first_blk, n_blk, n_total, xs_hbm, wgu_ref, bgu_ref, wd_ref, bd_ref, y_hbm,
                    wgu_bf, wd_bf, xbuf, ybuf, xsem, ysem):
    e = pl.program_id(0)
    nb = n_blk[e]
    g0 = first_blk[e]
    total = n_total[0]

    def rows(g):
        return pl.ds(pl.multiple_of(g * R_BLK, R_BLK), R_BLK)

    def x_copy(g, slot):
        return pltpu.make_async_copy(xs_hbm.at[rows(g), :], xbuf.at[slot], xsem.at[slot])

    def y_copy(g, slot):
        return pltpu.make_async_copy(ybuf.at[slot], y_hbm.at[rows(g), :], ysem.at[slot])

    @pl.when(e == 0)
    def _():
        x_copy(0, 0).start(priority=ROW_DMA_PRIORITY)

    @pl.when(nb > 0)
    def _():
        wgu_bf[...] = wgu_ref[...].astype(bf16)
        wd_bf[...] = wd_ref[...].astype(bf16)

        def body(j, carry):
            blk = g0 + j
            slot = blk % 2

            @pl.when(blk + 1 < total)
            def _():
                x_copy(blk + 1, 1 - slot).start(priority=ROW_DMA_PRIORITY)

            x_copy(blk, slot).wait()

            @pl.when(blk >= 2)
            def _():
                y_copy(blk - 2, slot).wait()

            pk = xbuf[slot]
            lo = lax.bitcast_convert_type(lax.shift_left(pk, 16), f32).astype(bf16)
            hi = lax.bitcast_convert_type(pk & jnp.int32(-65536), f32).astype(bf16)
            x = jnp.concatenate([lo, hi], axis=1)
            gu = jnp.dot(x, wgu_bf[...], preferred_element_type=f32) + bgu_ref[...]
            g = jnp.minimum(gu[:, :D_FF], SWIGLU_LIMIT)
            up = jnp.clip(gu[:, D_FF:], -SWIGLU_LIMIT, SWIGLU_LIMIT)
            h = (up + 1.0) * (g * jax.nn.sigmoid(SWIGLU_ALPHA * g))
            ybuf[slot] = jnp.dot(h.astype(bf16), wd_bf[...], preferred_element_type=f32) + bd_ref[...]
            y_copy(blk, slot).start(priority=ROW_DMA_PRIORITY)
            return carry

        lax.fori_loop(0, nb, body, 0)

    @pl.when(e == N_EXPERTS - 1)
    def _():
        @pl.when(total >= 2)
        def _():
            y_copy(total - 2, total % 2).wait()

        y_copy(total - 1, (total - 1) % 2).wait()


def _experts(x_slots, w_gu, b_gu, w_down, b_down, first_blk, n_blk, n_total):
    half = D_MODEL // 2
    wsel = lambda e, *_: (e, 0, 0)
    grid_spec = pltpu.PrefetchScalarGridSpec(
        num_scalar_prefetch=3,
        grid=(N_EXPERTS,),
        in_specs=[pl.BlockSpec(memory_space=pl.ANY),
                  pl.BlockSpec((None, D_MODEL, 2 * D_FF), wsel),
                  pl.BlockSpec((None, 1, 2 * D_FF), wsel),
                  pl.BlockSpec((None, D_FF, D_MODEL), wsel),
                  pl.BlockSpec((None, 1, D_MODEL), wsel)],
        out_specs=pl.BlockSpec(memory_space=pl.ANY),
        scratch_shapes=[pltpu.VMEM((D_MODEL, 2 * D_FF), bf16), pltpu.VMEM((D_FF, D_MODEL), bf16),
                        pltpu.VMEM((2, R_BLK, half), i32), pltpu.VMEM((2, R_BLK, D_MODEL), f32),
                        pltpu.SemaphoreType.DMA((2,)), pltpu.SemaphoreType.DMA((2,))],
    )
    return pl.pallas_call(
        _experts_kernel,
        grid_spec=grid_spec,
        out_shape=jax.ShapeDtypeStruct((x_slots.shape[0], D_MODEL), f32),
        compiler_params=_cparams(1),
        name="experts",
    )(first_blk, n_blk, n_total, x_slots, w_gu, b_gu.reshape(N_EXPERTS, 1, 2 * D_FF), w_down,
      b_down.reshape(N_EXPERTS, 1, D_MODEL))


def _slot_layout(counts):
    seg_rows = (counts + ROW_ALIGN - 1) // ROW_ALIGN * ROW_ALIGN
    seg_local = jnp.cumsum(seg_rows, axis=1) - seg_rows
    total = jnp.sum(seg_rows, axis=0)
    region = (total + R_BLK - 1) // R_BLK * R_BLK
    region_end = jnp.cumsum(region)
    ebase = region_end - region
    seg_slot = ebase[None, :] + jnp.cumsum(seg_rows, axis=0) - seg_rows
    flat = lambda a: a.reshape(-1).astype(i32)
    return dict(seg_slot=flat(seg_slot), seg_local=flat(seg_local), seg_rows=flat(seg_rows),
                tile_rows=flat(jnp.sum(seg_rows, axis=1)),
                tail_slot=flat(ebase + total), tail_rows=flat(region - total),
                first_blk=flat(ebase // R_BLK), n_blk=flat(region // R_BLK),
                n_total=flat(region_end[-1:] // R_BLK),
                lbase_v=seg_local.astype(i32)[:, :, None])


def kernel(x_prompt, x_sample, cache_k, cache_v, cache_conv, norm_mix, w_in, attn_sinks, conv_dw_w, conv_dw_b,
           conv_ln_g, conv_ln_b, w_pw, b_pw, w_out, norm_ffn, w_router, b_router, w_gu, b_gu, w_down, b_down,
           norm_final):
    depth = norm_mix.shape[0]
    assert depth == 1, "one layer per step"
    batch, seq, _ = x_prompt.shape
    dbatch, dseq, _ = x_sample.shape
    window = cache_k.shape[2]
    n_p, n_s = batch * seq, dbatch * dseq
    assert seq % TM_PROJ == 0 and seq % (ATTN_CHUNKS * CHUNK) == 0 and seq >= SPAN and n_s <= TM_DISP
    assert n_p % TM_DISP == 0 and TM_DISP % TM_MIX == 0 and seq % TM_MIX == 0
    l = 0
    row = lambda a: a.reshape(1, -1)

    w_in_bf = w_in[l].astype(bf16)
    tabs_p = _rope_tables(jnp.arange(seq, dtype=f32))
    tabs_s = _rope_tables(PAST_LEN + jnp.arange(dseq, dtype=f32))
    tabs_s = tuple(jnp.tile(t, (dbatch, 1)) for t in tabs_s)
    xp2 = x_prompt.reshape(n_p, D_MODEL)
    xs2 = x_sample.reshape(n_s, D_MODEL)
    q_p, kv_p, u_p, ga_p, gc_p, ut_p = _in_proj(xp2, row(norm_mix[l]), w_in_bf, tabs_p, TM_PROJ, seq // TM_PROJ, bf16)
    q_s, kv_s, u_s, ga_s, gc_s, _ = _in_proj(xs2, row(norm_mix[l]), w_in_bf, tabs_s, n_s, 1, f32)

    sinks = attn_sinks[l].astype(f32).reshape(N_KV_HEADS, 1, GROUP)
    attn_p = _attn_prompt(q_p, kv_p, jnp.repeat(sinks, CHUNK, axis=2), batch, seq)
    qpos = PAST_LEN + np.arange(dseq)
    kpos = PAST_LEN - window + np.arange(window + dseq)
    qch, kch = qpos // CHUNK, kpos // CHUNK
    mask = (kch[None, :] >= qch[:, None] - WINDOW_CHUNKS) & (kch[None, :] <= qch[:, None]) & (kpos[None, :] >= 0)
    mask_rows = jnp.asarray(np.tile(mask.astype(np.float32).T, (1, GROUP)))
    ck = cache_k[l].reshape(dbatch, window, KV_W)
    cv = cache_v[l].reshape(dbatch, window, KV_W)
    attn_s = _attn_sample(q_s, kv_s, ck, cv, mask_rows, jnp.repeat(sinks, dseq, axis=2), dbatch, dseq)

    conv_w = jnp.broadcast_to(conv_dw_w[l][:, None, :], (CONV_WIDTH, SUBLANES, CONV_CH))
    mix_w = (conv_w, row(conv_dw_b[l]), row(conv_ln_g[l]),
             row(conv_ln_b[l]), w_pw[l].astype(bf16), row(b_pw[l]), w_out[l].astype(bf16), row(norm_ffn[l]),
             w_router[l].T.astype(bf16), b_router[l].astype(f32).reshape(N_EXPERTS, 1))
    x1_p, xf_p, te_p, gt_p, cnt_p = _mix_prompt(u_p, attn_p, ga_p, gc_p, xp2, mix_w, seq // TM_MIX)
    x1_s, xf_s, te_s, gt_s, cnt_s = _mix_sample(u_s, cache_conv[l], attn_s, ga_s, gc_s, xs2, mix_w, dbatch, dseq)

    n_tiles = n_p // TM_DISP + 1
    n_all = n_tiles * TM_DISP
    n_blocks = -(-(TOP_K * n_all + (ROW_ALIGN - 1) * N_EXPERTS * n_tiles + N_EXPERTS * (R_BLK - ROW_ALIGN)) // R_BLK)
    cnt_p = cnt_p.reshape(n_p // TM_DISP, TM_DISP // TM_MIX, N_EXPERTS).sum(axis=1)
    lay = _slot_layout(jnp.concatenate([cnt_p, cnt_s[:, :, 0]], axis=0))
    top_e = jnp.concatenate([te_p, te_s], axis=1)
    gates = jnp.concatenate([gt_p, gt_s], axis=1)
    lpos, x_slots = _dispatch(xf_p, xf_s, top_e, lay["lbase_v"],
                              (lay["seg_slot"], lay["seg_local"], lay["seg_rows"], lay["tile_rows"],
                               lay["tail_slot"], lay["tail_rows"]),
                              n_blocks * R_BLK)
    y_slots = _experts(x_slots, w_gu[l], b_gu[l], w_down[l], b_down[l], lay["first_blk"], lay["n_blk"],
                       lay["n_total"])
    y_p, y_s = _combine(y_slots, lpos.T, gates.T, x1_p, x1_s, row(norm_final),
                        (lay["seg_slot"], lay["seg_local"], lay["seg_rows"], lay["tile_rows"]))

    y_prompt = y_p.reshape(batch, seq, D_MODEL)
    y_sample = y_s[:n_s].reshape(dbatch, dseq, D_MODEL)
    kv_tail = kv_p.reshape(batch, seq, 2 * KV_W)[:, seq - window:]
    new_k_p = kv_tail[:, :, :KV_W].reshape(1, batch, window, N_KV_HEADS, HEAD_DIM)
    new_v_p = kv_tail[:, :, KV_W:].reshape(1, batch, window, N_KV_HEADS, HEAD_DIM)
    tiles_per_seq = seq // TM_PROJ
    new_conv_p = ut_p.reshape(batch, tiles_per_seq, HALO, CONV_CH)[:, -1, HALO - (CONV_WIDTH - 1):][None]
    kv_s4 = kv_s.reshape(dbatch, dseq, 2, N_KV_HEADS, HEAD_DIM)
    new_k_s = jnp.concatenate([cache_k[l], kv_s4[:, :, 0]], axis=1)[:, -window:][None]
    new_v_s = jnp.concatenate([cache_v[l], kv_s4[:, :, 1]], axis=1)[:, -window:][None]
    new_conv_s = jnp.concatenate([cache_conv[l], u_s.reshape(dbatch, dseq, CONV_CH)], axis=1)[:, -(CONV_WIDTH - 1):][None]
    return (y_prompt, y_sample, new_k_p, new_v_p, new_conv_p, new_k_s, new_v_s, new_conv_s)
```

```python
import functools

import numpy as np
import jax
import jax.numpy as jnp
from jax import lax
from jax.experimental import pallas as pl
from jax.experimental.pallas import tpu as pltpu

f32 = jnp.float32
bf16 = jnp.bfloat16
i32 = jnp.int32

D_MODEL = 1024
PAST_LEN = 1024
CHUNK = 64
N_HEADS = 16
N_KV_HEADS = 2
HEAD_DIM = 64
GROUP = N_HEADS // N_KV_HEADS
ROT_DIM = HEAD_DIM // 4
ROPE_THETA = 500000.0
WINDOW = 128
WINDOW_CHUNKS = WINDOW // CHUNK
SPAN = (WINDOW_CHUNKS + 1) * CHUNK
CONV_CH = D_MODEL
CONV_WIDTH = 31
N_EXPERTS = 32
TOP_K = 4
D_FF = D_MODEL
SWIGLU_LIMIT = 7.0
SWIGLU_ALPHA = 1.702
NORM_EPS = 1e-5
NEG_INF = -1e30
Q_W = N_HEADS * HEAD_DIM
KV_W = N_KV_HEADS * HEAD_DIM
IN_COLS = Q_W + 2 * KV_W + 2 * CONV_CH + 2 * D_MODEL
Q_SCALE = HEAD_DIM ** -0.5

LANES = 128
SUBLANES = 8
VMEM_LIMIT = 56 * 1024 * 1024

HALO = 32
TM_PROJ = 1024
TM_MIX = 512
TM_DISP = 512
ATTN_CHUNKS = 8
ROW_ALIGN = SUBLANES
R_BLK = 512
LOCAL_ROWS = -(-(TOP_K * TM_DISP + N_EXPERTS * (ROW_ALIGN - 1)) // LANES) * LANES
SEG_CHUNKS = tuple(2 ** p for p in range(int(np.log2(TM_DISP)), int(np.log2(ROW_ALIGN)) - 1, -1))
WAIT_CHUNKS = tuple(2 ** p for p in range(int(np.log2(LOCAL_ROWS)), int(np.log2(ROW_ALIGN)) - 1, -1))
TAIL_CHUNKS = tuple(c for c in SEG_CHUNKS if c < R_BLK)


def _cparams(n_axes):
    return pltpu.CompilerParams(dimension_semantics=("arbitrary",) * n_axes,
                                vmem_limit_bytes=VMEM_LIMIT)


def _rms(x, g):
    return x * lax.rsqrt(jnp.mean(x * x, axis=-1, keepdims=True) + NORM_EPS) * g


def _inproj_kernel(x_ref, g_ref, w_ref, cos_ref, sa_ref, sb_ref,
                   q_ref, kv_ref, u_ref, ga_ref, gc_ref, ut_ref):
    xn = _rms(x_ref[...], g_ref[...]).astype(bf16)
    cos = cos_ref[...]
    sa = sa_ref[...]
    sb = sb_ref[...]

    def rope(t):
        return t * cos + pltpu.roll(t, LANES - ROT_DIM // 2, 1) * sa + pltpu.roll(t, ROT_DIM // 2, 1) * sb

    def proj(c0, n):
        return jnp.dot(xn, w_ref[:, c0:c0 + n], preferred_element_type=f32)

    q = proj(0, Q_W)
    for c in range(Q_W // LANES):
        q_ref[:, c * LANES:(c + 1) * LANES] = (rope(q[:, c * LANES:(c + 1) * LANES]) * Q_SCALE).astype(bf16)
    kv = proj(Q_W, 2 * KV_W)
    kv_ref[:, :KV_W] = rope(kv[:, :KV_W])
    kv_ref[:, KV_W:] = kv[:, KV_W:]
    c0 = Q_W + 2 * KV_W
    u = proj(c0, CONV_CH) * jax.nn.sigmoid(proj(c0 + CONV_CH, CONV_CH))
    u_ref[...] = u.astype(u_ref.dtype)
    ut_ref[...] = u[u.shape[0] - HALO:, :]
    c0 += 2 * CONV_CH
    ga_ref[...] = jax.nn.sigmoid(proj(c0, D_MODEL)).astype(ga_ref.dtype)
    gc_ref[...] = jax.nn.sigmoid(proj(c0 + D_MODEL, D_MODEL)).astype(gc_ref.dtype)


def _rope_tables(pos):
    half = ROT_DIM // 2
    inv = ROPE_THETA ** (-jnp.arange(half, dtype=f32) * 2.0 / ROT_DIM)
    ang = pos[:, None] * inv[None, :]
    cos, sin = jnp.cos(ang), jnp.sin(ang)
    t = pos.shape[0]
    ones = jnp.ones((t, HEAD_DIM - ROT_DIM), f32)
    zeros = jnp.zeros((t, HEAD_DIM - ROT_DIM), f32)
    zh = jnp.zeros((t, half), f32)
    c = jnp.concatenate([cos, cos, ones], axis=1)
    a = jnp.concatenate([-sin, zh, zeros], axis=1)
    b = jnp.concatenate([zh, sin, zeros], axis=1)
    rep = LANES // HEAD_DIM
    return jnp.tile(c, (1, rep)), jnp.tile(a, (1, rep)), jnp.tile(b, (1, rep))


def _in_proj(x2d, norm_g, w_bf, tables, tm, tiles_per_seq, act_dtype):
    n = x2d.shape[0]
    nt = n // tm
    row = lambda i: (i, 0)
    const = lambda i: (0, 0)
    tab = lambda i: (i % tiles_per_seq, 0)
    return pl.pallas_call(
        _inproj_kernel,
        grid=(nt,),
        in_specs=[pl.BlockSpec((tm, D_MODEL), row),
                  pl.BlockSpec((1, D_MODEL), const),
                  pl.BlockSpec((D_MODEL, IN_COLS), const, pipeline_mode=pl.Buffered(1)),
                  pl.BlockSpec((tm, LANES), tab),
                  pl.BlockSpec((tm, LANES), tab),
                  pl.BlockSpec((tm, LANES), tab)],
        out_specs=[pl.BlockSpec((tm, Q_W), row),
                   pl.BlockSpec((tm, 2 * KV_W), row),
                   pl.BlockSpec((tm, CONV_CH), row),
                   pl.BlockSpec((tm, D_MODEL), row),
                   pl.BlockSpec((tm, D_MODEL), row),
                   pl.BlockSpec((None, HALO, CONV_CH), lambda i: (i, 0, 0))],
        out_shape=[jax.ShapeDtypeStruct((n, Q_W), bf16),
                   jax.ShapeDtypeStruct((n, 2 * KV_W), f32),
                   jax.ShapeDtypeStruct((n, CONV_CH), act_dtype),
                   jax.ShapeDtypeStruct((n, D_MODEL), act_dtype),
                   jax.ShapeDtypeStruct((n, D_MODEL), act_dtype),
                   jax.ShapeDtypeStruct((nt, HALO, CONV_CH), f32)],
        compiler_params=_cparams(1),
        name="in_proj",
    )(x2d, norm_g, w_bf, *tables)


def _attn_group(qg, kg, vg, mask_t, sink):
    s = lax.dot_general(kg, qg, (((1,), (1,)), ((), ())), preferred_element_type=f32)
    s = jnp.where(mask_t, s, NEG_INF)
    m = jnp.maximum(jnp.max(s, axis=0, keepdims=True), sink)
    p = jnp.exp(s - m)
    denom = jnp.sum(p, axis=0, keepdims=True) + jnp.exp(sink - m)
    pn = (p / denom).astype(bf16)
    return lax.dot_general(pn, vg, (((0,), (0,)), ((), ())), preferred_element_type=f32)


def _heads_to_rows(q, g):
    base = g * GROUP * HEAD_DIM
    return jnp.concatenate([q[:, base + h * HEAD_DIM: base + (h + 1) * HEAD_DIM] for h in range(GROUP)], axis=0)


def _rows_to_heads(o, tq):
    return [o[h * tq:(h + 1) * tq, :] for h in range(GROUP)]


def _attn_prompt_kernel(q_ref, kv_ref, sink_ref, o_ref):
    j = pl.program_id(1)
    for c in range(ATTN_CHUNKS):
        n = j * ATTN_CHUNKS + c
        first = jnp.maximum(n - WINDOW_CHUNKS, 0)
        win = kv_ref[pl.ds(pl.multiple_of(first * CHUNK, CHUNK), SPAN), :]
        mask = lax.broadcasted_iota(i32, (SPAN, GROUP * CHUNK), 0) < (n - first + 1) * CHUNK
        q = q_ref[c * CHUNK:(c + 1) * CHUNK, :]
        outs = []
        for g in range(N_KV_HEADS):
            kg = win[:, g * HEAD_DIM:(g + 1) * HEAD_DIM].astype(bf16)
            vg = win[:, KV_W + g * HEAD_DIM: KV_W + (g + 1) * HEAD_DIM].astype(bf16)
            o = _attn_group(_heads_to_rows(q, g), kg, vg, mask, sink_ref[g])
            outs += _rows_to_heads(o, CHUNK)
        o_ref[c * CHUNK:(c + 1) * CHUNK, :] = jnp.concatenate(outs, axis=1).astype(bf16)


def _attn_prompt(q, kv, sink_rows, batch, seq):
    qrows = ATTN_CHUNKS * CHUNK
    steps = seq // qrows
    return pl.pallas_call(
        _attn_prompt_kernel,
        grid=(batch, steps),
        in_specs=[pl.BlockSpec((qrows, Q_W), lambda b, j: (b * steps + j, 0)),
                  pl.BlockSpec((seq, 2 * KV_W), lambda b, j: (b, 0)),
                  pl.BlockSpec((N_KV_HEADS, 1, GROUP * CHUNK), lambda b, j: (0, 0, 0))],
        out_specs=pl.BlockSpec((qrows, Q_W), lambda b, j: (b * steps + j, 0)),
        out_shape=jax.ShapeDtypeStruct((batch * seq, Q_W), bf16),
        compiler_params=_cparams(2),
        name="attn_prompt",
    )(q, kv, sink_rows)


def _attn_sample_kernel(q_ref, kvn_ref, ck_ref, cv_ref, mask_ref, sink_ref, o_ref):
    tq = q_ref.shape[0]
    q = q_ref[...]
    kvn = kvn_ref[...]
    ck = ck_ref[...]
    cv = cv_ref[...]
    mask = mask_ref[...] > 0.5
    outs = []
    for g in range(N_KV_HEADS):
        sl = slice(g * HEAD_DIM, (g + 1) * HEAD_DIM)
        kg = jnp.concatenate([ck[:, sl], kvn[:, sl]], axis=0).astype(bf16)
        vg = jnp.concatenate([cv[:, sl], kvn[:, KV_W + g * HEAD_DIM: KV_W + (g + 1) * HEAD_DIM]], axis=0).astype(bf16)
        o = _attn_group(_heads_to_rows(q, g), kg, vg, mask, sink_ref[g])
        outs += _rows_to_heads(o, tq)
    o_ref[...] = jnp.concatenate(outs, axis=1)


def _attn_sample(q, kv_new, cache_k, cache_v, mask_rows, sink_rows, batch, tq):
    w = cache_k.shape[1]
    return pl.pallas_call(
        _attn_sample_kernel,
        grid=(batch,),
        in_specs=[pl.BlockSpec((tq, Q_W), lambda b: (b, 0)),
                  pl.BlockSpec((tq, 2 * KV_W), lambda b: (b, 0)),
                  pl.BlockSpec((None, w, KV_W), lambda b: (b, 0, 0)),
                  pl.BlockSpec((None, w, KV_W), lambda b: (b, 0, 0)),
                  pl.BlockSpec((w + tq, GROUP * tq), lambda b: (0, 0)),
                  pl.BlockSpec((N_KV_HEADS, 1, GROUP * tq), lambda b: (0, 0, 0))],
        out_specs=pl.BlockSpec((tq, Q_W), lambda b: (b, 0)),
        out_shape=jax.ShapeDtypeStruct((batch * tq, Q_W), f32),
        compiler_params=_cparams(1),
        name="attn_sample",
    )(q, kv_new, cache_k, cache_v, mask_rows, sink_rows)


def _dwconv(s_ref, row0, n_rows, w_ref, b_ref, y_ref, yrow0, rc):
    lead = HALO - (CONV_WIDTH - 1)
    offsets = range(lead, lead + CONV_WIDTH)

    def lane_body(lc, carry):
        c0 = pl.multiple_of(lc * LANES, LANES)
        cols = pl.ds(c0, LANES)
        bias = b_ref[:, cols]
        groups = rc // SUBLANES
        for t0 in range(0, n_rows, rc):
            acc = jnp.broadcast_to(bias[None], (groups, SUBLANES, LANES))
            for r in range(SUBLANES):
                offs = [o for o in offsets if o % SUBLANES == r]
                if not offs:
                    continue
                amax = max(offs) // SUBLANES
                z = s_ref[pl.ds(row0 + t0 + r, rc + SUBLANES * amax), cols].reshape(groups + amax, SUBLANES, LANES)
                for o in offs:
                    a = o // SUBLANES
                    acc = acc + z[a: a + groups] * w_ref[o - lead, :, cols][None]
            y_ref[pl.ds(yrow0 + t0, rc), cols] = acc.reshape(rc, LANES)
        return carry

    lax.fori_loop(0, CONV_CH // LANES, lane_body, 0)


def _mix_tail(y, attn, ga, gc, x, valid, lg_ref, lb_ref, wpw_ref, bpw_ref, wout_ref, nf_ref, wr_ref, br_ref):
    tm = y.shape[0]
    mu = jnp.mean(y, axis=-1, keepdims=True)
    d = y - mu
    var = jnp.mean(d * d, axis=-1, keepdims=True)
    yn = d * lax.rsqrt(var + NORM_EPS) * lg_ref[...] + lb_ref[...]
    act = (yn * jax.nn.sigmoid(yn)).astype(bf16)
    conv_out = jnp.dot(act, wpw_ref[...], preferred_element_type=f32) + bpw_ref[...]
    h = (ga.astype(f32) * attn.astype(f32) + gc.astype(f32) * conv_out).astype(bf16)
    x1 = x + jnp.dot(h, wout_ref[...], preferred_element_type=f32)
    xf = _rms(x1, nf_ref[...]).astype(bf16)
    logits = lax.dot_general(wr_ref[...], xf, (((1,), (1,)), ((), ())), preferred_element_type=f32) + br_ref[...]
    eidx = lax.broadcasted_iota(i32, (N_EXPERTS, tm), 0)
    routed = jnp.zeros((N_EXPERTS, tm), jnp.bool_)
    top_l, top_e = [], []
    l = logits
    for _ in range(TOP_K):
        m = jnp.max(l, axis=0, keepdims=True)
        idx = jnp.min(jnp.where(l == m, eidx, N_EXPERTS), axis=0, keepdims=True)
        sel = eidx == idx
        routed = routed | sel
        l = jnp.where(sel, -jnp.inf, l)
        top_l.append(m)
        top_e.append(idx)
    ex = [jnp.exp(t - top_l[0]) for t in top_l]
    tot = ex[0] + ex[1] + ex[2] + ex[3]
    gates = jnp.concatenate([e / tot for e in ex], axis=0)
    top_e = jnp.concatenate(top_e, axis=0)
    if valid is not None:
        routed = routed & valid
        top_e = jnp.where(valid, top_e, -1)
        gates = jnp.where(valid, gates, 0.0)
    counts = jnp.sum(jnp.where(routed, 1, 0).astype(i32), axis=1, keepdims=True)
    return x1, xf, top_e, gates, counts


def _mix_prompt_kernel(u_ref, halo_ref, attn_ref, ga_ref, gc_ref, x_ref, cw_ref, cb_ref, lg_ref, lb_ref,
                       wpw_ref, bpw_ref, wout_ref, nf_ref, wr_ref, br_ref,
                       x1_ref, xf_ref, te_ref, gt_ref, cnt_ref, s_scr, y_scr, *, tiles_per_seq):
    i = pl.program_id(0)
    first = (i % tiles_per_seq) == 0
    s_scr[0:HALO, :] = jnp.where(first, 0.0, halo_ref[...].astype(f32))
    s_scr[HALO:, :] = u_ref[...].astype(f32)
    _dwconv(s_scr, 0, TM_MIX, cw_ref, cb_ref, y_scr, 0, 128)
    x1, xf, top_e, gates, counts = _mix_tail(
        y_scr[...], attn_ref[...], ga_ref[...], gc_ref[...], x_ref[...], None,
        lg_ref, lb_ref, wpw_ref, bpw_ref, wout_ref, nf_ref, wr_ref, br_ref)
    x1_ref[...] = x1
    xf_ref[...] = xf
    te_ref[...] = top_e
    gt_ref[...] = gates
    cnt_ref[...] = counts


def _mix_sample_kernel(u_ref, hist_ref, attn_ref, ga_ref, gc_ref, x_ref, cw_ref, cb_ref, lg_ref, lb_ref,
                       wpw_ref, bpw_ref, wout_ref, nf_ref, wr_ref, br_ref,
                       x1_ref, xf_ref, te_ref, gt_ref, cnt_ref, s_scr, y_scr, *, batch, tq):
    n = batch * tq
    hist_rows = CONV_WIDTH - 1
    lead = HALO - hist_rows
    stride = HALO + tq
    s_scr[...] = jnp.zeros(s_scr.shape, f32)
    y_scr[...] = jnp.zeros(y_scr.shape, f32)
    as_conv_input = lambda a: a.astype(bf16).astype(f32)
    for b in range(batch):
        s_scr[b * stride + lead: b * stride + HALO, :] = as_conv_input(hist_ref[b])
        s_scr[b * stride + HALO: (b + 1) * stride, :] = as_conv_input(u_ref[b * tq:(b + 1) * tq, :])
        _dwconv(s_scr, b * stride, tq, cw_ref, cb_ref, y_scr, b * tq, tq)
    valid = lax.broadcasted_iota(i32, (1, TM_DISP), 1) < n
    x1, xf, top_e, gates, counts = _mix_tail(
        y_scr[...], attn_ref[...], ga_ref[...], gc_ref[...], x_ref[...], valid,
        lg_ref, lb_ref, wpw_ref, bpw_ref, wout_ref, nf_ref, wr_ref, br_ref)
    x1_ref[...] = x1
    xf_ref[...] = xf
    te_ref[...] = top_e
    gt_ref[...] = gates
    cnt_ref[0] = counts


def _mix_weight_specs():
    shapes = [(CONV_WIDTH, SUBLANES, CONV_CH), (1, CONV_CH), (1, CONV_CH), (1, CONV_CH), (CONV_CH, D_MODEL),
              (1, D_MODEL), (D_MODEL, D_MODEL), (1, D_MODEL), (N_EXPERTS, D_MODEL), (N_EXPERTS, 1)]
    return [pl.BlockSpec(s, functools.partial(lambda nd, i: (0,) * nd, len(s))) for s in shapes]


def _mix_out(n, nt):
    shapes = [jax.ShapeDtypeStruct((n, D_MODEL), f32), jax.ShapeDtypeStruct((n, D_MODEL), bf16),
              jax.ShapeDtypeStruct((TOP_K, n), i32), jax.ShapeDtypeStruct((TOP_K, n), f32),
              jax.ShapeDtypeStruct((nt, N_EXPERTS, 1), i32)]
    return shapes


def _mix_prompt(u, attn, ga, gc, x2d, weights, tiles_per_seq):
    n = x2d.shape[0]
    nt = n // TM_MIX
    row = lambda i: (i, 0)
    halo = lambda i: (jnp.maximum(i * (TM_MIX // HALO) - 1, 0), 0)
    tok = lambda i: (0, i)
    return pl.pallas_call(
        functools.partial(_mix_prompt_kernel, tiles_per_seq=tiles_per_seq),
        grid=(nt,),
        in_specs=[pl.BlockSpec((TM_MIX, CONV_CH), row), pl.BlockSpec((HALO, CONV_CH), halo),
                  pl.BlockSpec((TM_MIX, Q_W), row), pl.BlockSpec((TM_MIX, D_MODEL), row),
                  pl.BlockSpec((TM_MIX, D_MODEL), row), pl.BlockSpec((TM_MIX, D_MODEL), row)] + _mix_weight_specs(),
        out_specs=[pl.BlockSpec((TM_MIX, D_MODEL), row), pl.BlockSpec((TM_MIX, D_MODEL), row),
                   pl.BlockSpec((TOP_K, TM_MIX), tok), pl.BlockSpec((TOP_K, TM_MIX), tok),
                   pl.BlockSpec((None, N_EXPERTS, 1), lambda i: (i, 0, 0))],
        out_shape=_mix_out(n, nt),
        scratch_shapes=[pltpu.VMEM((TM_MIX + HALO, CONV_CH), f32), pltpu.VMEM((TM_MIX, CONV_CH), f32)],
        compiler_params=_cparams(1),
        name="mix_prompt",
    )(u, u, attn, ga, gc, x2d, *weights)


def _mix_sample(u, hist, attn, ga, gc, x2d, weights, batch, tq):
    pad = lambda a: jnp.pad(a, ((0, TM_DISP - a.shape[0]), (0, 0)))
    return pl.pallas_call(
        functools.partial(_mix_sample_kernel, batch=batch, tq=tq),
        out_shape=_mix_out(TM_DISP, 1),
        scratch_shapes=[pltpu.VMEM((batch * (HALO + tq), CONV_CH), f32), pltpu.VMEM((TM_DISP, CONV_CH), f32)],
        compiler_params=pltpu.CompilerParams(vmem_limit_bytes=VMEM_LIMIT),
        name="mix_sample",
    )(u, hist, pad(attn), pad(ga), pad(gc), pad(x2d), *weights)


def _segment_dmas(tile, seg_rows, seg_local, seg_slot, make_copy):
    def body(e, carry):
        idx = tile * N_EXPERTS + e
        n = seg_rows[idx]
        loc = seg_local[idx]
        dst = seg_slot[idx]
        for sz in SEG_CHUNKS:
            @pl.when((n & sz) != 0)
            def _():
                off = pl.multiple_of(n & ~(2 * sz - 1), ROW_ALIGN)
                make_copy(pl.multiple_of(loc + off, ROW_ALIGN), pl.multiple_of(dst + off, ROW_ALIGN), sz).start()
        return carry

    lax.fori_loop(0, N_EXPERTS, body, 0)


def _tile_wait(n_rows, make_copy):
    for sz in WAIT_CHUNKS:
        @pl.when((n_rows & sz) != 0)
        def _():
            make_copy(0, 0, sz).wait()


def _dispatch_kernel(seg_slot, seg_local, seg_rows, tile_rows, tail_slot, tail_rows,
                     xa_ref, xb_ref, te_ref, lb_ref, lpos_ref, slots_hbm, buf, zbuf, sem, zsem,
                     *, n_prompt_tiles, n_tiles):
    i = pl.program_id(0)
    slot = i % 2
    x = jnp.where(i < n_prompt_tiles, xa_ref[...], xb_ref[...])
    te = te_ref[...]
    eidx = lax.broadcasted_iota(i32, (N_EXPERTS, TM_DISP), 0)
    hits = [te[k:k + 1, :] == eidx for k in range(TOP_K)]
    routed = hits[0] | hits[1] | hits[2] | hits[3]
    before = lax.broadcasted_iota(i32, (TM_DISP, TM_DISP), 0) < lax.broadcasted_iota(i32, (TM_DISP, TM_DISP), 1)
    rank = jnp.dot(jnp.where(routed, 1.0, 0.0).astype(bf16), jnp.where(before, 1.0, 0.0).astype(bf16),
                   preferred_element_type=f32)
    pos = lb_ref[...] + rank.astype(i32)
    lpos = []
    for k in range(TOP_K):
        p = jnp.sum(jnp.where(hits[k], pos, 0), axis=0, keepdims=True)
        lpos.append(jnp.where(te[k:k + 1, :] >= 0, p, -1))
    lpos_ref[...] = jnp.concatenate(lpos, axis=0)
    ridx = lax.broadcasted_iota(i32, (LOCAL_ROWS, TM_DISP), 0)
    onehot = (ridx == lpos[0]) | (ridx == lpos[1]) | (ridx == lpos[2]) | (ridx == lpos[3])
    srt = jnp.dot(jnp.where(onehot, 1.0, 0.0).astype(bf16), x, preferred_element_type=f32)
    half = D_MODEL // 2
    lo = lax.shift_right_logical(lax.bitcast_convert_type(srt[:, :half], i32), 16)
    hi = lax.bitcast_convert_type(srt[:, half:], i32) & jnp.int32(-65536)
    buf[slot] = hi | lo

    def copy_of(s):
        def make(loc, dst, sz):
            return pltpu.make_async_copy(buf.at[s, pl.ds(loc, sz), :], slots_hbm.at[pl.ds(dst, sz), :], sem.at[s])
        return make

    _segment_dmas(i, seg_rows, seg_local, seg_slot, copy_of(slot))

    @pl.when(i > 0)
    def _():
        _tile_wait(tile_rows[i - 1], copy_of(1 - slot))

    @pl.when(i == n_tiles - 1)
    def _():
        _tile_wait(tile_rows[i], copy_of(slot))
        zbuf[...] = jnp.zeros(zbuf.shape, i32)
        for start in (True, False):
            def body(e, carry):
                n = tail_rows[e]
                dst = tail_slot[e]
                for sz in TAIL_CHUNKS:
                    @pl.when((n & sz) != 0)
                    def _():
                        off = pl.multiple_of(n & ~(2 * sz - 1), ROW_ALIGN)
                        cp = pltpu.make_async_copy(zbuf.at[pl.ds(0, sz), :],
                                                   slots_hbm.at[pl.ds(pl.multiple_of(dst + off, ROW_ALIGN), sz), :], zsem)
                        if start:
                            cp.start()
                        else:
                            cp.wait()
                return carry
            lax.fori_loop(0, N_EXPERTS, body, 0)


def _dispatch(xf_p, xf_s, top_e, lbase_v, meta, n_slots):
    n_prompt_tiles = xf_p.shape[0] // TM_DISP
    n_tiles = n_prompt_tiles + 1
    half = D_MODEL // 2
    grid_spec = pltpu.PrefetchScalarGridSpec(
        num_scalar_prefetch=6,
        grid=(n_tiles,),
        in_specs=[pl.BlockSpec((TM_DISP, D_MODEL), lambda i, *_: (jnp.minimum(i, n_prompt_tiles - 1), 0)),
                  pl.BlockSpec((TM_DISP, D_MODEL), lambda i, *_: (0, 0)),
                  pl.BlockSpec((TOP_K, TM_DISP), lambda i, *_: (0, i)),
                  pl.BlockSpec((None, N_EXPERTS, 1), lambda i, *_: (i, 0, 0))],
        out_specs=[pl.BlockSpec((TOP_K, TM_DISP), lambda i, *_: (0, i)),
                   pl.BlockSpec(memory_space=pl.ANY)],
        scratch_shapes=[pltpu.VMEM((2, LOCAL_ROWS, half), i32), pltpu.VMEM((max(TAIL_CHUNKS), half), i32),
                        pltpu.SemaphoreType.DMA((2,)), pltpu.SemaphoreType.DMA(())],
    )
    return pl.pallas_call(
        functools.partial(_dispatch_kernel, n_prompt_tiles=n_prompt_tiles, n_tiles=n_tiles),
        grid_spec=grid_spec,
        out_shape=[jax.ShapeDtypeStruct((TOP_K, n_tiles * TM_DISP), i32),
                   jax.ShapeDtypeStruct((n_slots, half), i32)],
        compiler_params=_cparams(1),
        name="dispatch",
    )(*meta, xf_p, xf_s, top_e, lbase_v)


def _combine_kernel(seg_slot, seg_local, seg_rows, tile_rows,
                    lpos_ref, gate_ref, xa_ref, xb_ref, nfin_ref, y_hbm, ya_ref, yb_ref, buf, sem,
                    *, n_prompt_tiles, n_tiles):
    i = pl.program_id(0)
    slot = i % 2

    def copy_of(s):
        def make(loc, src, sz):
            return pltpu.make_async_copy(y_hbm.at[pl.ds(src, sz), :], buf.at[s, pl.ds(loc, sz), :], sem.at[s])
        return make

    @pl.when(i == 0)
    def _():
        buf[...] = jnp.zeros(buf.shape, f32)
        _segment_dmas(0, seg_rows, seg_local, seg_slot, copy_of(0))

    @pl.when(i + 1 < n_tiles)
    def _():
        _segment_dmas(i + 1, seg_rows, seg_local, seg_slot, copy_of(1 - slot))

    _tile_wait(tile_rows[i], copy_of(slot))

    lpos = lpos_ref[...]
    gate = gate_ref[...]
    cidx = lax.broadcasted_iota(i32, (TM_DISP, LOCAL_ROWS), 1)
    wmat = jnp.zeros((TM_DISP, LOCAL_ROWS), f32)
    for k in range(TOP_K):
        wmat = jnp.where(cidx == lpos[:, k:k + 1], gate[:, k:k + 1], wmat)
    moe = jnp.dot(wmat.astype(bf16), buf[slot].astype(bf16), preferred_element_type=f32)
    x1 = jnp.where(i < n_prompt_tiles, xa_ref[...], xb_ref[...])
    y = _rms(x1 + moe, nfin_ref[...])

    @pl.when(i < n_prompt_tiles)
    def _():
        ya_ref[...] = y

    @pl.when(i >= n_prompt_tiles)
    def _():
        yb_ref[...] = y


def _combine(y_slots, lpos_t, gate_t, x1_p, x1_s, norm_final, meta):
    n_prompt_tiles = x1_p.shape[0] // TM_DISP
    n_tiles = n_prompt_tiles + 1
    last = n_prompt_tiles - 1
    grid_spec = pltpu.PrefetchScalarGridSpec(
        num_scalar_prefetch=4,
        grid=(n_tiles,),
        in_specs=[pl.BlockSpec((TM_DISP, TOP_K), lambda i, *_: (i, 0)),
                  pl.BlockSpec((TM_DISP, TOP_K), lambda i, *_: (i, 0)),
                  pl.BlockSpec((TM_DISP, D_MODEL), lambda i, *_: (jnp.minimum(i, last), 0)),
                  pl.BlockSpec((TM_DISP, D_MODEL), lambda i, *_: (0, 0)),
                  pl.BlockSpec((1, D_MODEL), lambda i, *_: (0, 0)),
                  pl.BlockSpec(memory_space=pl.ANY)],
        out_specs=[pl.BlockSpec((TM_DISP, D_MODEL), lambda i, *_: (jnp.minimum(i, last), 0)),
                   pl.BlockSpec((TM_DISP, D_MODEL), lambda i, *_: (0, 0))],
        scratch_shapes=[pltpu.VMEM((2, LOCAL_ROWS, D_MODEL), f32), pltpu.SemaphoreType.DMA((2,))],
    )
    return pl.pallas_call(
        functools.partial(_combine_kernel, n_prompt_tiles=n_prompt_tiles, n_tiles=n_tiles),
        grid_spec=grid_spec,
        out_shape=[jax.ShapeDtypeStruct(x1_p.shape, f32), jax.ShapeDtypeStruct(x1_s.shape, f32)],
        compiler_params=_cparams(1),
        name="combine",
    )(*meta, lpos_t, gate_t, x1_p, x1_s, norm_final, y_slots)


ROW_DMA_PRIORITY = 1


def _experts_kernel(first_blk, n_blk, n_total, xs_hbm, wgu_ref, bgu_ref, wd_ref, bd_ref, y_hbm,
                    wgu_bf, wd_bf, xbuf, ybuf, xsem, ysem):
    e = pl.program_id(0)
    nb = n_blk[e]
    g0 = first_blk[e]
    total = n_total[0]

    def rows(g):
        return pl.ds(pl.multiple_of(g * R_BLK, R_BLK), R_BLK)

    def x_copy(g, slot):
        return pltpu.make_async_copy(xs_hbm.at[rows(g), :], xbuf.at[slot], xsem.at[slot])

    def y_copy(g, slot):
        return pltpu.make_async_copy(ybuf.at[slot], y_hbm.at[rows(g), :], ysem.at[slot])

    @pl.when(e == 0)
    def _():
        x_copy(0, 0).start(priority=ROW_DMA_PRIORITY)

    @pl.when(nb > 0)
    def _():
        wgu_bf[...] = wgu_ref[...].astype(bf16)
        wd_bf[...] = wd_ref[...].astype(bf16)

        def body(j, carry):
            blk = g0 + j
            slot = blk % 2

            @pl.when(blk + 1 < total)
            def _():
                x_copy(blk + 1, 1 - slot).start(priority=ROW_DMA_PRIORITY)

            x_copy(blk, slot).wait()

            @pl.when(blk >= 2)
            def _():
                y_copy(blk - 2, slot).wait()

            pk = xbuf[slot]
            lo = lax.bitcast_convert_type(lax.shift_left(pk, 16), f32).astype(bf16)
            hi = lax.bitcast_convert_type(pk & jnp.int32(-65536), f32).astype(bf16)
            x = jnp.concatenate([lo, hi], axis=1)
            gu = jnp.dot(x, wgu_bf[...], preferred_element_type=f32) + bgu_ref[...]
            g = jnp.minimum(gu[:, :D_FF], SWIGLU_LIMIT)
            up = jnp.clip(gu[:, D_FF:], -SWIGLU_LIMIT, SWIGLU_LIMIT)
            h = (up + 1.0) * (g * jax.nn.sigmoid(SWIGLU_ALPHA * g))
            ybuf[slot] = jnp.dot(h.astype(bf16), wd_bf[...], preferred_element_type=f32) + bd_ref[...]
            y_copy(blk, slot).start(priority=ROW_DMA_PRIORITY)
            return carry

        lax.fori_loop(0, nb, body, 0)

    @pl.when(e == N_EXPERTS - 1)
    def _():
        @pl.when(total >= 2)
        def _():
            y_copy(total - 2, total % 2).wait()

        y_copy(total - 1, (total - 1) % 2).wait()


def _experts(x_slots, w_gu, b_gu, w_down, b_down, first_blk, n_blk, n_total):
    half = D_MODEL // 2
    wsel = lambda e, *_: (e, 0, 0)
    grid_spec = pltpu.PrefetchScalarGridSpec(
        num_scalar_prefetch=3,
        grid=(N_EXPERTS,),
        in_specs=[pl.BlockSpec(memory_space=pl.ANY),
                  pl.BlockSpec((None, D_MODEL, 2 * D_FF), wsel),
                  pl.BlockSpec((None, 1, 2 * D_FF), wsel),
                  pl.BlockSpec((None, D_FF, D_MODEL), wsel),
                  pl.BlockSpec((None, 1, D_MODEL), wsel)],
        out_specs=pl.BlockSpec(memory_space=pl.ANY),
        scratch_shapes=[pltpu.VMEM((D_MODEL, 2 * D_FF), bf16), pltpu.VMEM((D_FF, D_MODEL), bf16),
                        pltpu.VMEM((2, R_BLK, half), i32), pltpu.VMEM((2, R_BLK, D_MODEL), f32),
                        pltpu.SemaphoreType.DMA((2,)), pltpu.SemaphoreType.DMA((2,))],
    )
    return pl.pallas_call(
        _experts_kernel,
        grid_spec=grid_spec,
        out_shape=jax.ShapeDtypeStruct((x_slots.shape[0], D_MODEL), f32),
        compiler_params=_cparams(1),
        name="experts",
    )(first_blk, n_blk, n_total, x_slots, w_gu, b_gu.reshape(N_EXPERTS, 1, 2 * D_FF), w_down,
      b_down.reshape(N_EXPERTS, 1, D_MODEL))


def _slot_layout(counts):
    seg_rows = (counts + ROW_ALIGN - 1) // ROW_ALIGN * ROW_ALIGN
    seg_local = jnp.cumsum(seg_rows, axis=1) - seg_rows
    total = jnp.sum(seg_rows, axis=0)
    region = (total + R_BLK - 1) // R_BLK * R_BLK
    region_end = jnp.cumsum(region)
    ebase = region_end - region
    seg_slot = ebase[None, :] + jnp.cumsum(seg_rows, axis=0) - seg_rows
    flat = lambda a: a.reshape(-1).astype(i32)
    return dict(seg_slot=flat(seg_slot), seg_local=flat(seg_local), seg_rows=flat(seg_rows),
                tile_rows=flat(jnp.sum(seg_rows, axis=1)),
                tail_slot=flat(ebase + total), tail_rows=flat(region - total),
                first_blk=flat(ebase // R_BLK), n_blk=flat(region // R_BLK),
                n_total=flat(region_end[-1:] // R_BLK),
                lbase_v=seg_local.astype(i32)[:, :, None])


def kernel(x_prompt, x_sample, cache_k, cache_v, cache_conv, norm_mix, w_in, attn_sinks, conv_dw_w, conv_dw_b,
           conv_ln_g, conv_ln_b, w_pw, b_pw, w_out, norm_ffn, w_router, b_router, w_gu, b_gu, w_down, b_down,
           norm_final):
    depth = norm_mix.shape[0]
    assert depth == 1, "one layer per step"
    batch, seq, _ = x_prompt.shape
    dbatch, dseq, _ = x_sample.shape
    window = cache_k.shape[2]
    n_p, n_s = batch * seq, dbatch * dseq
    assert seq % TM_PROJ == 0 and seq % (ATTN_CHUNKS * CHUNK) == 0 and seq >= SPAN and n_s <= TM_DISP
    assert n_p % TM_DISP == 0 and TM_DISP % TM_MIX == 0 and seq % TM_MIX == 0
    l = 0
    row = lambda a: a.reshape(1, -1)

    w_in_bf = w_in[l].astype(bf16)
    tabs_p = _rope_tables(jnp.arange(seq, dtype=f32))
    tabs_s = _rope_tables(PAST_LEN + jnp.arange(dseq, dtype=f32))
    tabs_s = tuple(jnp.tile(t, (dbatch, 1)) for t in tabs_s)
    xp2 = x_prompt.reshape(n_p, D_MODEL)
    xs2 = x_sample.reshape(n_s, D_MODEL)
    q_p, kv_p, u_p, ga_p, gc_p, ut_p = _in_proj(xp2, row(norm_mix[l]), w_in_bf, tabs_p, TM_PROJ, seq // TM_PROJ, bf16)
    q_s, kv_s, u_s, ga_s, gc_s, _ = _in_proj(xs2, row(norm_mix[l]), w_in_bf, tabs_s, n_s, 1, f32)

    sinks = attn_sinks[l].astype(f32).reshape(N_KV_HEADS, 1, GROUP)
    attn_p = _attn_prompt(q_p, kv_p, jnp.repeat(sinks, CHUNK, axis=2), batch, seq)
    qpos = PAST_LEN + np.arange(dseq)
    kpos = PAST_LEN - window + np.arange(window + dseq)
    qch, kch = qpos // CHUNK, kpos // CHUNK
    mask = (kch[None, :] >= qch[:, None] - WINDOW_CHUNKS) & (kch[None, :] <= qch[:, None]) & (kpos[None, :] >= 0)
    mask_rows = jnp.asarray(np.tile(mask.astype(np.float32).T, (1, GROUP)))
    ck = cache_k[l].reshape(dbatch, window, KV_W)
    cv = cache_v[l].reshape(dbatch, window, KV_W)
    attn_s = _attn_sample(q_s, kv_s, ck, cv, mask_rows, jnp.repeat(sinks, dseq, axis=2), dbatch, dseq)

    conv_w = jnp.broadcast_to(conv_dw_w[l][:, None, :], (CONV_WIDTH, SUBLANES, CONV_CH))
    mix_w = (conv_w, row(conv_dw_b[l]), row(conv_ln_g[l]),
             row(conv_ln_b[l]), w_pw[l].astype(bf16), row(b_pw[l]), w_out[l].astype(bf16), row(norm_ffn[l]),
             w_router[l].T.astype(bf16), b_router[l].astype(f32).reshape(N_EXPERTS, 1))
    x1_p, xf_p, te_p, gt_p, cnt_p = _mix_prompt(u_p, attn_p, ga_p, gc_p, xp2, mix_w, seq // TM_MIX)
    x1_s, xf_s, te_s, gt_s, cnt_s = _mix_sample(u_s, cache_conv[l], attn_s, ga_s, gc_s, xs2, mix_w, dbatch, dseq)

    n_tiles = n_p // TM_DISP + 1
    n_all = n_tiles * TM_DISP
    n_blocks = -(-(TOP_K * n_all + (ROW_ALIGN - 1) * N_EXPERTS * n_tiles + N_EXPERTS * (R_BLK - ROW_ALIGN)) // R_BLK)
    cnt_p = cnt_p.reshape(n_p // TM_DISP, TM_DISP // TM_MIX, N_EXPERTS).sum(axis=1)
    lay = _slot_layout(jnp.concatenate([cnt_p, cnt_s[:, :, 0]], axis=0))
    top_e = jnp.concatenate([te_p, te_s], axis=1)
    gates = jnp.concatenate([gt_p, gt_s], axis=1)
    lpos, x_slots = _dispatch(xf_p, xf_s, top_e, lay["lbase_v"],
                              (lay["seg_slot"], lay["seg_local"], lay["seg_rows"], lay["tile_rows"],
                               lay["tail_slot"], lay["tail_rows"]),
                              n_blocks * R_BLK)
    y_slots = _experts(x_slots, w_gu[l], b_gu[l], w_down[l], b_down[l], lay["first_blk"], lay["n_blk"],
                       lay["n_total"])
    y_p, y_s = _combine(y_slots, lpos.T, gates.T, x1_p, x1_s, row(norm_final),
                        (lay["seg_slot"], lay["seg_local"], lay["seg_rows"], lay["tile_rows"]))

    y_prompt = y_p.reshape(batch, seq, D_MODEL)
    y_sample = y_s[:n_s].reshape(dbatch, dseq, D_MODEL)
    kv_tail = kv_p.reshape(batch, seq, 2 * KV_W)[:, seq - window:]
    new_k_p = kv_tail[:, :, :KV_W].reshape(1, batch, window, N_KV_HEADS, HEAD_DIM)
    new_v_p = kv_tail[:, :, KV_W:].reshape(1, batch, window, N_KV_HEADS, HEAD_DIM)
    tiles_per_seq = seq // TM_PROJ
    new_conv_p = ut_p.reshape(batch, tiles_per_seq, HALO, CONV_CH)[:, -1, HALO - (CONV_WIDTH - 1):][None]
    kv_s4 = kv_s.reshape(dbatch, dseq, 2, N_KV_HEADS, HEAD_DIM)
    new_k_s = jnp.concatenate([cache_k[l], kv_s4[:, :, 0]], axis=1)[:, -window:][None]
    new_v_s = jnp.concatenate([cache_v[l], kv_s4[:, :, 1]], axis=1)[:, -window:][None]
    new_conv_s = jnp.concatenate([cache_conv[l], u_s.reshape(dbatch, dseq, CONV_CH)], axis=1)[:, -(CONV_WIDTH - 1):][None]
    return (y_prompt, y_sample, new_k_p, new_v_p, new_conv_p, new_k_s, new_v_s, new_conv_s)
```

```python
import functools

import numpy as np
import jax
import jax.numpy as jnp
from jax import lax
from jax.experimental import pallas as pl
from jax.experimental.pallas import tpu as pltpu

f32 = jnp.float32
bf16 = jnp.bfloat16
i32 = jnp.int32

D_MODEL = 1024
PAST_LEN = 1024
CHUNK = 64
N_HEADS = 16
N_KV_HEADS = 2
HEAD_DIM = 64
GROUP = N_HEADS // N_KV_HEADS
ROT_DIM = HEAD_DIM // 4
ROPE_THETA = 500000.0
WINDOW = 128
WINDOW_CHUNKS = WINDOW // CHUNK
SPAN = (WINDOW_CHUNKS + 1) * CHUNK
CONV_CH = D_MODEL
CONV_WIDTH = 31
N_EXPERTS = 32
TOP_K = 4
D_FF = D_MODEL
SWIGLU_LIMIT = 7.0
SWIGLU_ALPHA = 1.702
NORM_EPS = 1e-5
NEG_INF = -1e30
Q_W = N_HEADS * HEAD_DIM
KV_W = N_KV_HEADS * HEAD_DIM
IN_COLS = Q_W + 2 * KV_W + 2 * CONV_CH + 2 * D_MODEL
Q_SCALE = HEAD_DIM ** -0.5

LANES = 128
SUBLANES = 8
VMEM_LIMIT = 56 * 1024 * 1024

HALO = 32
TM_PROJ = 1024
TM_MIX = 512
TM_DISP = 512
ATTN_CHUNKS = 8
ROW_ALIGN = SUBLANES
R_BLK = 512
LOCAL_ROWS = -(-(TOP_K * TM_DISP + N_EXPERTS * (ROW_ALIGN - 1)) // LANES) * LANES
SEG_CHUNKS = tuple(2 ** p for p in range(int(np.log2(TM_DISP)), int(np.log2(ROW_ALIGN)) - 1, -1))
WAIT_CHUNKS = tuple(2 ** p for p in range(int(np.log2(LOCAL_ROWS)), int(np.log2(ROW_ALIGN)) - 1, -1))
TAIL_CHUNKS = tuple(c for c in SEG_CHUNKS if c < R_BLK)


def _cparams(n_axes):
    return pltpu.CompilerParams(dimension_semantics=("arbitrary",) * n_axes,
                                vmem_limit_bytes=VMEM_LIMIT)


def _rms(x, g):
    return x * lax.rsqrt(jnp.mean(x * x, axis=-1, keepdims=True) + NORM_EPS) * g


def _inproj_kernel(x_ref, g_ref, w_ref, cos_ref, sa_ref, sb_ref,
                   q_ref, kv_ref, u_ref, ga_ref, gc_ref, ut_ref):
    xn = _rms(x_ref[...], g_ref[...]).astype(bf16)
    cos = cos_ref[...]
    sa = sa_ref[...]
    sb = sb_ref[...]

    def rope(t):
        return t * cos + pltpu.roll(t, LANES - ROT_DIM // 2, 1) * sa + pltpu.roll(t, ROT_DIM // 2, 1) * sb

    def proj(c0, n):
        return jnp.dot(xn, w_ref[:, c0:c0 + n], preferred_element_type=f32)

    q = proj(0, Q_W)
    for c in range(Q_W // LANES):
        q_ref[:, c * LANES:(c + 1) * LANES] = (rope(q[:, c * LANES:(c + 1) * LANES]) * Q_SCALE).astype(bf16)
    kv = proj(Q_W, 2 * KV_W)
    kv_ref[:, :KV_W] = rope(kv[:, :KV_W])
    kv_ref[:, KV_W:] = kv[:, KV_W:]
    c0 = Q_W + 2 * KV_W
    u = proj(c0, CONV_CH) * jax.nn.sigmoid(proj(c0 + CONV_CH, CONV_CH))
    u_ref[...] = u.astype(u_ref.dtype)
    ut_ref[...] = u[u.shape[0] - HALO:, :]
    c0 += 2 * CONV_CH
    ga_ref[...] = jax.nn.sigmoid(proj(c0, D_MODEL)).astype(ga_ref.dtype)
    gc_ref[...] = jax.nn.sigmoid(proj(c0 + D_MODEL, D_MODEL)).astype(gc_ref.dtype)


def _rope_tables(pos):
    half = ROT_DIM // 2
    inv = ROPE_THETA ** (-jnp.arange(half, dtype=f32) * 2.0 / ROT_DIM)
    ang = pos[:, None] * inv[None, :]
    cos, sin = jnp.cos(ang), jnp.sin(ang)
    t = pos.shape[0]
    ones = jnp.ones((t, HEAD_DIM - ROT_DIM), f32)
    zeros = jnp.zeros((t, HEAD_DIM - ROT_DIM), f32)
    zh = jnp.zeros((t, half), f32)
    c = jnp.concatenate([cos, cos, ones], axis=1)
    a = jnp.concatenate([-sin, zh, zeros], axis=1)
    b = jnp.concatenate([zh, sin, zeros], axis=1)
    rep = LANES // HEAD_DIM
    return jnp.tile(c, (1, rep)), jnp.tile(a, (1, rep)), jnp.tile(b, (1, rep))


def _in_proj(x2d, norm_g, w_bf, tables, tm, tiles_per_seq, act_dtype):
    n = x2d.shape[0]
    nt = n // tm
    row = lambda i: (i, 0)
    const = lambda i: (0, 0)
    tab = lambda i: (i % tiles_per_seq, 0)
    return pl.pallas_call(
        _inproj_kernel,
        grid=(nt,),
        in_specs=[pl.BlockSpec((tm, D_MODEL), row),
                  pl.BlockSpec((1, D_MODEL), const),
                  pl.BlockSpec((D_MODEL, IN_COLS), const, pipeline_mode=pl.Buffered(1)),
                  pl.BlockSpec((tm, LANES), tab),
                  pl.BlockSpec((tm, LANES), tab),
                  pl.BlockSpec((tm, LANES), tab)],
        out_specs=[pl.BlockSpec((tm, Q_W), row),
                   pl.BlockSpec((tm, 2 * KV_W), row),
                   pl.BlockSpec((tm, CONV_CH), row),
                   pl.BlockSpec((tm, D_MODEL), row),
                   pl.BlockSpec((tm, D_MODEL), row),
                   pl.BlockSpec((None, HALO, CONV_CH), lambda i: (i, 0, 0))],
        out_shape=[jax.ShapeDtypeStruct((n, Q_W), bf16),
                   jax.ShapeDtypeStruct((n, 2 * KV_W), f32),
                   jax.ShapeDtypeStruct((n, CONV_CH), act_dtype),
                   jax.ShapeDtypeStruct((n, D_MODEL), act_dtype),
                   jax.ShapeDtypeStruct((n, D_MODEL), act_dtype),
                   jax.ShapeDtypeStruct((nt, HALO, CONV_CH), f32)],
        compiler_params=_cparams(1),
        name="in_proj",
    )(x2d, norm_g, w_bf, *tables)


def _attn_group(qg, kg, vg, mask_t, sink):
    s = lax.dot_general(kg, qg, (((1,), (1,)), ((), ())), preferred_element_type=f32)
    s = jnp.where(mask_t, s, NEG_INF)
    m = jnp.maximum(jnp.max(s, axis=0, keepdims=True), sink)
    p = jnp.exp(s - m)
    denom = jnp.sum(p, axis=0, keepdims=True) + jnp.exp(sink - m)
    pn = (p / denom).astype(bf16)
    return lax.dot_general(pn, vg, (((0,), (0,)), ((), ())), preferred_element_type=f32)


def _heads_to_rows(q, g):
    base = g * GROUP * HEAD_DIM
    return jnp.concatenate([q[:, base + h * HEAD_DIM: base + (h + 1) * HEAD_DIM] for h in range(GROUP)], axis=0)


def _rows_to_heads(o, tq):
    return [o[h * tq:(h + 1) * tq, :] for h in range(GROUP)]


def _attn_prompt_kernel(q_ref, kv_ref, sink_ref, o_ref):
    j = pl.program_id(1)
    for c in range(ATTN_CHUNKS):
        n = j * ATTN_CHUNKS + c
        first = jnp.maximum(n - WINDOW_CHUNKS, 0)
        win = kv_ref[pl.ds(pl.multiple_of(first * CHUNK, CHUNK), SPAN), :]
        mask = lax.broadcasted_iota(i32, (SPAN, GROUP * CHUNK), 0) < (n - first + 1) * CHUNK
        q = q_ref[c * CHUNK:(c + 1) * CHUNK, :]
        outs = []
        for g in range(N_KV_HEADS):
            kg = win[:, g * HEAD_DIM:(g + 1) * HEAD_DIM].astype(bf16)
            vg = win[:, KV_W + g * HEAD_DIM: KV_W + (g + 1) * HEAD_DIM].astype(bf16)
            o = _attn_group(_heads_to_rows(q, g), kg, vg, mask, sink_ref[g])
            outs += _rows_to_heads(o, CHUNK)
        o_ref[c * CHUNK:(c + 1) * CHUNK, :] = jnp.concatenate(outs, axis=1).astype(bf16)


def _attn_prompt(q, kv, sink_rows, batch, seq):
    qrows = ATTN_CHUNKS * CHUNK
    steps = seq // qrows
    return pl.pallas_call(
        _attn_prompt_kernel,
        grid=(batch, steps),
        in_specs=[pl.BlockSpec((qrows, Q_W), lambda b, j: (b * steps + j, 0)),
                  pl.BlockSpec((seq, 2 * KV_W), lambda b, j: (b, 0)),
                  pl.BlockSpec((N_KV_HEADS, 1, GROUP * CHUNK), lambda b, j: (0, 0, 0))],
        out_specs=pl.BlockSpec((qrows, Q_W), lambda b, j: (b * steps + j, 0)),
        out_shape=jax.ShapeDtypeStruct((batch * seq, Q_W), bf16),
        compiler_params=_cparams(2),
        name="attn_prompt",
    )(q, kv, sink_rows)


def _attn_sample_kernel(q_ref, kvn_ref, ck_ref, cv_ref, mask_ref, sink_ref, o_ref):
    tq = q_ref.shape[0]
    q = q_ref[...]
    kvn = kvn_ref[...]
    ck = ck_ref[...]
    cv = cv_ref[...]
    mask = mask_ref[...] > 0.5
    outs = []
    for g in range(N_KV_HEADS):
        sl = slice(g * HEAD_DIM, (g + 1) * HEAD_DIM)
        kg = jnp.concatenate([ck[:, sl], kvn[:, sl]], axis=0).astype(bf16)
        vg = jnp.concatenate([cv[:, sl], kvn[:, KV_W + g * HEAD_DIM: KV_W + (g + 1) * HEAD_DIM]], axis=0).astype(bf16)
        o = _attn_group(_heads_to_rows(q, g), kg, vg, mask, sink_ref[g])
        outs += _rows_to_heads(o, tq)
    o_ref[...] = jnp.concatenate(outs, axis=1)


def _attn_sample(q, kv_new, cache_k, cache_v, mask_rows, sink_rows, batch, tq):
    w = cache_k.shape[1]
    return pl.pallas_call(
        _attn_sample_kernel,
        grid=(batch,),
        in_specs=[pl.BlockSpec((tq, Q_W), lambda b: (b, 0)),
                  pl.BlockSpec((tq, 2 * KV_W), lambda b: (b, 0)),
                  pl.BlockSpec((None, w, KV_W), lambda b: (b, 0, 0)),
                  pl.BlockSpec((None, w, KV_W), lambda b: (b, 0, 0)),
                  pl.BlockSpec((w + tq, GROUP * tq), lambda b: (0, 0)),
                  pl.BlockSpec((N_KV_HEADS, 1, GROUP * tq), lambda b: (0, 0, 0))],
        out_specs=pl.BlockSpec((tq, Q_W), lambda b: (b, 0)),
        out_shape=jax.ShapeDtypeStruct((batch * tq, Q_W), f32),
        compiler_params=_cparams(1),
        name="attn_sample",
    )(q, kv_new, cache_k, cache_v, mask_rows, sink_rows)


def _dwconv(s_ref, row0, n_rows, w_ref, b_ref, y_ref, yrow0, rc):
    lead = HALO - (CONV_WIDTH - 1)
    offsets = range(lead, lead + CONV_WIDTH)

    def lane_body(lc, carry):
        c0 = pl.multiple_of(lc * LANES, LANES)
        cols = pl.ds(c0, LANES)
        bias = b_ref[:, cols]
        groups = rc // SUBLANES
        for t0 in range(0, n_rows, rc):
            acc = jnp.broadcast_to(bias[None], (groups, SUBLANES, LANES))
            for r in range(SUBLANES):
                offs = [o for o in offsets if o % SUBLANES == r]
                if not offs:
                    continue
                amax = max(offs) // SUBLANES
                z = s_ref[pl.ds(row0 + t0 + r, rc + SUBLANES * amax), cols].reshape(groups + amax, SUBLANES, LANES)
                for o in offs:
                    a = o // SUBLANES
                    acc = acc + z[a: a + groups] * w_ref[o - lead, :, cols][None]
            y_ref[pl.ds(yrow0 + t0, rc), cols] = acc.reshape(rc, LANES)
        return carry

    lax.fori_loop(0, CONV_CH // LANES, lane_body, 0)


def _mix_tail(y, attn, ga, gc, x, valid, lg_ref, lb_ref, wpw_ref, bpw_ref, wout_ref, nf_ref, wr_ref, br_ref):
    tm = y.shape[0]
    mu = jnp.mean(y, axis=-1, keepdims=True)
    d = y - mu
    var = jnp.mean(d * d, axis=-1, keepdims=True)
    yn = d * lax.rsqrt(var + NORM_EPS) * lg_ref[...] + lb_ref[...]
    act = (yn * jax.nn.sigmoid(yn)).astype(bf16)
    conv_out = jnp.dot(act, wpw_ref[...], preferred_element_type=f32) + bpw_ref[...]
    h = (ga.astype(f32) * attn.astype(f32) + gc.astype(f32) * conv_out).astype(bf16)
    x1 = x + jnp.dot(h, wout_ref[...], preferred_element_type=f32)
    xf = _rms(x1, nf_ref[...]).astype(bf16)
    logits = lax.dot_general(wr_ref[...], xf, (((1,), (1,)), ((), ())), preferred_element_type=f32) + br_ref[...]
    eidx = lax.broadcasted_iota(i32, (N_EXPERTS, tm), 0)
    routed = jnp.zeros((N_EXPERTS, tm), jnp.bool_)
    top_l, top_e = [], []
    l = logits
    for _ in range(TOP_K):
        m = jnp.max(l, axis=0, keepdims=True)
        idx = jnp.min(jnp.where(l == m, eidx, N_EXPERTS), axis=0, keepdims=True)
        sel = eidx == idx
        routed = routed | sel
        l = jnp.where(sel, -jnp.inf, l)
        top_l.append(m)
        top_e.append(idx)
    ex = [jnp.exp(t - top_l[0]) for t in top_l]
    tot = ex[0] + ex[1] + ex[2] + ex[3]
    gates = jnp.concatenate([e / tot for e in ex], axis=0)
    top_e = jnp.concatenate(top_e, axis=0)
    if valid is not None:
        routed = routed & valid
        top_e = jnp.where(valid, top_e, -1)
        gates = jnp.where(valid, gates, 0.0)
    counts = jnp.sum(jnp.where(routed, 1, 0).astype(i32), axis=1, keepdims=True)
    return x1, xf, top_e, gates, counts


def _mix_prompt_kernel(u_ref, halo_ref, attn_ref, ga_ref, gc_ref, x_ref, cw_ref, cb_ref, lg_ref, lb_ref,
                       wpw_ref, bpw_ref, wout_ref, nf_ref, wr_ref, br_ref,
                       x1_ref, xf_ref, te_ref, gt_ref, cnt_ref, s_scr, y_scr, *, tiles_per_seq):
    i = pl.program_id(0)
    first = (i % tiles_per_seq) == 0
    s_scr[0:HALO, :] = jnp.where(first, 0.0, halo_ref[...].astype(f32))
    s_scr[HALO:, :] = u_ref[...].astype(f32)
    _dwconv(s_scr, 0, TM_MIX, cw_ref, cb_ref, y_scr, 0, 128)
    x1, xf, top_e, gates, counts = _mix_tail(
        y_scr[...], attn_ref[...], ga_ref[...], gc_ref[...], x_ref[...], None,
        lg_ref, lb_ref, wpw_ref, bpw_ref, wout_ref, nf_ref, wr_ref, br_ref)
    x1_ref[...] = x1
    xf_ref[...] = xf
    te_ref[...] = top_e
    gt_ref[...] = gates
    cnt_ref[...] = counts


def _mix_sample_kernel(u_ref, hist_ref, attn_ref, ga_ref, gc_ref, x_ref, cw_ref, cb_ref, lg_ref, lb_ref,
                       wpw_ref, bpw_ref, wout_ref, nf_ref, wr_ref, br_ref,
                       x1_ref, xf_ref, te_ref, gt_ref, cnt_ref, s_scr, y_scr, *, batch, tq):
    n = batch * tq
    hist_rows = CONV_WIDTH - 1
    lead = HALO - hist_rows
    stride = HALO + tq
    s_scr[...] = jnp.zeros(s_scr.shape, f32)
    as_conv_input = lambda a: a.astype(bf16).astype(f32)
    for b in range(batch):
        s_scr[b * stride + lead: b * stride + HALO, :] = as_conv_input(hist_ref[b])
        s_scr[b * stride + HALO: (b + 1) * stride, :] = as_conv_input(u_ref[b * tq:(b + 1) * tq, :])
        _dwconv(s_scr, b * stride, tq, cw_ref, cb_ref, y_scr, b * tq, tq)
    x1, xf, top_e, gates, counts = _mix_tail(
        y_scr[...], attn_ref[...], ga_ref[...], gc_ref[...], x_ref[...], None,
        lg_ref, lb_ref, wpw_ref, bpw_ref, wout_ref, nf_ref, wr_ref, br_ref)
    x1_ref[...] = jnp.zeros(x1_ref.shape, f32)
    xf_ref[...] = jnp.zeros(xf_ref.shape, bf16)
    te_ref[...] = jnp.full(te_ref.shape, -1, i32)
    gt_ref[...] = jnp.zeros(gt_ref.shape, f32)
    x1_ref[0:n, :] = x1
    xf_ref[0:n, :] = xf
    te_ref[:, 0:n] = top_e
    gt_ref[:, 0:n] = gates
    cnt_ref[0] = counts


def _mix_weight_specs():
    shapes = [(CONV_WIDTH, SUBLANES, CONV_CH), (1, CONV_CH), (1, CONV_CH), (1, CONV_CH), (CONV_CH, D_MODEL),
              (1, D_MODEL), (D_MODEL, D_MODEL), (1, D_MODEL), (N_EXPERTS, D_MODEL), (N_EXPERTS, 1)]
    return [pl.BlockSpec(s, functools.partial(lambda nd, i: (0,) * nd, len(s))) for s in shapes]


def _mix_out(n, nt):
    shapes = [jax.ShapeDtypeStruct((n, D_MODEL), f32), jax.ShapeDtypeStruct((n, D_MODEL), bf16),
              jax.ShapeDtypeStruct((TOP_K, n), i32), jax.ShapeDtypeStruct((TOP_K, n), f32),
              jax.ShapeDtypeStruct((nt, N_EXPERTS, 1), i32)]
    return shapes


def _mix_prompt(u, attn, ga, gc, x2d, weights, tiles_per_seq):
    n = x2d.shape[0]
    nt = n // TM_MIX
    row = lambda i: (i, 0)
    halo = lambda i: (jnp.maximum(i * (TM_MIX // HALO) - 1, 0), 0)
    tok = lambda i: (0, i)
    return pl.pallas_call(
        functools.partial(_mix_prompt_kernel, tiles_per_seq=tiles_per_seq),
        grid=(nt,),
        in_specs=[pl.BlockSpec((TM_MIX, CONV_CH), row), pl.BlockSpec((HALO, CONV_CH), halo),
                  pl.BlockSpec((TM_MIX, Q_W), row), pl.BlockSpec((TM_MIX, D_MODEL), row),
                  pl.BlockSpec((TM_MIX, D_MODEL), row), pl.BlockSpec((TM_MIX, D_MODEL), row)] + _mix_weight_specs(),
        out_specs=[pl.BlockSpec((TM_MIX, D_MODEL), row), pl.BlockSpec((TM_MIX, D_MODEL), row),
                   pl.BlockSpec((TOP_K, TM_MIX), tok), pl.BlockSpec((TOP_K, TM_MIX), tok),
                   pl.BlockSpec((None, N_EXPERTS, 1), lambda i: (i, 0, 0))],
        out_shape=_mix_out(n, nt),
        scratch_shapes=[pltpu.VMEM((TM_MIX + HALO, CONV_CH), f32), pltpu.VMEM((TM_MIX, CONV_CH), f32)],
        compiler_params=_cparams(1),
        name="mix_prompt",
    )(u, u, attn, ga, gc, x2d, *weights)


def _mix_sample(u, hist, attn, ga, gc, x2d, weights, batch, tq):
    return pl.pallas_call(
        functools.partial(_mix_sample_kernel, batch=batch, tq=tq),
        out_shape=_mix_out(TM_DISP, 1),
        scratch_shapes=[pltpu.VMEM((batch * (HALO + tq), CONV_CH), f32), pltpu.VMEM((batch * tq, CONV_CH), f32)],
        compiler_params=pltpu.CompilerParams(vmem_limit_bytes=VMEM_LIMIT),
        name="mix_sample",
    )(u, hist, attn, ga, gc, x2d, *weights)


def _segment_dmas(tile, seg_rows, seg_local, seg_slot, make_copy):
    def body(e, carry):
        idx = tile * N_EXPERTS + e
        n = seg_rows[idx]
        loc = seg_local[idx]
        dst = seg_slot[idx]
        for sz in SEG_CHUNKS:
            @pl.when((n & sz) != 0)
            def _():
                off = pl.multiple_of(n & ~(2 * sz - 1), ROW_ALIGN)
                make_copy(pl.multiple_of(loc + off, ROW_ALIGN), pl.multiple_of(dst + off, ROW_ALIGN), sz).start()
        return carry

    lax.fori_loop(0, N_EXPERTS, body, 0)


def _tile_wait(n_rows, make_copy):
    for sz in WAIT_CHUNKS:
        @pl.when((n_rows & sz) != 0)
        def _():
            make_copy(0, 0, sz).wait()


def _dispatch_kernel(seg_slot, seg_local, seg_rows, tile_rows, tail_slot, tail_rows,
                     xa_ref, xb_ref, te_ref, lb_ref, lpos_ref, slots_hbm, buf, zbuf, sem, zsem,
                     *, n_prompt_tiles, n_tiles):
    i = pl.program_id(0)
    slot = i % 2
    x = jnp.where(i < n_prompt_tiles, xa_ref[...], xb_ref[...])
    te = te_ref[...]
    eidx = lax.broadcasted_iota(i32, (N_EXPERTS, TM_DISP), 0)
    hits = [te[k:k + 1, :] == eidx for k in range(TOP_K)]
    routed = hits[0] | hits[1] | hits[2] | hits[3]
    before = lax.broadcasted_iota(i32, (TM_DISP, TM_DISP), 0) < lax.broadcasted_iota(i32, (TM_DISP, TM_DISP), 1)
    rank = jnp.dot(jnp.where(routed, 1.0, 0.0).astype(bf16), jnp.where(before, 1.0, 0.0).astype(bf16),
                   preferred_element_type=f32)
    pos = lb_ref[...] + rank.astype(i32)
    lpos = []
    for k in range(TOP_K):
        p = jnp.sum(jnp.where(hits[k], pos, 0), axis=0, keepdims=True)
        lpos.append(jnp.where(te[k:k + 1, :] >= 0, p, -1))
    lpos_ref[...] = jnp.concatenate(lpos, axis=0)
    ridx = lax.broadcasted_iota(i32, (LOCAL_ROWS, TM_DISP), 0).astype(f32)
    lf = [p.astype(f32) for p in lpos]
    onehot = ((ridx - lf[0]) * (ridx - lf[1])) * ((ridx - lf[2]) * (ridx - lf[3])) == 0.0
    srt = jnp.dot(jnp.where(onehot, 1.0, 0.0).astype(bf16), x, preferred_element_type=f32)
    half = D_MODEL // 2
    lo = lax.shift_right_logical(lax.bitcast_convert_type(srt[:, :half], i32), 16)
    hi = lax.bitcast_convert_type(srt[:, half:], i32) & jnp.int32(-65536)
    buf[slot] = hi | lo

    def copy_of(s):
        def make(loc, dst, sz):
            return pltpu.make_async_copy(buf.at[s, pl.ds(loc, sz), :], slots_hbm.at[pl.ds(dst, sz), :], sem.at[s])
        return make

    _segment_dmas(i, seg_rows, seg_local, seg_slot, copy_of(slot))

    @pl.when(i > 0)
    def _():
        _tile_wait(tile_rows[i - 1], copy_of(1 - slot))

    @pl.when(i == n_tiles - 1)
    def _():
        _tile_wait(tile_rows[i], copy_of(slot))
        zbuf[...] = jnp.zeros(zbuf.shape, i32)
        for start in (True, False):
            def body(e, carry):
                n = tail_rows[e]
                dst = tail_slot[e]
                for sz in TAIL_CHUNKS:
                    @pl.when((n & sz) != 0)
                    def _():
                        off = pl.multiple_of(n & ~(2 * sz - 1), ROW_ALIGN)
                        cp = pltpu.make_async_copy(zbuf.at[pl.ds(0, sz), :],
                                                   slots_hbm.at[pl.ds(pl.multiple_of(dst + off, ROW_ALIGN), sz), :], zsem)
                        if start:
                            cp.start()
                        else:
                            cp.wait()
                return carry
            lax.fori_loop(0, N_EXPERTS, body, 0)


def _dispatch(xf_p, xf_s, top_e, lbase_v, meta, n_slots):
    n_prompt_tiles = xf_p.shape[0] // TM_DISP
    n_tiles = n_prompt_tiles + 1
    half = D_MODEL // 2
    grid_spec = pltpu.PrefetchScalarGridSpec(
        num_scalar_prefetch=6,
        grid=(n_tiles,),
        in_specs=[pl.BlockSpec((TM_DISP, D_MODEL), lambda i, *_: (jnp.minimum(i, n_prompt_tiles - 1), 0)),
                  pl.BlockSpec((TM_DISP, D_MODEL), lambda i, *_: (0, 0)),
                  pl.BlockSpec((TOP_K, TM_DISP), lambda i, *_: (0, i)),
                  pl.BlockSpec((None, N_EXPERTS, 1), lambda i, *_: (i, 0, 0))],
        out_specs=[pl.BlockSpec((TOP_K, TM_DISP), lambda i, *_: (0, i)),
                   pl.BlockSpec(memory_space=pl.ANY)],
        scratch_shapes=[pltpu.VMEM((2, LOCAL_ROWS, half), i32), pltpu.VMEM((max(TAIL_CHUNKS), half), i32),
                        pltpu.SemaphoreType.DMA((2,)), pltpu.SemaphoreType.DMA(())],
    )
    return pl.pallas_call(
        functools.partial(_dispatch_kernel, n_prompt_tiles=n_prompt_tiles, n_tiles=n_tiles),
        grid_spec=grid_spec,
        out_shape=[jax.ShapeDtypeStruct((TOP_K, n_tiles * TM_DISP), i32),
                   jax.ShapeDtypeStruct((n_slots, half), i32)],
        compiler_params=_cparams(1),
        name="dispatch",
    )(*meta, xf_p, xf_s, top_e, lbase_v)


def _combine_kernel(seg_slot, seg_local, seg_rows, tile_rows,
                    lpos_ref, gate_ref, xa_ref, xb_ref, nfin_ref, y_hbm, ya_ref, yb_ref, buf, sem,
                    *, n_prompt_tiles, n_tiles):
    i = pl.program_id(0)
    slot = i % 2

    def copy_of(s):
        def make(loc, src, sz):
            return pltpu.make_async_copy(y_hbm.at[pl.ds(src, sz), :], buf.at[s, pl.ds(loc, sz), :], sem.at[s])
        return make

    @pl.when(i == 0)
    def _():
        buf[...] = jnp.zeros(buf.shape, f32)
        _segment_dmas(0, seg_rows, seg_local, seg_slot, copy_of(0))

    @pl.when(i + 1 < n_tiles)
    def _():
        _segment_dmas(i + 1, seg_rows, seg_local, seg_slot, copy_of(1 - slot))

    _tile_wait(tile_rows[i], copy_of(slot))

    lpos = lpos_ref[...]
    gate = gate_ref[...]
    cidx = lax.broadcasted_iota(i32, (TM_DISP, LOCAL_ROWS), 1)
    wmat = jnp.zeros((TM_DISP, LOCAL_ROWS), f32)
    for k in range(TOP_K):
        wmat = jnp.where(cidx == lpos[:, k:k + 1], gate[:, k:k + 1], wmat)
    moe = jnp.dot(wmat.astype(bf16), buf[slot].astype(bf16), preferred_element_type=f32)
    x1 = jnp.where(i < n_prompt_tiles, xa_ref[...], xb_ref[...])
    y = _rms(x1 + moe, nfin_ref[...])

    @pl.when(i < n_prompt_tiles)
    def _():
        ya_ref[...] = y

    @pl.when(i >= n_prompt_tiles)
    def _():
        yb_ref[...] = y


def _combine(y_slots, lpos_t, gate_t, x1_p, x1_s, norm_final, meta):
    n_prompt_tiles = x1_p.shape[0] // TM_DISP
    n_tiles = n_prompt_tiles + 1
    last = n_prompt_tiles - 1
    grid_spec = pltpu.PrefetchScalarGridSpec(
        num_scalar_prefetch=4,
        grid=(n_tiles,),
        in_specs=[pl.BlockSpec((TM_DISP, TOP_K), lambda i, *_: (i, 0)),
                  pl.BlockSpec((TM_DISP, TOP_K), lambda i, *_: (i, 0)),
                  pl.BlockSpec((TM_DISP, D_MODEL), lambda i, *_: (jnp.minimum(i, last), 0)),
                  pl.BlockSpec((TM_DISP, D_MODEL), lambda i, *_: (0, 0)),
                  pl.BlockSpec((1, D_MODEL), lambda i, *_: (0, 0)),
                  pl.BlockSpec(memory_space=pl.ANY)],
        out_specs=[pl.BlockSpec((TM_DISP, D_MODEL), lambda i, *_: (jnp.minimum(i, last), 0)),
                   pl.BlockSpec((TM_DISP, D_MODEL), lambda i, *_: (0, 0))],
        scratch_shapes=[pltpu.VMEM((2, LOCAL_ROWS, D_MODEL), f32), pltpu.SemaphoreType.DMA((2,))],
    )
    return pl.pallas_call(
        functools.partial(_combine_kernel, n_prompt_tiles=n_prompt_tiles, n_tiles=n_tiles),
        grid_spec=grid_spec,
        out_shape=[jax.ShapeDtypeStruct(x1_p.shape, f32), jax.ShapeDtypeStruct(x1_s.shape, f32)],
        compiler_params=_cparams(1),
        name="combine",
    )(*meta, lpos_t, gate_t, x1_p, x1_s, norm_final, y_slots)


ROW_DMA_PRIORITY = 1


def _experts_kernel(first_blk, n_blk, n_total, xs_hbm, wgu_ref, bgu_ref, wd_ref, bd_ref, y_hbm,
                    wgu_bf, wd_bf, xbuf, ybuf, xsem, ysem):
    e = pl.program_id(0)
    nb = n_blk[e]
    g0 = first_blk[e]
    total = n_total[0]

    def rows(g):
        return pl.ds(pl.multiple_of(g * R_BLK, R_BLK), R_BLK)

    def x_copy(g, slot):
        return pltpu.make_async_copy(xs_hbm.at[rows(g), :], xbuf.at[slot], xsem.at[slot])

    def y_copy(g, slot):
        return pltpu.make_async_copy(ybuf.at[slot], y_hbm.at[rows(g), :], ysem.at[slot])

    @pl.when(e == 0)
    def _():
        x_copy(0, 0).start(priority=ROW_DMA_PRIORITY)

    @pl.when(nb > 0)
    def _():
        wgu_bf[...] = wgu_ref[...].astype(bf16)
        wd_bf[...] = wd_ref[...].astype(bf16)

        def body(j, carry):
            blk = g0 + j
            slot = blk % 2

            @pl.when(blk + 1 < total)
            def _():
                x_copy(blk + 1, 1 - slot).start(priority=ROW_DMA_PRIORITY)

            x_copy(blk, slot).wait()

            @pl.when(blk >= 2)
            def _():
                y_copy(blk - 2, slot).wait()

            pk = xbuf[slot]
            lo = lax.bitcast_convert_type(lax.shift_left(pk, 16), f32).astype(bf16)
            hi = lax.bitcast_convert_type(pk & jnp.int32(-65536), f32).astype(bf16)
            x = jnp.concatenate([lo, hi], axis=1)
            gu = jnp.dot(x, wgu_bf[...], preferred_element_type=f32) + bgu_ref[...]
            g = jnp.minimum(gu[:, :D_FF], SWIGLU_LIMIT)
            up = jnp.clip(gu[:, D_FF:], -SWIGLU_LIMIT, SWIGLU_LIMIT)
            h = (up + 1.0) * (g * jax.nn.sigmoid(SWIGLU_ALPHA * g))
            ybuf[slot] = jnp.dot(h.astype(bf16), wd_bf[...], preferred_element_type=f32) + bd_ref[...]
            y_copy(blk, slot).start(priority=ROW_DMA_PRIORITY)
            return carry

        lax.fori_loop(0, nb, body, 0)

    @pl.when(e == N_EXPERTS - 1)
    def _():
        @pl.when(total >= 2)
        def _():
            y_copy(total - 2, total % 2).wait()

        y_copy(total - 1, (total - 1) % 2).wait()


def _experts(x_slots, w_gu, b_gu, w_down, b_down, first_blk, n_blk, n_total):
    half = D_MODEL // 2
    wsel = lambda e, *_: (e, 0, 0)
    grid_spec = pltpu.PrefetchScalarGridSpec(
        num_scalar_prefetch=3,
        grid=(N_EXPERTS,),
        in_specs=[pl.BlockSpec(memory_space=pl.ANY),
                  pl.BlockSpec((None, D_MODEL, 2 * D_FF), wsel),
                  pl.BlockSpec((None, 1, 2 * D_FF), wsel),
                  pl.BlockSpec((None, D_FF, D_MODEL), wsel),
                  pl.BlockSpec((None, 1, D_MODEL), wsel)],
        out_specs=pl.BlockSpec(memory_space=pl.ANY),
        scratch_shapes=[pltpu.VMEM((D_MODEL, 2 * D_FF), bf16), pltpu.VMEM((D_FF, D_MODEL), bf16),
                        pltpu.VMEM((2, R_BLK, half), i32), pltpu.VMEM((2, R_BLK, D_MODEL), f32),
                        pltpu.SemaphoreType.DMA((2,)), pltpu.SemaphoreType.DMA((2,))],
    )
    return pl.pallas_call(
        _experts_kernel,
        grid_spec=grid_spec,
        out_shape=jax.ShapeDtypeStruct((x_slots.shape[0], D_MODEL), f32),
        compiler_params=_cparams(1),
        name="experts",
    )(first_blk, n_blk, n_total, x_slots, w_gu, b_gu.reshape(N_EXPERTS, 1, 2 * D_FF), w_down,
      b_down.reshape(N_EXPERTS, 1, D_MODEL))


def _slot_layout(counts):
    seg_rows = (counts + ROW_ALIGN - 1) // ROW_ALIGN * ROW_ALIGN
    seg_local = jnp.cumsum(seg_rows, axis=1) - seg_rows
    total = jnp.sum(seg_rows, axis=0)
    region = (total + R_BLK - 1) // R_BLK * R_BLK
    region_end = jnp.cumsum(region)
    ebase = region_end - region
    seg_slot = ebase[None, :] + jnp.cumsum(seg_rows, axis=0) - seg_rows
    flat = lambda a: a.reshape(-1).astype(i32)
    return dict(seg_slot=flat(seg_slot), seg_local=flat(seg_local), seg_rows=flat(seg_rows),
                tile_rows=flat(jnp.sum(seg_rows, axis=1)),
                tail_slot=flat(ebase + total), tail_rows=flat(region - total),
                first_blk=flat(ebase // R_BLK), n_blk=flat(region // R_BLK),
                n_total=flat(region_end[-1:] // R_BLK),
                lbase_v=seg_local.astype(i32)[:, :, None])


def kernel(x_prompt, x_sample, cache_k, cache_v, cache_conv, norm_mix, w_in, attn_sinks, conv_dw_w, conv_dw_b,
           conv_ln_g, conv_ln_b, w_pw, b_pw, w_out, norm_ffn, w_router, b_router, w_gu, b_gu, w_down, b_down,
           norm_final):
    depth = norm_mix.shape[0]
    assert depth == 1, "one layer per step"
    batch, seq, _ = x_prompt.shape
    dbatch, dseq, _ = x_sample.shape
    window = cache_k.shape[2]
    n_p, n_s = batch * seq, dbatch * dseq
    assert seq % TM_PROJ == 0 and seq % (ATTN_CHUNKS * CHUNK) == 0 and seq >= SPAN and n_s <= TM_DISP
    assert n_p % TM_DISP == 0 and TM_DISP % TM_MIX == 0 and seq % TM_MIX == 0
    l = 0
    row = lambda a: a.reshape(1, -1)

    w_in_bf = w_in[l].astype(bf16)
    tabs_p = _rope_tables(jnp.arange(seq, dtype=f32))
    tabs_s = _rope_tables(PAST_LEN + jnp.arange(dseq, dtype=f32))
    tabs_s = tuple(jnp.tile(t, (dbatch, 1)) for t in tabs_s)
    xp2 = x_prompt.reshape(n_p, D_MODEL)
    xs2 = x_sample.reshape(n_s, D_MODEL)
    q_p, kv_p, u_p, ga_p, gc_p, ut_p = _in_proj(xp2, row(norm_mix[l]), w_in_bf, tabs_p, TM_PROJ, seq // TM_PROJ, bf16)
    q_s, kv_s, u_s, ga_s, gc_s, _ = _in_proj(xs2, row(norm_mix[l]), w_in_bf, tabs_s, n_s, 1, f32)

    sinks = attn_sinks[l].astype(f32).reshape(N_KV_HEADS, 1, GROUP)
    attn_p = _attn_prompt(q_p, kv_p, jnp.repeat(sinks, CHUNK, axis=2), batch, seq)
    qpos = PAST_LEN + np.arange(dseq)
    kpos = PAST_LEN - window + np.arange(window + dseq)
    qch, kch = qpos // CHUNK, kpos // CHUNK
    mask = (kch[None, :] >= qch[:, None] - WINDOW_CHUNKS) & (kch[None, :] <= qch[:, None]) & (kpos[None, :] >= 0)
    mask_rows = jnp.asarray(np.tile(mask.astype(np.float32).T, (1, GROUP)))
    ck = cache_k[l].reshape(dbatch, window, KV_W)
    cv = cache_v[l].reshape(dbatch, window, KV_W)
    attn_s = _attn_sample(q_s, kv_s, ck, cv, mask_rows, jnp.repeat(sinks, dseq, axis=2), dbatch, dseq)

    conv_w = jnp.broadcast_to(conv_dw_w[l][:, None, :], (CONV_WIDTH, SUBLANES, CONV_CH))
    mix_w = (conv_w, row(conv_dw_b[l]), row(conv_ln_g[l]),
             row(conv_ln_b[l]), w_pw[l].astype(bf16), row(b_pw[l]), w_out[l].astype(bf16), row(norm_ffn[l]),
             w_router[l].T.astype(bf16), b_router[l].astype(f32).reshape(N_EXPERTS, 1))
    x1_p, xf_p, te_p, gt_p, cnt_p = _mix_prompt(u_p, attn_p, ga_p, gc_p, xp2, mix_w, seq // TM_MIX)
    x1_s, xf_s, te_s, gt_s, cnt_s = _mix_sample(u_s, cache_conv[l], attn_s, ga_s, gc_s, xs2, mix_w, dbatch, dseq)

    n_tiles = n_p // TM_DISP + 1
    n_all = n_tiles * TM_DISP
    n_blocks = -(-(TOP_K * n_all + (ROW_ALIGN - 1) * N_EXPERTS * n_tiles + N_EXPERTS * (R_BLK - ROW_ALIGN)) // R_BLK)
    cnt_p = cnt_p.reshape(n_p // TM_DISP, TM_DISP // TM_MIX, N_EXPERTS).sum(axis=1)
    lay = _slot_layout(jnp.concatenate([cnt_p, cnt_s[:, :, 0]], axis=0))
    top_e = jnp.concatenate([te_p, te_s], axis=1)
    gates = jnp.concatenate([gt_p, gt_s], axis=1)
    lpos, x_slots = _dispatch(xf_p, xf_s, top_e, lay["lbase_v"],
                              (lay["seg_slot"], lay["seg_local"], lay["seg_rows"], lay["tile_rows"],
                               lay["tail_slot"], lay["tail_rows"]),
                              n_blocks * R_BLK)
    y_slots = _experts(x_slots, w_gu[l], b_gu[l], w_down[l], b_down[l], lay["first_blk"], lay["n_blk"],
                       lay["n_total"])
    y_p, y_s = _combine(y_slots, lpos.T, gates.T, x1_p, x1_s, row(norm_final),
                        (lay["seg_slot"], lay["seg_local"], lay["seg_rows"], lay["tile_rows"]))

    y_prompt = y_p.reshape(batch, seq, D_MODEL)
    y_sample = y_s[:n_s].reshape(dbatch, dseq, D_MODEL)
    kv_tail = kv_p.reshape(batch, seq, 2 * KV_W)[:, seq - window:]
    new_k_p = kv_tail[:, :, :KV_W].reshape(1, batch, window, N_KV_HEADS, HEAD_DIM)
    new_v_p = kv_tail[:, :, KV_W:].reshape(1, batch, window, N_KV_HEADS, HEAD_DIM)
    tiles_per_seq = seq // TM_PROJ
    new_conv_p = ut_p.reshape(batch, tiles_per_seq, HALO, CONV_CH)[:, -1, HALO - (CONV_WIDTH - 1):][None]
    kv_s4 = kv_s.reshape(dbatch, dseq, 2, N_KV_HEADS, HEAD_DIM)
    new_k_s = jnp.concatenate([cache_k[l], kv_s4[:, :, 0]], axis=1)[:, -window:][None]
    new_v_s = jnp.concatenate([cache_v[l], kv_s4[:, :, 1]], axis=1)[:, -window:][None]
    new_conv_s = jnp.concatenate([cache_conv[l], u_s.reshape(dbatch, dseq, CONV_CH)], axis=1)[:, -(CONV_WIDTH - 1):][None]
    return (y_prompt, y_sample, new_k_p, new_v_p, new_conv_p, new_k_s, new_v_s, new_conv_s)
```

```python
import functools

import numpy as np
import jax
import jax.numpy as jnp
from jax import lax
from jax.experimental import pallas as pl
from jax.experimental.pallas import tpu as pltpu

f32 = jnp.float32
bf16 = jnp.bfloat16
i32 = jnp.int32

D_MODEL = 1024
PAST_LEN = 1024
CHUNK = 64
N_HEADS = 16
N_KV_HEADS = 2
HEAD_DIM = 64
GROUP = N_HEADS // N_KV_HEADS
ROT_DIM = HEAD_DIM // 4
ROPE_THETA = 500000.0
WINDOW = 128
WINDOW_CHUNKS = WINDOW // CHUNK
SPAN = (WINDOW_CHUNKS + 1) * CHUNK
CONV_CH = D_MODEL
CONV_WIDTH = 31
N_EXPERTS = 32
TOP_K = 4
D_FF = D_MODEL
SWIGLU_LIMIT = 7.0
SWIGLU_ALPHA = 1.702
NORM_EPS = 1e-5
NEG_INF = -1e30
Q_W = N_HEADS * HEAD_DIM
KV_W = N_KV_HEADS * HEAD_DIM
IN_COLS = Q_W + 2 * KV_W + 2 * CONV_CH + 2 * D_MODEL
Q_SCALE = HEAD_DIM ** -0.5

LANES = 128
SUBLANES = 8
VMEM_LIMIT = 56 * 1024 * 1024

HALO = 32
TM_PROJ = 1024
TM_MIX = 512
TM_DISP = 512
ATTN_CHUNKS = 16
ROW_ALIGN = SUBLANES
R_BLK = 512
LOCAL_ROWS = -(-(TOP_K * TM_DISP + N_EXPERTS * (ROW_ALIGN - 1)) // LANES) * LANES
SEG_CHUNKS = tuple(2 ** p for p in range(int(np.log2(TM_DISP)), int(np.log2(ROW_ALIGN)) - 1, -1))
WAIT_CHUNKS = tuple(2 ** p for p in range(int(np.log2(LOCAL_ROWS)), int(np.log2(ROW_ALIGN)) - 1, -1))
TAIL_CHUNKS = tuple(c for c in SEG_CHUNKS if c < R_BLK)


def _cparams(n_axes):
    return pltpu.CompilerParams(dimension_semantics=("arbitrary",) * n_axes,
                                vmem_limit_bytes=VMEM_LIMIT)


def _rms(x, g):
    return x * lax.rsqrt(jnp.mean(x * x, axis=-1, keepdims=True) + NORM_EPS) * g


def _inproj_kernel(x_ref, g_ref, w_ref, cos_ref, sa_ref, sb_ref,
                   q_ref, kv_ref, u_ref, ga_ref, gc_ref, ut_ref):
    xn = _rms(x_ref[...], g_ref[...]).astype(bf16)
    cos = cos_ref[...]
    sa = sa_ref[...]
    sb = sb_ref[...]

    def rope(t):
        return t * cos + pltpu.roll(t, LANES - ROT_DIM // 2, 1) * sa + pltpu.roll(t, ROT_DIM // 2, 1) * sb

    def proj(c0, n):
        return jnp.dot(xn, w_ref[:, c0:c0 + n], preferred_element_type=f32)

    q = proj(0, Q_W)
    for c in range(Q_W // LANES):
        q_ref[:, c * LANES:(c + 1) * LANES] = (rope(q[:, c * LANES:(c + 1) * LANES]) * Q_SCALE).astype(bf16)
    kv = proj(Q_W, 2 * KV_W)
    kv_ref[:, :KV_W] = rope(kv[:, :KV_W])
    kv_ref[:, KV_W:] = kv[:, KV_W:]
    c0 = Q_W + 2 * KV_W
    u = proj(c0, CONV_CH) * jax.nn.sigmoid(proj(c0 + CONV_CH, CONV_CH))
    u_ref[...] = u.astype(u_ref.dtype)
    ut_ref[...] = u[u.shape[0] - HALO:, :]
    c0 += 2 * CONV_CH
    ga_ref[...] = jax.nn.sigmoid(proj(c0, D_MODEL)).astype(ga_ref.dtype)
    gc_ref[...] = jax.nn.sigmoid(proj(c0 + D_MODEL, D_MODEL)).astype(gc_ref.dtype)


def _rope_tables(pos):
    half = ROT_DIM // 2
    inv = ROPE_THETA ** (-jnp.arange(half, dtype=f32) * 2.0 / ROT_DIM)
    ang = pos[:, None] * inv[None, :]
    cos, sin = jnp.cos(ang), jnp.sin(ang)
    t = pos.shape[0]
    ones = jnp.ones((t, HEAD_DIM - ROT_DIM), f32)
    zeros = jnp.zeros((t, HEAD_DIM - ROT_DIM), f32)
    zh = jnp.zeros((t, half), f32)
    c = jnp.concatenate([cos, cos, ones], axis=1)
    a = jnp.concatenate([-sin, zh, zeros], axis=1)
    b = jnp.concatenate([zh, sin, zeros], axis=1)
    rep = LANES // HEAD_DIM
    return jnp.tile(c, (1, rep)), jnp.tile(a, (1, rep)), jnp.tile(b, (1, rep))


def _in_proj(x2d, norm_g, w_bf, tables, tm, tiles_per_seq, act_dtype):
    n = x2d.shape[0]
    nt = n // tm
    row = lambda i: (i, 0)
    const = lambda i: (0, 0)
    tab = lambda i: (i % tiles_per_seq, 0)
    return pl.pallas_call(
        _inproj_kernel,
        grid=(nt,),
        in_specs=[pl.BlockSpec((tm, D_MODEL), row),
                  pl.BlockSpec((1, D_MODEL), const),
                  pl.BlockSpec((D_MODEL, IN_COLS), const, pipeline_mode=pl.Buffered(1)),
                  pl.BlockSpec((tm, LANES), tab),
                  pl.BlockSpec((tm, LANES), tab),
                  pl.BlockSpec((tm, LANES), tab)],
        out_specs=[pl.BlockSpec((tm, Q_W), row),
                   pl.BlockSpec((tm, 2 * KV_W), row),
                   pl.BlockSpec((tm, CONV_CH), row),
                   pl.BlockSpec((tm, D_MODEL), row),
                   pl.BlockSpec((tm, D_MODEL), row),
                   pl.BlockSpec((None, HALO, CONV_CH), lambda i: (i, 0, 0))],
        out_shape=[jax.ShapeDtypeStruct((n, Q_W), bf16),
                   jax.ShapeDtypeStruct((n, 2 * KV_W), f32),
                   jax.ShapeDtypeStruct((n, CONV_CH), act_dtype),
                   jax.ShapeDtypeStruct((n, D_MODEL), act_dtype),
                   jax.ShapeDtypeStruct((n, D_MODEL), act_dtype),
                   jax.ShapeDtypeStruct((nt, HALO, CONV_CH), f32)],
        compiler_params=_cparams(1),
        name="in_proj",
    )(x2d, norm_g, w_bf, *tables)


def _attn_group(qg, kg, vg, mask_t, sink):
    s = lax.dot_general(kg, qg, (((1,), (1,)), ((), ())), preferred_element_type=f32)
    if mask_t is not None:
        s = jnp.where(mask_t, s, NEG_INF)
    m = jnp.maximum(jnp.max(s, axis=0, keepdims=True), sink)
    p = jnp.exp(s - m)
    denom = jnp.sum(p, axis=0, keepdims=True) + jnp.exp(sink - m)
    pn = (p / denom).astype(bf16)
    return lax.dot_general(pn, vg, (((0,), (0,)), ((), ())), preferred_element_type=f32)


def _heads_to_rows(q, g):
    base = g * GROUP * HEAD_DIM
    return jnp.concatenate([q[:, base + h * HEAD_DIM: base + (h + 1) * HEAD_DIM] for h in range(GROUP)], axis=0)


def _rows_to_heads(o, tq):
    return [o[h * tq:(h + 1) * tq, :] for h in range(GROUP)]


def _attn_prompt_kernel(q_ref, kv_ref, sink_ref, o_ref):
    j = pl.program_id(1)
    for c in range(ATTN_CHUNKS):
        n = j * ATTN_CHUNKS + c
        first = jnp.maximum(n - WINDOW_CHUNKS, 0)
        win = kv_ref[pl.ds(pl.multiple_of(first * CHUNK, CHUNK), SPAN), :]
        if c < WINDOW_CHUNKS:
            mask = lax.broadcasted_iota(i32, (SPAN, GROUP * CHUNK), 0) < (n - first + 1) * CHUNK
        else:
            mask = None
        q = q_ref[c * CHUNK:(c + 1) * CHUNK, :]
        outs = []
        for g in range(N_KV_HEADS):
            kg = win[:, g * HEAD_DIM:(g + 1) * HEAD_DIM].astype(bf16)
            vg = win[:, KV_W + g * HEAD_DIM: KV_W + (g + 1) * HEAD_DIM].astype(bf16)
            o = _attn_group(_heads_to_rows(q, g), kg, vg, mask, sink_ref[g])
            outs += _rows_to_heads(o, CHUNK)
        o_ref[c * CHUNK:(c + 1) * CHUNK, :] = jnp.concatenate(outs, axis=1).astype(bf16)


def _attn_prompt(q, kv, sink_rows, batch, seq):
    qrows = ATTN_CHUNKS * CHUNK
    steps = seq // qrows
    return pl.pallas_call(
        _attn_prompt_kernel,
        grid=(batch, steps),
        in_specs=[pl.BlockSpec((qrows, Q_W), lambda b, j: (b * steps + j, 0)),
                  pl.BlockSpec((seq, 2 * KV_W), lambda b, j: (b, 0)),
                  pl.BlockSpec((N_KV_HEADS, 1, GROUP * CHUNK), lambda b, j: (0, 0, 0))],
        out_specs=pl.BlockSpec((qrows, Q_W), lambda b, j: (b * steps + j, 0)),
        out_shape=jax.ShapeDtypeStruct((batch * seq, Q_W), bf16),
        compiler_params=_cparams(2),
        name="attn_prompt",
    )(q, kv, sink_rows)


def _attn_sample_kernel(q_ref, kvn_ref, ck_ref, cv_ref, mask_ref, sink_ref, o_ref):
    tq = q_ref.shape[0]
    q = q_ref[...]
    kvn = kvn_ref[...]
    ck = ck_ref[...]
    cv = cv_ref[...]
    mask = mask_ref[...] > 0.5
    outs = []
    for g in range(N_KV_HEADS):
        sl = slice(g * HEAD_DIM, (g + 1) * HEAD_DIM)
        kg = jnp.concatenate([ck[:, sl], kvn[:, sl]], axis=0).astype(bf16)
        vg = jnp.concatenate([cv[:, sl], kvn[:, KV_W + g * HEAD_DIM: KV_W + (g + 1) * HEAD_DIM]], axis=0).astype(bf16)
        o = _attn_group(_heads_to_rows(q, g), kg, vg, mask, sink_ref[g])
        outs += _rows_to_heads(o, tq)
    o_ref[...] = jnp.concatenate(outs, axis=1)


def _attn_sample(q, kv_new, cache_k, cache_v, mask_rows, sink_rows, batch, tq):
    w = cache_k.shape[1]
    return pl.pallas_call(
        _attn_sample_kernel,
        grid=(batch,),
        in_specs=[pl.BlockSpec((tq, Q_W), lambda b: (b, 0)),
                  pl.BlockSpec((tq, 2 * KV_W), lambda b: (b, 0)),
                  pl.BlockSpec((None, w, KV_W), lambda b: (b, 0, 0)),
                  pl.BlockSpec((None, w, KV_W), lambda b: (b, 0, 0)),
                  pl.BlockSpec((w + tq, GROUP * tq), lambda b: (0, 0)),
                  pl.BlockSpec((N_KV_HEADS, 1, GROUP * tq), lambda b: (0, 0, 0))],
        out_specs=pl.BlockSpec((tq, Q_W), lambda b: (b, 0)),
        out_shape=jax.ShapeDtypeStruct((batch * tq, Q_W), f32),
        compiler_params=_cparams(1),
        name="attn_sample",
    )(q, kv_new, cache_k, cache_v, mask_rows, sink_rows)


def _dwconv(s_ref, row0, n_rows, w_ref, b_ref, y_ref, yrow0, rc):
    lead = HALO - (CONV_WIDTH - 1)
    offsets = range(lead, lead + CONV_WIDTH)

    def lane_body(lc, carry):
        c0 = pl.multiple_of(lc * LANES, LANES)
        cols = pl.ds(c0, LANES)
        bias = b_ref[:, cols]
        groups = rc // SUBLANES
        for t0 in range(0, n_rows, rc):
            acc = jnp.broadcast_to(bias[None], (groups, SUBLANES, LANES))
            for r in range(SUBLANES):
                offs = [o for o in offsets if o % SUBLANES == r]
                if not offs:
                    continue
                amax = max(offs) // SUBLANES
                z = s_ref[pl.ds(row0 + t0 + r, rc + SUBLANES * amax), cols].reshape(groups + amax, SUBLANES, LANES)
                for o in offs:
                    a = o // SUBLANES
                    acc = acc + z[a: a + groups] * w_ref[o - lead, :, cols][None]
            y_ref[pl.ds(yrow0 + t0, rc), cols] = acc.reshape(rc, LANES)
        return carry

    lax.fori_loop(0, CONV_CH // LANES, lane_body, 0)


def _mix_tail(y, attn, ga, gc, x, valid, lg_ref, lb_ref, wpw_ref, bpw_ref, wout_ref, nf_ref, wr_ref, br_ref):
    tm = y.shape[0]
    mu = jnp.mean(y, axis=-1, keepdims=True)
    d = y - mu
    var = jnp.mean(d * d, axis=-1, keepdims=True)
    yn = d * lax.rsqrt(var + NORM_EPS) * lg_ref[...] + lb_ref[...]
    act = (yn * jax.nn.sigmoid(yn)).astype(bf16)
    conv_out = jnp.dot(act, wpw_ref[...], preferred_element_type=f32) + bpw_ref[...]
    h = (ga.astype(f32) * attn.astype(f32) + gc.astype(f32) * conv_out).astype(bf16)
    x1 = x + jnp.dot(h, wout_ref[...], preferred_element_type=f32)
    xf = _rms(x1, nf_ref[...]).astype(bf16)
    logits = lax.dot_general(wr_ref[...], xf, (((1,), (1,)), ((), ())), preferred_element_type=f32) + br_ref[...]
    eidx = lax.broadcasted_iota(i32, (N_EXPERTS, tm), 0)
    routed = jnp.zeros((N_EXPERTS, tm), jnp.bool_)
    top_l, top_e = [], []
    l = logits
    for _ in range(TOP_K):
        m = jnp.max(l, axis=0, keepdims=True)
        idx = jnp.min(jnp.where(l == m, eidx, N_EXPERTS), axis=0, keepdims=True)
        sel = eidx == idx
        routed = routed | sel
        l = jnp.where(sel, -jnp.inf, l)
        top_l.append(m)
        top_e.append(idx)
    ex = [jnp.exp(t - top_l[0]) for t in top_l]
    tot = ex[0] + ex[1] + ex[2] + ex[3]
    gates = jnp.concatenate([e / tot for e in ex], axis=0)
    top_e = jnp.concatenate(top_e, axis=0)
    if valid is not None:
        routed = routed & valid
        top_e = jnp.where(valid, top_e, -1)
        gates = jnp.where(valid, gates, 0.0)
    counts = jnp.sum(jnp.where(routed, 1, 0).astype(i32), axis=1, keepdims=True)
    return x1, xf, top_e, gates, counts


def _mix_prompt_kernel(u_ref, halo_ref, attn_ref, ga_ref, gc_ref, x_ref, cw_ref, cb_ref, lg_ref, lb_ref,
                       wpw_ref, bpw_ref, wout_ref, nf_ref, wr_ref, br_ref,
                       x1_ref, xf_ref, te_ref, gt_ref, cnt_ref, s_scr, y_scr, *, tiles_per_seq):
    i = pl.program_id(0)
    first = (i % tiles_per_seq) == 0
    s_scr[0:HALO, :] = jnp.where(first, 0.0, halo_ref[...].astype(f32))
    s_scr[HALO:, :] = u_ref[...].astype(f32)
    _dwconv(s_scr, 0, TM_MIX, cw_ref, cb_ref, y_scr, 0, 128)
    x1, xf, top_e, gates, counts = _mix_tail(
        y_scr[...], attn_ref[...], ga_ref[...], gc_ref[...], x_ref[...], None,
        lg_ref, lb_ref, wpw_ref, bpw_ref, wout_ref, nf_ref, wr_ref, br_ref)
    x1_ref[...] = x1
    xf_ref[...] = xf
    te_ref[...] = top_e
    gt_ref[...] = gates
    cnt_ref[...] = counts


def _mix_sample_kernel(u_ref, hist_ref, attn_ref, ga_ref, gc_ref, x_ref, cw_ref, cb_ref, lg_ref, lb_ref,
                       wpw_ref, bpw_ref, wout_ref, nf_ref, wr_ref, br_ref,
                       x1_ref, xf_ref, te_ref, gt_ref, cnt_ref, s_scr, y_scr, *, batch, tq):
    n = batch * tq
    hist_rows = CONV_WIDTH - 1
    lead = HALO - hist_rows
    stride = HALO + tq
    s_scr[...] = jnp.zeros(s_scr.shape, f32)
    as_conv_input = lambda a: a.astype(bf16).astype(f32)
    for b in range(batch):
        s_scr[b * stride + lead: b * stride + HALO, :] = as_conv_input(hist_ref[b])
        s_scr[b * stride + HALO: (b + 1) * stride, :] = as_conv_input(u_ref[b * tq:(b + 1) * tq, :])
        _dwconv(s_scr, b * stride, tq, cw_ref, cb_ref, y_scr, b * tq, tq)
    x1, xf, top_e, gates, counts = _mix_tail(
        y_scr[...], attn_ref[...], ga_ref[...], gc_ref[...], x_ref[...], None,
        lg_ref, lb_ref, wpw_ref, bpw_ref, wout_ref, nf_ref, wr_ref, br_ref)
    x1_ref[...] = jnp.zeros(x1_ref.shape, f32)
    xf_ref[...] = jnp.zeros(xf_ref.shape, bf16)
    te_ref[...] = jnp.full(te_ref.shape, -1, i32)
    gt_ref[...] = jnp.zeros(gt_ref.shape, f32)
    x1_ref[0:n, :] = x1
    xf_ref[0:n, :] = xf
    te_ref[:, 0:n] = top_e
    gt_ref[:, 0:n] = gates
    cnt_ref[0] = counts


def _mix_weight_specs():
    shapes = [(CONV_WIDTH, SUBLANES, CONV_CH), (1, CONV_CH), (1, CONV_CH), (1, CONV_CH), (CONV_CH, D_MODEL),
              (1, D_MODEL), (D_MODEL, D_MODEL), (1, D_MODEL), (N_EXPERTS, D_MODEL), (N_EXPERTS, 1)]
    return [pl.BlockSpec(s, functools.partial(lambda nd, i: (0,) * nd, len(s))) for s in shapes]


def _mix_out(n, nt):
    shapes = [jax.ShapeDtypeStruct((n, D_MODEL), f32), jax.ShapeDtypeStruct((n, D_MODEL), bf16),
              jax.ShapeDtypeStruct((TOP_K, n), i32), jax.ShapeDtypeStruct((TOP_K, n), f32),
              jax.ShapeDtypeStruct((nt, N_EXPERTS, 1), i32)]
    return shapes


def _mix_prompt(u, attn, ga, gc, x2d, weights, tiles_per_seq):
    n = x2d.shape[0]
    nt = n // TM_MIX
    row = lambda i: (i, 0)
    halo = lambda i: (jnp.maximum(i * (TM_MIX // HALO) - 1, 0), 0)
    tok = lambda i: (0, i)
    return pl.pallas_call(
        functools.partial(_mix_prompt_kernel, tiles_per_seq=tiles_per_seq),
        grid=(nt,),
        in_specs=[pl.BlockSpec((TM_MIX, CONV_CH), row), pl.BlockSpec((HALO, CONV_CH), halo),
                  pl.BlockSpec((TM_MIX, Q_W), row), pl.BlockSpec((TM_MIX, D_MODEL), row),
                  pl.BlockSpec((TM_MIX, D_MODEL), row), pl.BlockSpec((TM_MIX, D_MODEL), row)] + _mix_weight_specs(),
        out_specs=[pl.BlockSpec((TM_MIX, D_MODEL), row), pl.BlockSpec((TM_MIX, D_MODEL), row),
                   pl.BlockSpec((TOP_K, TM_MIX), tok), pl.BlockSpec((TOP_K, TM_MIX), tok),
                   pl.BlockSpec((None, N_EXPERTS, 1), lambda i: (i, 0, 0))],
        out_shape=_mix_out(n, nt),
        scratch_shapes=[pltpu.VMEM((TM_MIX + HALO, CONV_CH), f32), pltpu.VMEM((TM_MIX, CONV_CH), f32)],
        compiler_params=_cparams(1),
        name="mix_prompt",
    )(u, u, attn, ga, gc, x2d, *weights)


def _mix_sample(u, hist, attn, ga, gc, x2d, weights, batch, tq):
    return pl.pallas_call(
        functools.partial(_mix_sample_kernel, batch=batch, tq=tq),
        out_shape=_mix_out(TM_DISP, 1),
        scratch_shapes=[pltpu.VMEM((batch * (HALO + tq), CONV_CH), f32), pltpu.VMEM((batch * tq, CONV_CH), f32)],
        compiler_params=pltpu.CompilerParams(vmem_limit_bytes=VMEM_LIMIT),
        name="mix_sample",
    )(u, hist, attn, ga, gc, x2d, *weights)


def _segment_dmas(tile, seg_rows, seg_local, seg_slot, make_copy):
    def body(e, carry):
        idx = tile * N_EXPERTS + e
        n = seg_rows[idx]
        loc = seg_local[idx]
        dst = seg_slot[idx]
        for sz in SEG_CHUNKS:
            @pl.when((n & sz) != 0)
            def _():
                off = pl.multiple_of(n & ~(2 * sz - 1), ROW_ALIGN)
                make_copy(pl.multiple_of(loc + off, ROW_ALIGN), pl.multiple_of(dst + off, ROW_ALIGN), sz).start()
        return carry

    lax.fori_loop(0, N_EXPERTS, body, 0)


def _tile_wait(n_rows, make_copy):
    for sz in WAIT_CHUNKS:
        @pl.when((n_rows & sz) != 0)
        def _():
            make_copy(0, 0, sz).wait()


def _dispatch_kernel(seg_slot, seg_local, seg_rows, tile_rows, tail_slot, tail_rows,
                     xa_ref, xb_ref, te_ref, lb_ref, lpos_ref, slots_hbm, buf, zbuf, sem, zsem,
                     *, n_prompt_tiles, n_tiles):
    i = pl.program_id(0)
    slot = i % 2
    x = jnp.where(i < n_prompt_tiles, xa_ref[...], xb_ref[...])
    te = te_ref[...]
    eidx = lax.broadcasted_iota(i32, (N_EXPERTS, TM_DISP), 0)
    hits = [te[k:k + 1, :] == eidx for k in range(TOP_K)]
    routed = hits[0] | hits[1] | hits[2] | hits[3]
    before = lax.broadcasted_iota(i32, (TM_DISP, TM_DISP), 0) < lax.broadcasted_iota(i32, (TM_DISP, TM_DISP), 1)
    rank = jnp.dot(jnp.where(routed, 1.0, 0.0).astype(bf16), jnp.where(before, 1.0, 0.0).astype(bf16),
                   preferred_element_type=f32)
    pos = lb_ref[...] + rank.astype(i32)
    lpos = []
    for k in range(TOP_K):
        p = jnp.sum(jnp.where(hits[k], pos, 0), axis=0, keepdims=True)
        lpos.append(jnp.where(te[k:k + 1, :] >= 0, p, -1))
    lpos_ref[...] = jnp.concatenate(lpos, axis=0)
    ridx = lax.broadcasted_iota(i32, (LOCAL_ROWS, TM_DISP), 0).astype(f32)
    lf = [p.astype(f32) for p in lpos]
    onehot = ((ridx - lf[0]) * (ridx - lf[1])) * ((ridx - lf[2]) * (ridx - lf[3])) == 0.0
    srt = jnp.dot(jnp.where(onehot, 1.0, 0.0).astype(bf16), x, preferred_element_type=f32)
    half = D_MODEL // 2
    lo = lax.shift_right_logical(lax.bitcast_convert_type(srt[:, :half], i32), 16)
    hi = lax.bitcast_convert_type(srt[:, half:], i32) & jnp.int32(-65536)
    buf[slot] = hi | lo

    def copy_of(s):
        def make(loc, dst, sz):
            return pltpu.make_async_copy(buf.at[s, pl.ds(loc, sz), :], slots_hbm.at[pl.ds(dst, sz), :], sem.at[s])
        return make

    _segment_dmas(i, seg_rows, seg_local, seg_slot, copy_of(slot))

    @pl.when(i > 0)
    def _():
        _tile_wait(tile_rows[i - 1], copy_of(1 - slot))

    @pl.when(i == n_tiles - 1)
    def _():
        _tile_wait(tile_rows[i], copy_of(slot))
        zbuf[...] = jnp.zeros(zbuf.shape, i32)
        for start in (True, False):
            def body(e, carry):
                n = tail_rows[e]
                dst = tail_slot[e]
                for sz in TAIL_CHUNKS:
                    @pl.when((n & sz) != 0)
                    def _():
                        off = pl.multiple_of(n & ~(2 * sz - 1), ROW_ALIGN)
                        cp = pltpu.make_async_copy(zbuf.at[pl.ds(0, sz), :],
                                                   slots_hbm.at[pl.ds(pl.multiple_of(dst + off, ROW_ALIGN), sz), :], zsem)
                        if start:
                            cp.start()
                        else:
                            cp.wait()
                return carry
            lax.fori_loop(0, N_EXPERTS, body, 0)


def _dispatch(xf_p, xf_s, top_e, lbase_v, meta, n_slots):
    n_prompt_tiles = xf_p.shape[0] // TM_DISP
    n_tiles = n_prompt_tiles + 1
    half = D_MODEL // 2
    grid_spec = pltpu.PrefetchScalarGridSpec(
        num_scalar_prefetch=6,
        grid=(n_tiles,),
        in_specs=[pl.BlockSpec((TM_DISP, D_MODEL), lambda i, *_: (jnp.minimum(i, n_prompt_tiles - 1), 0)),
                  pl.BlockSpec((TM_DISP, D_MODEL), lambda i, *_: (0, 0)),
                  pl.BlockSpec((TOP_K, TM_DISP), lambda i, *_: (0, i)),
                  pl.BlockSpec((None, N_EXPERTS, 1), lambda i, *_: (i, 0, 0))],
        out_specs=[pl.BlockSpec((TOP_K, TM_DISP), lambda i, *_: (0, i)),
                   pl.BlockSpec(memory_space=pl.ANY)],
        scratch_shapes=[pltpu.VMEM((2, LOCAL_ROWS, half), i32), pltpu.VMEM((max(TAIL_CHUNKS), half), i32),
                        pltpu.SemaphoreType.DMA((2,)), pltpu.SemaphoreType.DMA(())],
    )
    return pl.pallas_call(
        functools.partial(_dispatch_kernel, n_prompt_tiles=n_prompt_tiles, n_tiles=n_tiles),
        grid_spec=grid_spec,
        out_shape=[jax.ShapeDtypeStruct((TOP_K, n_tiles * TM_DISP), i32),
                   jax.ShapeDtypeStruct((n_slots, half), i32)],
        compiler_params=_cparams(1),
        name="dispatch",
    )(*meta, xf_p, xf_s, top_e, lbase_v)


def _combine_kernel(seg_slot, seg_local, seg_rows, tile_rows,
                    lpos_ref, gate_ref, xa_ref, xb_ref, nfin_ref, y_hbm, ya_ref, yb_ref, buf, sem,
                    *, n_prompt_tiles, n_tiles):
    i = pl.program_id(0)
    slot = i % 2

    def copy_of(s):
        def make(loc, src, sz):
            return pltpu.make_async_copy(y_hbm.at[pl.ds(src, sz), :], buf.at[s, pl.ds(loc, sz), :], sem.at[s])
        return make

    @pl.when(i == 0)
    def _():
        buf[...] = jnp.zeros(buf.shape, f32)
        _segment_dmas(0, seg_rows, seg_local, seg_slot, copy_of(0))

    @pl.when(i + 1 < n_tiles)
    def _():
        _segment_dmas(i + 1, seg_rows, seg_local, seg_slot, copy_of(1 - slot))

    _tile_wait(tile_rows[i], copy_of(slot))

    lpos = lpos_ref[...]
    gate = gate_ref[...]
    cidx = lax.broadcasted_iota(i32, (TM_DISP, LOCAL_ROWS), 1)
    wmat = jnp.zeros((TM_DISP, LOCAL_ROWS), f32)
    for k in range(TOP_K):
        wmat = jnp.where(cidx == lpos[:, k:k + 1], gate[:, k:k + 1], wmat)
    moe = jnp.dot(wmat.astype(bf16), buf[slot].astype(bf16), preferred_element_type=f32)
    x1 = jnp.where(i < n_prompt_tiles, xa_ref[...], xb_ref[...])
    y = _rms(x1 + moe, nfin_ref[...])

    @pl.when(i < n_prompt_tiles)
    def _():
        ya_ref[...] = y

    @pl.when(i >= n_prompt_tiles)
    def _():
        yb_ref[...] = y


def _combine(y_slots, lpos_t, gate_t, x1_p, x1_s, norm_final, meta):
    n_prompt_tiles = x1_p.shape[0] // TM_DISP
    n_tiles = n_prompt_tiles + 1
    last = n_prompt_tiles - 1
    grid_spec = pltpu.PrefetchScalarGridSpec(
        num_scalar_prefetch=4,
        grid=(n_tiles,),
        in_specs=[pl.BlockSpec((TM_DISP, TOP_K), lambda i, *_: (i, 0)),
                  pl.BlockSpec((TM_DISP, TOP_K), lambda i, *_: (i, 0)),
                  pl.BlockSpec((TM_DISP, D_MODEL), lambda i, *_: (jnp.minimum(i, last), 0)),
                  pl.BlockSpec((TM_DISP, D_MODEL), lambda i, *_: (0, 0)),
                  pl.BlockSpec((1, D_MODEL), lambda i, *_: (0, 0)),
                  pl.BlockSpec(memory_space=pl.ANY)],
        out_specs=[pl.BlockSpec((TM_DISP, D_MODEL), lambda i, *_: (jnp.minimum(i, last), 0)),
                   pl.BlockSpec((TM_DISP, D_MODEL), lambda i, *_: (0, 0))],
        scratch_shapes=[pltpu.VMEM((2, LOCAL_ROWS, D_MODEL), f32), pltpu.SemaphoreType.DMA((2,))],
    )
    return pl.pallas_call(
        functools.partial(_combine_kernel, n_prompt_tiles=n_prompt_tiles, n_tiles=n_tiles),
        grid_spec=grid_spec,
        out_shape=[jax.ShapeDtypeStruct(x1_p.shape, f32), jax.ShapeDtypeStruct(x1_s.shape, f32)],
        compiler_params=_cparams(1),
        name="combine",
    )(*meta, lpos_t, gate_t, x1_p, x1_s, norm_final, y_slots)


ROW_DMA_PRIORITY = 1


def _experts_kernel(first_blk, n_blk, n_total, xs_hbm, wgu_ref, bgu_ref, wd_ref, bd_ref, y_hbm,
                    wgu_bf, wd_bf, xbuf, ybuf, xsem, ysem):
    e = pl.program_id(0)
    nb = n_blk[e]
    g0 = first_blk[e]
    total = n_total[0]

    def rows(g):
        return pl.ds(pl.multiple_of(g * R_BLK, R_BLK), R_BLK)

    def x_copy(g, slot):
        return pltpu.make_async_copy(xs_hbm.at[rows(g), :], xbuf.at[slot], xsem.at[slot])

    def y_copy(g, slot):
        return pltpu.make_async_copy(ybuf.at[slot], y_hbm.at[rows(g), :], ysem.at[slot])

    @pl.when(e == 0)
    def _():
        x_copy(0, 0).start(priority=ROW_DMA_PRIORITY)

    @pl.when(nb > 0)
    def _():
        wgu_bf[...] = wgu_ref[...].astype(bf16)
        wd_bf[...] = wd_ref[...].astype(bf16)

        def body(j, carry):
            blk = g0 + j
            slot = blk % 2

            @pl.when(blk + 1 < total)
            def _():
                x_copy(blk + 1, 1 - slot).start(priority=ROW_DMA_PRIORITY)

            x_copy(blk, slot).wait()

            @pl.when(blk >= 2)
            def _():
                y_copy(blk - 2, slot).wait()

            pk = xbuf[slot]
            lo = lax.bitcast_convert_type(lax.shift_left(pk, 16), f32).astype(bf16)
            hi = lax.bitcast_convert_type(pk & jnp.int32(-65536), f32).astype(bf16)
            x = jnp.concatenate([lo, hi], axis=1)
            gu = jnp.dot(x, wgu_bf[...], preferred_element_type=f32) + bgu_ref[...]
            g = jnp.minimum(gu[:, :D_FF], SWIGLU_LIMIT)
            up = jnp.clip(gu[:, D_FF:], -SWIGLU_LIMIT, SWIGLU_LIMIT)
            h = (up + 1.0) * (g * jax.nn.sigmoid(SWIGLU_ALPHA * g))
            ybuf[slot] = jnp.dot(h.astype(bf16), wd_bf[...], preferred_element_type=f32) + bd_ref[...]
            y_copy(blk, slot).start(priority=ROW_DMA_PRIORITY)
            return carry

        lax.fori_loop(0, nb, body, 0)

    @pl.when(e == N_EXPERTS - 1)
    def _():
        @pl.when(total >= 2)
        def _():
            y_copy(total - 2, total % 2).wait()

        y_copy(total - 1, (total - 1) % 2).wait()


def _experts(x_slots, w_gu, b_gu, w_down, b_down, first_blk, n_blk, n_total):
    half = D_MODEL // 2
    wsel = lambda e, *_: (e, 0, 0)
    grid_spec = pltpu.PrefetchScalarGridSpec(
        num_scalar_prefetch=3,
        grid=(N_EXPERTS,),
        in_specs=[pl.BlockSpec(memory_space=pl.ANY),
                  pl.BlockSpec((None, D_MODEL, 2 * D_FF), wsel),
                  pl.BlockSpec((None, 1, 2 * D_FF), wsel),
                  pl.BlockSpec((None, D_FF, D_MODEL), wsel),
                  pl.BlockSpec((None, 1, D_MODEL), wsel)],
        out_specs=pl.BlockSpec(memory_space=pl.ANY),
        scratch_shapes=[pltpu.VMEM((D_MODEL, 2 * D_FF), bf16), pltpu.VMEM((D_FF, D_MODEL), bf16),
                        pltpu.VMEM((2, R_BLK, half), i32), pltpu.VMEM((2, R_BLK, D_MODEL), f32),
                        pltpu.SemaphoreType.DMA((2,)), pltpu.SemaphoreType.DMA((2,))],
    )
    return pl.pallas_call(
        _experts_kernel,
        grid_spec=grid_spec,
        out_shape=jax.ShapeDtypeStruct((x_slots.shape[0], D_MODEL), f32),
        compiler_params=_cparams(1),
        name="experts",
    )(first_blk, n_blk, n_total, x_slots, w_gu, b_gu.reshape(N_EXPERTS, 1, 2 * D_FF), w_down,
      b_down.reshape(N_EXPERTS, 1, D_MODEL))


def _slot_layout(counts):
    seg_rows = (counts + ROW_ALIGN - 1) // ROW_ALIGN * ROW_ALIGN
    seg_local = jnp.cumsum(seg_rows, axis=1) - seg_rows
    total = jnp.sum(seg_rows, axis=0)
    region = (total + R_BLK - 1) // R_BLK * R_BLK
    region_end = jnp.cumsum(region)
    ebase = region_end - region
    seg_slot = ebase[None, :] + jnp.cumsum(seg_rows, axis=0) - seg_rows
    flat = lambda a: a.reshape(-1).astype(i32)
    return dict(seg_slot=flat(seg_slot), seg_local=flat(seg_local), seg_rows=flat(seg_rows),
                tile_rows=flat(jnp.sum(seg_rows, axis=1)),
                tail_slot=flat(ebase + total), tail_rows=flat(region - total),
                first_blk=flat(ebase // R_BLK), n_blk=flat(region // R_BLK),
                n_total=flat(region_end[-1:] // R_BLK),
                lbase_v=seg_local.astype(i32)[:, :, None])


def kernel(x_prompt, x_sample, cache_k, cache_v, cache_conv, norm_mix, w_in, attn_sinks, conv_dw_w, conv_dw_b,
           conv_ln_g, conv_ln_b, w_pw, b_pw, w_out, norm_ffn, w_router, b_router, w_gu, b_gu, w_down, b_down,
           norm_final):
    depth = norm_mix.shape[0]
    assert depth == 1, "one layer per step"
    batch, seq, _ = x_prompt.shape
    dbatch, dseq, _ = x_sample.shape
    window = cache_k.shape[2]
    n_p, n_s = batch * seq, dbatch * dseq
    assert seq % TM_PROJ == 0 and seq % (ATTN_CHUNKS * CHUNK) == 0 and seq >= SPAN and n_s <= TM_DISP
    assert n_p % TM_DISP == 0 and TM_DISP % TM_MIX == 0 and seq % TM_MIX == 0
    l = 0
    row = lambda a: a.reshape(1, -1)

    w_in_bf = w_in[l].astype(bf16)
    tabs_p = _rope_tables(jnp.arange(seq, dtype=f32))
    tabs_s = _rope_tables(PAST_LEN + jnp.arange(dseq, dtype=f32))
    tabs_s = tuple(jnp.tile(t, (dbatch, 1)) for t in tabs_s)
    xp2 = x_prompt.reshape(n_p, D_MODEL)
    xs2 = x_sample.reshape(n_s, D_MODEL)
    q_p, kv_p, u_p, ga_p, gc_p, ut_p = _in_proj(xp2, row(norm_mix[l]), w_in_bf, tabs_p, TM_PROJ, seq // TM_PROJ, bf16)
    q_s, kv_s, u_s, ga_s, gc_s, _ = _in_proj(xs2, row(norm_mix[l]), w_in_bf, tabs_s, n_s, 1, f32)

    sinks = attn_sinks[l].astype(f32).reshape(N_KV_HEADS, 1, GROUP)
    attn_p = _attn_prompt(q_p, kv_p, jnp.repeat(sinks, CHUNK, axis=2), batch, seq)
    qpos = PAST_LEN + np.arange(dseq)
    kpos = PAST_LEN - window + np.arange(window + dseq)
    qch, kch = qpos // CHUNK, kpos // CHUNK
    mask = (kch[None, :] >= qch[:, None] - WINDOW_CHUNKS) & (kch[None, :] <= qch[:, None]) & (kpos[None, :] >= 0)
    mask_rows = jnp.asarray(np.tile(mask.astype(np.float32).T, (1, GROUP)))
    ck = cache_k[l].reshape(dbatch, window, KV_W)
    cv = cache_v[l].reshape(dbatch, window, KV_W)
    attn_s = _attn_sample(q_s, kv_s, ck, cv, mask_rows, jnp.repeat(sinks, dseq, axis=2), dbatch, dseq)

    conv_w = jnp.broadcast_to(conv_dw_w[l][:, None, :], (CONV_WIDTH, SUBLANES, CONV_CH))
    mix_w = (conv_w, row(conv_dw_b[l]), row(conv_ln_g[l]),
             row(conv_ln_b[l]), w_pw[l].astype(bf16), row(b_pw[l]), w_out[l].astype(bf16), row(norm_ffn[l]),
             w_router[l].T.astype(bf16), b_router[l].astype(f32).reshape(N_EXPERTS, 1))
    x1_p, xf_p, te_p, gt_p, cnt_p = _mix_prompt(u_p, attn_p, ga_p, gc_p, xp2, mix_w, seq // TM_MIX)
    x1_s, xf_s, te_s, gt_s, cnt_s = _mix_sample(u_s, cache_conv[l], attn_s, ga_s, gc_s, xs2, mix_w, dbatch, dseq)

    n_tiles = n_p // TM_DISP + 1
    n_all = n_tiles * TM_DISP
    n_blocks = -(-(TOP_K * n_all + (ROW_ALIGN - 1) * N_EXPERTS * n_tiles + N_EXPERTS * (R_BLK - ROW_ALIGN)) // R_BLK)
    cnt_p = cnt_p.reshape(n_p // TM_DISP, TM_DISP // TM_MIX, N_EXPERTS).sum(axis=1)
    lay = _slot_layout(jnp.concatenate([cnt_p, cnt_s[:, :, 0]], axis=0))
    top_e = jnp.concatenate([te_p, te_s], axis=1)
    gates = jnp.concatenate([gt_p, gt_s], axis=1)
    lpos, x_slots = _dispatch(xf_p, xf_s, top_e, lay["lbase_v"],
                              (lay["seg_slot"], lay["seg_local"], lay["seg_rows"], lay["tile_rows"],
                               lay["tail_slot"], lay["tail_rows"]),
                              n_blocks * R_BLK)
    y_slots = _experts(x_slots, w_gu[l], b_gu[l], w_down[l], b_down[l], lay["first_blk"], lay["n_blk"],
                       lay["n_total"])
    y_p, y_s = _combine(y_slots, lpos.T, gates.T, x1_p, x1_s, row(norm_final),
                        (lay["seg_slot"], lay["seg_local"], lay["seg_rows"], lay["tile_rows"]))

    y_prompt = y_p.reshape(batch, seq, D_MODEL)
    y_sample = y_s[:n_s].reshape(dbatch, dseq, D_MODEL)
    kv_tail = kv_p.reshape(batch, seq, 2 * KV_W)[:, seq - window:]
    new_k_p = kv_tail[:, :, :KV_W].reshape(1, batch, window, N_KV_HEADS, HEAD_DIM)
    new_v_p = kv_tail[:, :, KV_W:].reshape(1, batch, window, N_KV_HEADS, HEAD_DIM)
    tiles_per_seq = seq // TM_PROJ
    new_conv_p = ut_p.reshape(batch, tiles_per_seq, HALO, CONV_CH)[:, -1, HALO - (CONV_WIDTH - 1):][None]
    kv_s4 = kv_s.reshape(dbatch, dseq, 2, N_KV_HEADS, HEAD_DIM)
    new_k_s = jnp.concatenate([cache_k[l], kv_s4[:, :, 0]], axis=1)[:, -window:][None]
    new_v_s = jnp.concatenate([cache_v[l], kv_s4[:, :, 1]], axis=1)[:, -window:][None]
    new_conv_s = jnp.concatenate([cache_conv[l], u_s.reshape(dbatch, dseq, CONV_CH)], axis=1)[:, -(CONV_WIDTH - 1):][None]
    return (y_prompt, y_sample, new_k_p, new_v_p, new_conv_p, new_k_s, new_v_s, new_conv_s)
```

```python
import functools

import numpy as np
import jax
import jax.numpy as jnp
from jax import lax
from jax.experimental import pallas as pl
from jax.experimental.pallas import tpu as pltpu

f32 = jnp.float32
bf16 = jnp.bfloat16
i32 = jnp.int32

D_MODEL = 1024
PAST_LEN = 1024
CHUNK = 64
N_HEADS = 16
N_KV_HEADS = 2
HEAD_DIM = 64
GROUP = N_HEADS // N_KV_HEADS
ROT_DIM = HEAD_DIM // 4
ROPE_THETA = 500000.0
WINDOW = 128
WINDOW_CHUNKS = WINDOW // CHUNK
SPAN = (WINDOW_CHUNKS + 1) * CHUNK
CONV_CH = D_MODEL
CONV_WIDTH = 31
N_EXPERTS = 32
TOP_K = 4
D_FF = D_MODEL
SWIGLU_LIMIT = 7.0
SWIGLU_ALPHA = 1.702
NORM_EPS = 1e-5
NEG_INF = -1e30
Q_W = N_HEADS * HEAD_DIM
KV_W = N_KV_HEADS * HEAD_DIM
IN_COLS = Q_W + 2 * KV_W + 2 * CONV_CH + 2 * D_MODEL
Q_SCALE = HEAD_DIM ** -0.5

LANES = 128
SUBLANES = 8
VMEM_LIMIT = 56 * 1024 * 1024

HALO = 32
TM_PROJ = 1024
TM_MIX = 512
TM_DISP = 512
ATTN_CHUNKS = 16
TOKEN_COLS = SUBLANES
ROW_ALIGN = SUBLANES
R_BLK = 512
LOCAL_ROWS = -(-(TOP_K * TM_DISP + N_EXPERTS * (ROW_ALIGN - 1)) // LANES) * LANES
SEG_CHUNKS = tuple(2 ** p for p in range(int(np.log2(TM_DISP)), int(np.log2(ROW_ALIGN)) - 1, -1))
WAIT_CHUNKS = tuple(2 ** p for p in range(int(np.log2(LOCAL_ROWS)), int(np.log2(ROW_ALIGN)) - 1, -1))
TAIL_CHUNKS = tuple(c for c in SEG_CHUNKS if c < R_BLK)


def _cparams(n_axes):
    return pltpu.CompilerParams(dimension_semantics=("arbitrary",) * n_axes,
                                vmem_limit_bytes=VMEM_LIMIT)


def _rms(x, g):
    return x * lax.rsqrt(jnp.mean(x * x, axis=-1, keepdims=True) + NORM_EPS) * g


def _tokens_to_rows(a, fill):
    pad = jnp.full((LANES - a.shape[0], a.shape[1]), fill, a.dtype)
    return jnp.transpose(jnp.concatenate([a, pad], axis=0))[:, :TOKEN_COLS]


def _inproj_kernel(x_ref, g_ref, w_ref, cos_ref, sa_ref, sb_ref,
                   q_ref, kv_ref, u_ref, ga_ref, gc_ref, ut_ref):
    xn = _rms(x_ref[...], g_ref[...]).astype(bf16)
    cos = cos_ref[...]
    sa = sa_ref[...]
    sb = sb_ref[...]

    def rope(t):
        return t * cos + pltpu.roll(t, LANES - ROT_DIM // 2, 1) * sa + pltpu.roll(t, ROT_DIM // 2, 1) * sb

    def proj(c0, n):
        return jnp.dot(xn, w_ref[:, c0:c0 + n], preferred_element_type=f32)

    q = proj(0, Q_W)
    for c in range(Q_W // LANES):
        q_ref[:, c * LANES:(c + 1) * LANES] = (rope(q[:, c * LANES:(c + 1) * LANES]) * Q_SCALE).astype(bf16)
    kv = proj(Q_W, 2 * KV_W)
    kv_ref[:, :KV_W] = rope(kv[:, :KV_W])
    kv_ref[:, KV_W:] = kv[:, KV_W:]
    c0 = Q_W + 2 * KV_W
    u = proj(c0, CONV_CH) * jax.nn.sigmoid(proj(c0 + CONV_CH, CONV_CH))
    u_ref[...] = u.astype(u_ref.dtype)
    ut_ref[...] = u[u.shape[0] - HALO:, :]
    c0 += 2 * CONV_CH
    ga_ref[...] = jax.nn.sigmoid(proj(c0, D_MODEL)).astype(ga_ref.dtype)
    gc_ref[...] = jax.nn.sigmoid(proj(c0 + D_MODEL, D_MODEL)).astype(gc_ref.dtype)


def _rope_tables(pos):
    half = ROT_DIM // 2
    inv = ROPE_THETA ** (-jnp.arange(half, dtype=f32) * 2.0 / ROT_DIM)
    ang = pos[:, None] * inv[None, :]
    cos, sin = jnp.cos(ang), jnp.sin(ang)
    t = pos.shape[0]
    ones = jnp.ones((t, HEAD_DIM - ROT_DIM), f32)
    zeros = jnp.zeros((t, HEAD_DIM - ROT_DIM), f32)
    zh = jnp.zeros((t, half), f32)
    c = jnp.concatenate([cos, cos, ones], axis=1)
    a = jnp.concatenate([-sin, zh, zeros], axis=1)
    b = jnp.concatenate([zh, sin, zeros], axis=1)
    rep = LANES // HEAD_DIM
    return jnp.tile(c, (1, rep)), jnp.tile(a, (1, rep)), jnp.tile(b, (1, rep))


def _in_proj(x2d, norm_g, w_bf, tables, tm, tiles_per_seq, act_dtype):
    n = x2d.shape[0]
    nt = n // tm
    row = lambda i: (i, 0)
    const = lambda i: (0, 0)
    tab = lambda i: (i % tiles_per_seq, 0)
    return pl.pallas_call(
        _inproj_kernel,
        grid=(nt,),
        in_specs=[pl.BlockSpec((tm, D_MODEL), row),
                  pl.BlockSpec((1, D_MODEL), const),
                  pl.BlockSpec((D_MODEL, IN_COLS), const, pipeline_mode=pl.Buffered(1)),
                  pl.BlockSpec((tm, LANES), tab),
                  pl.BlockSpec((tm, LANES), tab),
                  pl.BlockSpec((tm, LANES), tab)],
        out_specs=[pl.BlockSpec((tm, Q_W), row),
                   pl.BlockSpec((tm, 2 * KV_W), row),
                   pl.BlockSpec((tm, CONV_CH), row),
                   pl.BlockSpec((tm, D_MODEL), row),
                   pl.BlockSpec((tm, D_MODEL), row),
                   pl.BlockSpec((None, HALO, CONV_CH), lambda i: (i, 0, 0))],
        out_shape=[jax.ShapeDtypeStruct((n, Q_W), bf16),
                   jax.ShapeDtypeStruct((n, 2 * KV_W), f32),
                   jax.ShapeDtypeStruct((n, CONV_CH), act_dtype),
                   jax.ShapeDtypeStruct((n, D_MODEL), act_dtype),
                   jax.ShapeDtypeStruct((n, D_MODEL), act_dtype),
                   jax.ShapeDtypeStruct((nt, HALO, CONV_CH), f32)],
        compiler_params=_cparams(1),
        name="in_proj",
    )(x2d, norm_g, w_bf, *tables)


def _attn_group(qg, kg, vg, mask_t, sink):
    s = lax.dot_general(kg, qg, (((1,), (1,)), ((), ())), preferred_element_type=f32)
    if mask_t is not None:
        s = jnp.where(mask_t, s, NEG_INF)
    m = jnp.maximum(jnp.max(s, axis=0, keepdims=True), sink)
    p = jnp.exp(s - m)
    denom = jnp.sum(p, axis=0, keepdims=True) + jnp.exp(sink - m)
    pn = (p / denom).astype(bf16)
    return lax.dot_general(pn, vg, (((0,), (0,)), ((), ())), preferred_element_type=f32)


def _heads_to_rows(q, g):
    base = g * GROUP * HEAD_DIM
    return jnp.concatenate([q[:, base + h * HEAD_DIM: base + (h + 1) * HEAD_DIM] for h in range(GROUP)], axis=0)


def _rows_to_heads(o, tq):
    return [o[h * tq:(h + 1) * tq, :] for h in range(GROUP)]


def _attn_prompt_kernel(q_ref, kv_ref, sink_ref, o_ref):
    j = pl.program_id(1)
    for c in range(ATTN_CHUNKS):
        n = j * ATTN_CHUNKS + c
        first = jnp.maximum(n - WINDOW_CHUNKS, 0)
        win = kv_ref[pl.ds(pl.multiple_of(first * CHUNK, CHUNK), SPAN), :]
        if c < WINDOW_CHUNKS:
            mask = lax.broadcasted_iota(i32, (SPAN, GROUP * CHUNK), 0) < (n - first + 1) * CHUNK
        else:
            mask = None
        q = q_ref[c * CHUNK:(c + 1) * CHUNK, :]
        outs = []
        for g in range(N_KV_HEADS):
            kg = win[:, g * HEAD_DIM:(g + 1) * HEAD_DIM].astype(bf16)
            vg = win[:, KV_W + g * HEAD_DIM: KV_W + (g + 1) * HEAD_DIM].astype(bf16)
            o = _attn_group(_heads_to_rows(q, g), kg, vg, mask, sink_ref[g])
            outs += _rows_to_heads(o, CHUNK)
        o_ref[c * CHUNK:(c + 1) * CHUNK, :] = jnp.concatenate(outs, axis=1).astype(bf16)


def _attn_prompt(q, kv, sink_rows, batch, seq):
    qrows = ATTN_CHUNKS * CHUNK
    steps = seq // qrows
    return pl.pallas_call(
        _attn_prompt_kernel,
        grid=(batch, steps),
        in_specs=[pl.BlockSpec((qrows, Q_W), lambda b, j: (b * steps + j, 0)),
                  pl.BlockSpec((seq, 2 * KV_W), lambda b, j: (b, 0)),
                  pl.BlockSpec((N_KV_HEADS, 1, GROUP * CHUNK), lambda b, j: (0, 0, 0))],
        out_specs=pl.BlockSpec((qrows, Q_W), lambda b, j: (b * steps + j, 0)),
        out_shape=jax.ShapeDtypeStruct((batch * seq, Q_W), bf16),
        compiler_params=_cparams(2),
        name="attn_prompt",
    )(q, kv, sink_rows)


def _attn_sample_kernel(q_ref, kvn_ref, ck_ref, cv_ref, mask_ref, sink_ref, o_ref):
    tq = q_ref.shape[0]
    q = q_ref[...]
    kvn = kvn_ref[...]
    ck = ck_ref[...]
    cv = cv_ref[...]
    mask = mask_ref[...] > 0.5
    outs = []
    for g in range(N_KV_HEADS):
        sl = slice(g * HEAD_DIM, (g + 1) * HEAD_DIM)
        kg = jnp.concatenate([ck[:, sl], kvn[:, sl]], axis=0).astype(bf16)
        vg = jnp.concatenate([cv[:, sl], kvn[:, KV_W + g * HEAD_DIM: KV_W + (g + 1) * HEAD_DIM]], axis=0).astype(bf16)
        o = _attn_group(_heads_to_rows(q, g), kg, vg, mask, sink_ref[g])
        outs += _rows_to_heads(o, tq)
    o_ref[...] = jnp.concatenate(outs, axis=1)


def _attn_sample(q, kv_new, cache_k, cache_v, mask_rows, sink_rows, batch, tq):
    w = cache_k.shape[1]
    return pl.pallas_call(
        _attn_sample_kernel,
        grid=(batch,),
        in_specs=[pl.BlockSpec((tq, Q_W), lambda b: (b, 0)),
                  pl.BlockSpec((tq, 2 * KV_W), lambda b: (b, 0)),
                  pl.BlockSpec((None, w, KV_W), lambda b: (b, 0, 0)),
                  pl.BlockSpec((None, w, KV_W), lambda b: (b, 0, 0)),
                  pl.BlockSpec((w + tq, GROUP * tq), lambda b: (0, 0)),
                  pl.BlockSpec((N_KV_HEADS, 1, GROUP * tq), lambda b: (0, 0, 0))],
        out_specs=pl.BlockSpec((tq, Q_W), lambda b: (b, 0)),
        out_shape=jax.ShapeDtypeStruct((batch * tq, Q_W), f32),
        compiler_params=_cparams(1),
        name="attn_sample",
    )(q, kv_new, cache_k, cache_v, mask_rows, sink_rows)


def _dwconv(s_ref, row0, n_rows, w_ref, b_ref, y_ref, yrow0, rc):
    lead = HALO - (CONV_WIDTH - 1)
    offsets = range(lead, lead + CONV_WIDTH)

    def lane_body(lc, carry):
        c0 = pl.multiple_of(lc * LANES, LANES)
        cols = pl.ds(c0, LANES)
        bias = b_ref[:, cols]
        groups = rc // SUBLANES
        for t0 in range(0, n_rows, rc):
            acc = jnp.broadcast_to(bias[None], (groups, SUBLANES, LANES))
            for r in range(SUBLANES):
                offs = [o for o in offsets if o % SUBLANES == r]
                if not offs:
                    continue
                amax = max(offs) // SUBLANES
                z = s_ref[pl.ds(row0 + t0 + r, rc + SUBLANES * amax), cols].reshape(groups + amax, SUBLANES, LANES)
                for o in offs:
                    a = o // SUBLANES
                    acc = acc + z[a: a + groups] * w_ref[o - lead, :, cols][None]
            y_ref[pl.ds(yrow0 + t0, rc), cols] = acc.reshape(rc, LANES)
        return carry

    lax.fori_loop(0, CONV_CH // LANES, lane_body, 0)


def _mix_tail(y, attn, ga, gc, x, valid, lg_ref, lb_ref, wpw_ref, bpw_ref, wout_ref, nf_ref, wr_ref, br_ref):
    tm = y.shape[0]
    mu = jnp.mean(y, axis=-1, keepdims=True)
    d = y - mu
    var = jnp.mean(d * d, axis=-1, keepdims=True)
    yn = d * lax.rsqrt(var + NORM_EPS) * lg_ref[...] + lb_ref[...]
    act = (yn * jax.nn.sigmoid(yn)).astype(bf16)
    conv_out = jnp.dot(act, wpw_ref[...], preferred_element_type=f32) + bpw_ref[...]
    h = (ga.astype(f32) * attn.astype(f32) + gc.astype(f32) * conv_out).astype(bf16)
    x1 = x + jnp.dot(h, wout_ref[...], preferred_element_type=f32)
    xf = _rms(x1, nf_ref[...]).astype(bf16)
    logits = lax.dot_general(wr_ref[...], xf, (((1,), (1,)), ((), ())), preferred_element_type=f32) + br_ref[...]
    eidx = lax.broadcasted_iota(i32, (N_EXPERTS, tm), 0)
    routed = jnp.zeros((N_EXPERTS, tm), jnp.bool_)
    top_l, top_e = [], []
    l = logits
    for _ in range(TOP_K):
        m = jnp.max(l, axis=0, keepdims=True)
        idx = jnp.min(jnp.where(l == m, eidx, N_EXPERTS), axis=0, keepdims=True)
        sel = eidx == idx
        routed = routed | sel
        l = jnp.where(sel, -jnp.inf, l)
        top_l.append(m)
        top_e.append(idx)
    ex = [jnp.exp(t - top_l[0]) for t in top_l]
    tot = ex[0] + ex[1] + ex[2] + ex[3]
    gates = jnp.concatenate([e / tot for e in ex], axis=0)
    top_e = jnp.concatenate(top_e, axis=0)
    if valid is not None:
        routed = routed & valid
        top_e = jnp.where(valid, top_e, -1)
        gates = jnp.where(valid, gates, 0.0)
    counts = jnp.sum(jnp.where(routed, 1, 0).astype(i32), axis=1, keepdims=True)
    return x1, xf, top_e, _tokens_to_rows(gates, 0.0), counts


def _mix_prompt_kernel(u_ref, halo_ref, attn_ref, ga_ref, gc_ref, x_ref, cw_ref, cb_ref, lg_ref, lb_ref,
                       wpw_ref, bpw_ref, wout_ref, nf_ref, wr_ref, br_ref,
                       x1_ref, xf_ref, te_ref, gt_ref, cnt_ref, s_scr, y_scr, *, tiles_per_seq):
    i = pl.program_id(0)
    first = (i % tiles_per_seq) == 0
    s_scr[0:HALO, :] = jnp.where(first, 0.0, halo_ref[...].astype(f32))
    s_scr[HALO:, :] = u_ref[...].astype(f32)
    _dwconv(s_scr, 0, TM_MIX, cw_ref, cb_ref, y_scr, 0, 128)
    x1, xf, top_e, gates, counts = _mix_tail(
        y_scr[...], attn_ref[...], ga_ref[...], gc_ref[...], x_ref[...], None,
        lg_ref, lb_ref, wpw_ref, bpw_ref, wout_ref, nf_ref, wr_ref, br_ref)
    x1_ref[...] = x1
    xf_ref[...] = xf
    te_ref[...] = top_e
    gt_ref[...] = gates
    cnt_ref[...] = counts


def _mix_sample_kernel(u_ref, hist_ref, attn_ref, ga_ref, gc_ref, x_ref, cw_ref, cb_ref, lg_ref, lb_ref,
                       wpw_ref, bpw_ref, wout_ref, nf_ref, wr_ref, br_ref,
                       x1_ref, xf_ref, te_ref, gt_ref, cnt_ref, s_scr, y_scr, *, batch, tq):
    n = batch * tq
    hist_rows = CONV_WIDTH - 1
    lead = HALO - hist_rows
    stride = HALO + tq
    s_scr[...] = jnp.zeros(s_scr.shape, f32)
    as_conv_input = lambda a: a.astype(bf16).astype(f32)
    for b in range(batch):
        s_scr[b * stride + lead: b * stride + HALO, :] = as_conv_input(hist_ref[b])
        s_scr[b * stride + HALO: (b + 1) * stride, :] = as_conv_input(u_ref[b * tq:(b + 1) * tq, :])
        _dwconv(s_scr, b * stride, tq, cw_ref, cb_ref, y_scr, b * tq, tq)
    x1, xf, top_e, gates, counts = _mix_tail(
        y_scr[...], attn_ref[...], ga_ref[...], gc_ref[...], x_ref[...], None,
        lg_ref, lb_ref, wpw_ref, bpw_ref, wout_ref, nf_ref, wr_ref, br_ref)
    x1_ref[...] = jnp.zeros(x1_ref.shape, f32)
    xf_ref[...] = jnp.zeros(xf_ref.shape, bf16)
    te_ref[...] = jnp.full(te_ref.shape, -1, i32)
    gt_ref[...] = jnp.zeros(gt_ref.shape, f32)
    x1_ref[0:n, :] = x1
    xf_ref[0:n, :] = xf
    te_ref[:, 0:n] = top_e
    gt_ref[0:n, :] = gates
    cnt_ref[0] = counts


def _mix_weight_specs():
    shapes = [(CONV_WIDTH, SUBLANES, CONV_CH), (1, CONV_CH), (1, CONV_CH), (1, CONV_CH), (CONV_CH, D_MODEL),
              (1, D_MODEL), (D_MODEL, D_MODEL), (1, D_MODEL), (N_EXPERTS, D_MODEL), (N_EXPERTS, 1)]
    return [pl.BlockSpec(s, functools.partial(lambda nd, i: (0,) * nd, len(s))) for s in shapes]


def _mix_out(n, nt):
    shapes = [jax.ShapeDtypeStruct((n, D_MODEL), f32), jax.ShapeDtypeStruct((n, D_MODEL), bf16),
              jax.ShapeDtypeStruct((TOP_K, n), i32), jax.ShapeDtypeStruct((n, TOKEN_COLS), f32),
              jax.ShapeDtypeStruct((nt, N_EXPERTS, 1), i32)]
    return shapes


def _mix_prompt(u, attn, ga, gc, x2d, weights, tiles_per_seq):
    n = x2d.shape[0]
    nt = n // TM_MIX
    row = lambda i: (i, 0)
    halo = lambda i: (jnp.maximum(i * (TM_MIX // HALO) - 1, 0), 0)
    tok = lambda i: (0, i)
    return pl.pallas_call(
        functools.partial(_mix_prompt_kernel, tiles_per_seq=tiles_per_seq),
        grid=(nt,),
        in_specs=[pl.BlockSpec((TM_MIX, CONV_CH), row), pl.BlockSpec((HALO, CONV_CH), halo),
                  pl.BlockSpec((TM_MIX, Q_W), row), pl.BlockSpec((TM_MIX, D_MODEL), row),
                  pl.BlockSpec((TM_MIX, D_MODEL), row), pl.BlockSpec((TM_MIX, D_MODEL), row)] + _mix_weight_specs(),
        out_specs=[pl.BlockSpec((TM_MIX, D_MODEL), row), pl.BlockSpec((TM_MIX, D_MODEL), row),
                   pl.BlockSpec((TOP_K, TM_MIX), tok), pl.BlockSpec((TM_MIX, TOKEN_COLS), row),
                   pl.BlockSpec((None, N_EXPERTS, 1), lambda i: (i, 0, 0))],
        out_shape=_mix_out(n, nt),
        scratch_shapes=[pltpu.VMEM((TM_MIX + HALO, CONV_CH), f32), pltpu.VMEM((TM_MIX, CONV_CH), f32)],
        compiler_params=_cparams(1),
        name="mix_prompt",
    )(u, u, attn, ga, gc, x2d, *weights)


def _mix_sample(u, hist, attn, ga, gc, x2d, weights, batch, tq):
    return pl.pallas_call(
        functools.partial(_mix_sample_kernel, batch=batch, tq=tq),
        out_shape=_mix_out(TM_DISP, 1),
        scratch_shapes=[pltpu.VMEM((batch * (HALO + tq), CONV_CH), f32), pltpu.VMEM((batch * tq, CONV_CH), f32)],
        compiler_params=pltpu.CompilerParams(vmem_limit_bytes=VMEM_LIMIT),
        name="mix_sample",
    )(u, hist, attn, ga, gc, x2d, *weights)


def _segment_dmas(tile, seg_rows, seg_local, seg_slot, make_copy):
    def body(e, carry):
        idx = tile * N_EXPERTS + e
        n = seg_rows[idx]
        loc = seg_local[idx]
        dst = seg_slot[idx]
        for sz in SEG_CHUNKS:
            @pl.when((n & sz) != 0)
            def _():
                off = pl.multiple_of(n & ~(2 * sz - 1), ROW_ALIGN)
                make_copy(pl.multiple_of(loc + off, ROW_ALIGN), pl.multiple_of(dst + off, ROW_ALIGN), sz).start()
        return carry

    lax.fori_loop(0, N_EXPERTS, body, 0)


def _tile_wait(n_rows, make_copy):
    for sz in WAIT_CHUNKS:
        @pl.when((n_rows & sz) != 0)
        def _():
            make_copy(0, 0, sz).wait()


def _dispatch_kernel(seg_slot, seg_local, seg_rows, tile_rows, tail_slot, tail_rows,
                     xa_ref, xb_ref, te_ref, lb_ref, lpos_ref, slots_hbm, buf, zbuf, sem, zsem,
                     *, n_prompt_tiles, n_tiles):
    i = pl.program_id(0)
    slot = i % 2
    x = jnp.where(i < n_prompt_tiles, xa_ref[...], xb_ref[...])
    te = te_ref[...]
    eidx = lax.broadcasted_iota(i32, (N_EXPERTS, TM_DISP), 0)
    hits = [te[k:k + 1, :] == eidx for k in range(TOP_K)]
    routed = hits[0] | hits[1] | hits[2] | hits[3]
    before = lax.broadcasted_iota(i32, (TM_DISP, TM_DISP), 0) < lax.broadcasted_iota(i32, (TM_DISP, TM_DISP), 1)
    rank = jnp.dot(jnp.where(routed, 1.0, 0.0).astype(bf16), jnp.where(before, 1.0, 0.0).astype(bf16),
                   preferred_element_type=f32)
    pos = lb_ref[...] + rank.astype(i32)
    lpos = []
    for k in range(TOP_K):
        p = jnp.sum(jnp.where(hits[k], pos, 0), axis=0, keepdims=True)
        lpos.append(jnp.where(te[k:k + 1, :] >= 0, p, -1))
    lpos_ref[...] = _tokens_to_rows(jnp.concatenate(lpos, axis=0), -1)
    ridx = lax.broadcasted_iota(i32, (LOCAL_ROWS, TM_DISP), 0).astype(f32)
    lf = [p.astype(f32) for p in lpos]
    onehot = ((ridx - lf[0]) * (ridx - lf[1])) * ((ridx - lf[2]) * (ridx - lf[3])) == 0.0
    srt = jnp.dot(jnp.where(onehot, 1.0, 0.0).astype(bf16), x, preferred_element_type=f32)
    half = D_MODEL // 2
    lo = lax.shift_right_logical(lax.bitcast_convert_type(srt[:, :half], i32), 16)
    hi = lax.bitcast_convert_type(srt[:, half:], i32) & jnp.int32(-65536)
    buf[slot] = hi | lo

    def copy_of(s):
        def make(loc, dst, sz):
            return pltpu.make_async_copy(buf.at[s, pl.ds(loc, sz), :], slots_hbm.at[pl.ds(dst, sz), :], sem.at[s])
        return make

    _segment_dmas(i, seg_rows, seg_local, seg_slot, copy_of(slot))

    @pl.when(i > 0)
    def _():
        _tile_wait(tile_rows[i - 1], copy_of(1 - slot))

    @pl.when(i == n_tiles - 1)
    def _():
        _tile_wait(tile_rows[i], copy_of(slot))
        zbuf[...] = jnp.zeros(zbuf.shape, i32)
        for start in (True, False):
            def body(e, carry):
                n = tail_rows[e]
                dst = tail_slot[e]
                for sz in TAIL_CHUNKS:
                    @pl.when((n & sz) != 0)
                    def _():
                        off = pl.multiple_of(n & ~(2 * sz - 1), ROW_ALIGN)
                        cp = pltpu.make_async_copy(zbuf.at[pl.ds(0, sz), :],
                                                   slots_hbm.at[pl.ds(pl.multiple_of(dst + off, ROW_ALIGN), sz), :], zsem)
                        if start:
                            cp.start()
                        else:
                            cp.wait()
                return carry
            lax.fori_loop(0, N_EXPERTS, body, 0)


def _dispatch(xf_p, xf_s, top_e, lbase_v, meta, n_slots):
    n_prompt_tiles = xf_p.shape[0] // TM_DISP
    n_tiles = n_prompt_tiles + 1
    half = D_MODEL // 2
    grid_spec = pltpu.PrefetchScalarGridSpec(
        num_scalar_prefetch=6,
        grid=(n_tiles,),
        in_specs=[pl.BlockSpec((TM_DISP, D_MODEL), lambda i, *_: (jnp.minimum(i, n_prompt_tiles - 1), 0)),
                  pl.BlockSpec((TM_DISP, D_MODEL), lambda i, *_: (0, 0)),
                  pl.BlockSpec((TOP_K, TM_DISP), lambda i, *_: (0, i)),
                  pl.BlockSpec((None, N_EXPERTS, 1), lambda i, *_: (i, 0, 0))],
        out_specs=[pl.BlockSpec((TM_DISP, TOKEN_COLS), lambda i, *_: (i, 0)),
                   pl.BlockSpec(memory_space=pl.ANY)],
        scratch_shapes=[pltpu.VMEM((2, LOCAL_ROWS, half), i32), pltpu.VMEM((max(TAIL_CHUNKS), half), i32),
                        pltpu.SemaphoreType.DMA((2,)), pltpu.SemaphoreType.DMA(())],
    )
    return pl.pallas_call(
        functools.partial(_dispatch_kernel, n_prompt_tiles=n_prompt_tiles, n_tiles=n_tiles),
        grid_spec=grid_spec,
        out_shape=[jax.ShapeDtypeStruct((n_tiles * TM_DISP, TOKEN_COLS), i32),
                   jax.ShapeDtypeStruct((n_slots, half), i32)],
        compiler_params=_cparams(1),
        name="dispatch",
    )(*meta, xf_p, xf_s, top_e, lbase_v)


def _combine_kernel(seg_slot, seg_local, seg_rows, tile_rows,
                    lpos_ref, gate_ref, xa_ref, xb_ref, nfin_ref, y_hbm, ya_ref, yb_ref, buf, sem,
                    *, n_prompt_tiles, n_tiles):
    i = pl.program_id(0)
    slot = i % 2

    def copy_of(s):
        def make(loc, src, sz):
            return pltpu.make_async_copy(y_hbm.at[pl.ds(src, sz), :], buf.at[s, pl.ds(loc, sz), :], sem.at[s])
        return make

    @pl.when(i == 0)
    def _():
        buf[...] = jnp.zeros(buf.shape, f32)
        _segment_dmas(0, seg_rows, seg_local, seg_slot, copy_of(0))

    @pl.when(i + 1 < n_tiles)
    def _():
        _segment_dmas(i + 1, seg_rows, seg_local, seg_slot, copy_of(1 - slot))

    _tile_wait(tile_rows[i], copy_of(slot))

    lpos = lpos_ref[...]
    gate = gate_ref[...]
    cidx = lax.broadcasted_iota(i32, (TM_DISP, LOCAL_ROWS), 1)
    wmat = jnp.zeros((TM_DISP, LOCAL_ROWS), f32)
    for k in range(TOP_K):
        wmat = jnp.where(cidx == lpos[:, k:k + 1], gate[:, k:k + 1], wmat)
    moe = jnp.dot(wmat.astype(bf16), buf[slot].astype(bf16), preferred_element_type=f32)
    x1 = jnp.where(i < n_prompt_tiles, xa_ref[...], xb_ref[...])
    y = _rms(x1 + moe, nfin_ref[...])

    @pl.when(i < n_prompt_tiles)
    def _():
        ya_ref[...] = y

    @pl.when(i >= n_prompt_tiles)
    def _():
        yb_ref[...] = y


def _combine(y_slots, lpos_t, gate_t, x1_p, x1_s, norm_final, meta):
    n_prompt_tiles = x1_p.shape[0] // TM_DISP
    n_tiles = n_prompt_tiles + 1
    last = n_prompt_tiles - 1
    grid_spec = pltpu.PrefetchScalarGridSpec(
        num_scalar_prefetch=4,
        grid=(n_tiles,),
        in_specs=[pl.BlockSpec((TM_DISP, TOKEN_COLS), lambda i, *_: (i, 0)),
                  pl.BlockSpec((TM_DISP, TOKEN_COLS), lambda i, *_: (i, 0)),
                  pl.BlockSpec((TM_DISP, D_MODEL), lambda i, *_: (jnp.minimum(i, last), 0)),
                  pl.BlockSpec((TM_DISP, D_MODEL), lambda i, *_: (0, 0)),
                  pl.BlockSpec((1, D_MODEL), lambda i, *_: (0, 0)),
                  pl.BlockSpec(memory_space=pl.ANY)],
        out_specs=[pl.BlockSpec((TM_DISP, D_MODEL), lambda i, *_: (jnp.minimum(i, last), 0)),
                   pl.BlockSpec((TM_DISP, D_MODEL), lambda i, *_: (0, 0))],
        scratch_shapes=[pltpu.VMEM((2, LOCAL_ROWS, D_MODEL), f32), pltpu.SemaphoreType.DMA((2,))],
    )
    return pl.pallas_call(
        functools.partial(_combine_kernel, n_prompt_tiles=n_prompt_tiles, n_tiles=n_tiles),
        grid_spec=grid_spec,
        out_shape=[jax.ShapeDtypeStruct(x1_p.shape, f32), jax.ShapeDtypeStruct(x1_s.shape, f32)],
        compiler_params=_cparams(1),
        name="combine",
    )(*meta, lpos_t, gate_t, x1_p, x1_s, norm_final, y_slots)


ROW_DMA_PRIORITY = 1


def _experts_kernel(first_blk, n_blk, n_total, xs_hbm, wgu_ref, bgu_ref, wd_ref, bd_ref, y_hbm,
                    wgu_bf, wd_bf, xbuf, ybuf, xsem, ysem):
    e = pl.program_id(0)
    nb = n_blk[e]
    g0 = first_blk[e]
    total = n_total[0]

    def rows(g):
        return pl.ds(pl.multiple_of(g * R_BLK, R_BLK), R_BLK)

    def x_copy(g, slot):
        return pltpu.make_async_copy(xs_hbm.at[rows(g), :], xbuf.at[slot], xsem.at[slot])

    def y_copy(g, slot):
        return pltpu.make_async_copy(ybuf.at[slot], y_hbm.at[rows(g), :], ysem.at[slot])

    @pl.when(e == 0)
    def _():
        x_copy(0, 0).start(priority=ROW_DMA_PRIORITY)

    @pl.when(nb > 0)
    def _():
        wgu_bf[...] = wgu_ref[...].astype(bf16)
        wd_bf[...] = wd_ref[...].astype(bf16)

        def body(j, carry):
            blk = g0 + j
            slot = blk % 2

            @pl.when(blk + 1 < total)
            def _():
                x_copy(blk + 1, 1 - slot).start(priority=ROW_DMA_PRIORITY)

            x_copy(blk, slot).wait()

            @pl.when(blk >= 2)
            def _():
                y_copy(blk - 2, slot).wait()

            pk = xbuf[slot]
            lo = lax.bitcast_convert_type(lax.shift_left(pk, 16), f32).astype(bf16)
            hi = lax.bitcast_convert_type(pk & jnp.int32(-65536), f32).astype(bf16)
            x = jnp.concatenate([lo, hi], axis=1)
            gu = jnp.dot(x, wgu_bf[...], preferred_element_type=f32) + bgu_ref[...]
            g = jnp.minimum(gu[:, :D_FF], SWIGLU_LIMIT)
            up = jnp.clip(gu[:, D_FF:], -SWIGLU_LIMIT, SWIGLU_LIMIT)
            h = (up + 1.0) * (g * jax.nn.sigmoid(SWIGLU_ALPHA * g))
            ybuf[slot] = jnp.dot(h.astype(bf16), wd_bf[...], preferred_element_type=f32) + bd_ref[...]
            y_copy(blk, slot).start(priority=ROW_DMA_PRIORITY)
            return carry

        lax.fori_loop(0, nb, body, 0)

    @pl.when(e == N_EXPERTS - 1)
    def _():
        @pl.when(total >= 2)
        def _():
            y_copy(total - 2, total % 2).wait()

        y_copy(total - 1, (total - 1) % 2).wait()


def _experts(x_slots, w_gu, b_gu, w_down, b_down, first_blk, n_blk, n_total):
    half = D_MODEL // 2
    wsel = lambda e, *_: (e, 0, 0)
    grid_spec = pltpu.PrefetchScalarGridSpec(
        num_scalar_prefetch=3,
        grid=(N_EXPERTS,),
        in_specs=[pl.BlockSpec(memory_space=pl.ANY),
                  pl.BlockSpec((None, D_MODEL, 2 * D_FF), wsel),
                  pl.BlockSpec((None, 1, 2 * D_FF), wsel),
                  pl.BlockSpec((None, D_FF, D_MODEL), wsel),
                  pl.BlockSpec((None, 1, D_MODEL), wsel)],
        out_specs=pl.BlockSpec(memory_space=pl.ANY),
        scratch_shapes=[pltpu.VMEM((D_MODEL, 2 * D_FF), bf16), pltpu.VMEM((D_FF, D_MODEL), bf16),
                        pltpu.VMEM((2, R_BLK, half), i32), pltpu.VMEM((2, R_BLK, D_MODEL), f32),
                        pltpu.SemaphoreType.DMA((2,)), pltpu.SemaphoreType.DMA((2,))],
    )
    return pl.pallas_call(
        _experts_kernel,
        grid_spec=grid_spec,
        out_shape=jax.ShapeDtypeStruct((x_slots.shape[0], D_MODEL), f32),
        compiler_params=_cparams(1),
        name="experts",
    )(first_blk, n_blk, n_total, x_slots, w_gu, b_gu.reshape(N_EXPERTS, 1, 2 * D_FF), w_down,
      b_down.reshape(N_EXPERTS, 1, D_MODEL))


def _slot_layout(counts):
    seg_rows = (counts + ROW_ALIGN - 1) // ROW_ALIGN * ROW_ALIGN
    seg_local = jnp.cumsum(seg_rows, axis=1) - seg_rows
    total = jnp.sum(seg_rows, axis=0)
    region = (total + R_BLK - 1) // R_BLK * R_BLK
    region_end = jnp.cumsum(region)
    ebase = region_end - region
    seg_slot = ebase[None, :] + jnp.cumsum(seg_rows, axis=0) - seg_rows
    flat = lambda a: a.reshape(-1).astype(i32)
    return dict(seg_slot=flat(seg_slot), seg_local=flat(seg_local), seg_rows=flat(seg_rows),
                tile_rows=flat(jnp.sum(seg_rows, axis=1)),
                tail_slot=flat(ebase + total), tail_rows=flat(region - total),
                first_blk=flat(ebase // R_BLK), n_blk=flat(region // R_BLK),
                n_total=flat(region_end[-1:] // R_BLK),
                lbase_v=seg_local.astype(i32)[:, :, None])


def kernel(x_prompt, x_sample, cache_k, cache_v, cache_conv, norm_mix, w_in, attn_sinks, conv_dw_w, conv_dw_b,
           conv_ln_g, conv_ln_b, w_pw, b_pw, w_out, norm_ffn, w_router, b_router, w_gu, b_gu, w_down, b_down,
           norm_final):
    depth = norm_mix.shape[0]
    assert depth == 1, "one layer per step"
    batch, seq, _ = x_prompt.shape
    dbatch, dseq, _ = x_sample.shape
    window = cache_k.shape[2]
    n_p, n_s = batch * seq, dbatch * dseq
    assert seq % TM_PROJ == 0 and seq % (ATTN_CHUNKS * CHUNK) == 0 and seq >= SPAN and n_s <= TM_DISP
    assert n_p % TM_DISP == 0 and TM_DISP % TM_MIX == 0 and seq % TM_MIX == 0
    l = 0
    row = lambda a: a.reshape(1, -1)

    w_in_bf = w_in[l].astype(bf16)
    tabs_p = _rope_tables(jnp.arange(seq, dtype=f32))
    tabs_s = _rope_tables(PAST_LEN + jnp.arange(dseq, dtype=f32))
    tabs_s = tuple(jnp.tile(t, (dbatch, 1)) for t in tabs_s)
    xp2 = x_prompt.reshape(n_p, D_MODEL)
    xs2 = x_sample.reshape(n_s, D_MODEL)
    q_p, kv_p, u_p, ga_p, gc_p, ut_p = _in_proj(xp2, row(norm_mix[l]), w_in_bf, tabs_p, TM_PROJ, seq // TM_PROJ, bf16)
    q_s, kv_s, u_s, ga_s, gc_s, _ = _in_proj(xs2, row(norm_mix[l]), w_in_bf, tabs_s, n_s, 1, f32)

    sinks = attn_sinks[l].astype(f32).reshape(N_KV_HEADS, 1, GROUP)
    attn_p = _attn_prompt(q_p, kv_p, jnp.repeat(sinks, CHUNK, axis=2), batch, seq)
    qpos = PAST_LEN + np.arange(dseq)
    kpos = PAST_LEN - window + np.arange(window + dseq)
    qch, kch = qpos // CHUNK, kpos // CHUNK
    mask = (kch[None, :] >= qch[:, None] - WINDOW_CHUNKS) & (kch[None, :] <= qch[:, None]) & (kpos[None, :] >= 0)
    mask_rows = jnp.asarray(np.tile(mask.astype(np.float32).T, (1, GROUP)))
    ck = cache_k[l].reshape(dbatch, window, KV_W)
    cv = cache_v[l].reshape(dbatch, window, KV_W)
    attn_s = _attn_sample(q_s, kv_s, ck, cv, mask_rows, jnp.repeat(sinks, dseq, axis=2), dbatch, dseq)

    conv_w = jnp.broadcast_to(conv_dw_w[l][:, None, :], (CONV_WIDTH, SUBLANES, CONV_CH))
    mix_w = (conv_w, row(conv_dw_b[l]), row(conv_ln_g[l]),
             row(conv_ln_b[l]), w_pw[l].astype(bf16), row(b_pw[l]), w_out[l].astype(bf16), row(norm_ffn[l]),
             w_router[l].T.astype(bf16), b_router[l].astype(f32).reshape(N_EXPERTS, 1))
    x1_p, xf_p, te_p, gt_p, cnt_p = _mix_prompt(u_p, attn_p, ga_p, gc_p, xp2, mix_w, seq // TM_MIX)
    x1_s, xf_s, te_s, gt_s, cnt_s = _mix_sample(u_s, cache_conv[l], attn_s, ga_s, gc_s, xs2, mix_w, dbatch, dseq)

    n_tiles = n_p // TM_DISP + 1
    n_all = n_tiles * TM_DISP
    n_blocks = -(-(TOP_K * n_all + (ROW_ALIGN - 1) * N_EXPERTS * n_tiles + N_EXPERTS * (R_BLK - ROW_ALIGN)) // R_BLK)
    cnt_p = cnt_p.reshape(n_p // TM_DISP, TM_DISP // TM_MIX, N_EXPERTS).sum(axis=1)
    lay = _slot_layout(jnp.concatenate([cnt_p, cnt_s[:, :, 0]], axis=0))
    top_e = jnp.concatenate([te_p, te_s], axis=1)
    gates = jnp.concatenate([gt_p, gt_s], axis=0)
    lpos, x_slots = _dispatch(xf_p, xf_s, top_e, lay["lbase_v"],
                              (lay["seg_slot"], lay["seg_local"], lay["seg_rows"], lay["tile_rows"],
                               lay["tail_slot"], lay["tail_rows"]),
                              n_blocks * R_BLK)
    y_slots = _experts(x_slots, w_gu[l], b_gu[l], w_down[l], b_down[l], lay["first_blk"], lay["n_blk"],
                       lay["n_total"])
    y_p, y_s = _combine(y_slots, lpos, gates, x1_p, x1_s, row(norm_final),
                        (lay["seg_slot"], lay["seg_local"], lay["seg_rows"], lay["tile_rows"]))

    y_prompt = y_p.reshape(batch, seq, D_MODEL)
    y_sample = y_s[:n_s].reshape(dbatch, dseq, D_MODEL)
    kv_tail = kv_p.reshape(batch, seq, 2 * KV_W)[:, seq - window:]
    new_k_p = kv_tail[:, :, :KV_W].reshape(1, batch, window, N_KV_HEADS, HEAD_DIM)
    new_v_p = kv_tail[:, :, KV_W:].reshape(1, batch, window, N_KV_HEADS, HEAD_DIM)
    tiles_per_seq = seq // TM_PROJ
    new_conv_p = ut_p.reshape(batch, tiles_per_seq, HALO, CONV_CH)[:, -1, HALO - (CONV_WIDTH - 1):][None]
    kv_s4 = kv_s.reshape(dbatch, dseq, 2, N_KV_HEADS, HEAD_DIM)
    new_k_s = jnp.concatenate([cache_k[l], kv_s4[:, :, 0]], axis=1)[:, -window:][None]
    new_v_s = jnp.concatenate([cache_v[l], kv_s4[:, :, 1]], axis=1)[:, -window:][None]
    new_conv_s = jnp.concatenate([cache_conv[l], u_s.reshape(dbatch, dseq, CONV_CH)], axis=1)[:, -(CONV_WIDTH - 1):][None]
    return (y_prompt, y_sample, new_k_p, new_v_p, new_conv_p, new_k_s, new_v_s, new_conv_s)
```

```python
import functools

import numpy as np
import jax
import jax.numpy as jnp
from jax import lax
from jax.experimental import pallas as pl
from jax.experimental.pallas import tpu as pltpu

f32 = jnp.float32
bf16 = jnp.bfloat16
i32 = jnp.int32

D_MODEL = 1024
PAST_LEN = 1024
CHUNK = 64
N_HEADS = 16
N_KV_HEADS = 2
HEAD_DIM = 64
GROUP = N_HEADS // N_KV_HEADS
ROT_DIM = HEAD_DIM // 4
ROPE_THETA = 500000.0
WINDOW = 128
WINDOW_CHUNKS = WINDOW // CHUNK
SPAN = (WINDOW_CHUNKS + 1) * CHUNK
CONV_CH = D_MODEL
CONV_WIDTH = 31
N_EXPERTS = 32
TOP_K = 4
D_FF = D_MODEL
SWIGLU_LIMIT = 7.0
SWIGLU_ALPHA = 1.702
NORM_EPS = 1e-5
NEG_INF = -1e30
Q_W = N_HEADS * HEAD_DIM
KV_W = N_KV_HEADS * HEAD_DIM
IN_COLS = Q_W + 2 * KV_W + 2 * CONV_CH + 2 * D_MODEL
Q_SCALE = HEAD_DIM ** -0.5

LANES = 128
SUBLANES = 8
VMEM_LIMIT = 56 * 1024 * 1024

HALO = 32
TM_PROJ = 1024
TM_MIX = 512
TM_DISP = 512
ATTN_CHUNKS = 16
TOKEN_COLS = SUBLANES
ROW_ALIGN = SUBLANES
R_BLK = 512
LOCAL_ROWS = -(-(TOP_K * TM_DISP + N_EXPERTS * (ROW_ALIGN - 1)) // LANES) * LANES
SEG_CHUNKS = tuple(2 ** p for p in range(int(np.log2(TM_DISP)), int(np.log2(ROW_ALIGN)) - 1, -1))
WAIT_CHUNKS = tuple(2 ** p for p in range(int(np.log2(LOCAL_ROWS)), int(np.log2(ROW_ALIGN)) - 1, -1))
TAIL_CHUNKS = tuple(c for c in SEG_CHUNKS if c < R_BLK)


def _cparams(n_axes):
    return pltpu.CompilerParams(dimension_semantics=("arbitrary",) * n_axes,
                                vmem_limit_bytes=VMEM_LIMIT)


def _rms(x, g):
    return x * lax.rsqrt(jnp.mean(x * x, axis=-1, keepdims=True) + NORM_EPS) * g


def _tokens_to_rows(a, fill):
    pad = jnp.full((LANES - a.shape[0], a.shape[1]), fill, a.dtype)
    return jnp.transpose(jnp.concatenate([a, pad], axis=0))[:, :TOKEN_COLS]


def _inproj_kernel(x_ref, g_ref, w_ref, cos_ref, sa_ref, sb_ref,
                   q_ref, kv_ref, u_ref, ga_ref, gc_ref, ut_ref):
    xn = _rms(x_ref[...], g_ref[...]).astype(bf16)
    cos = cos_ref[...]
    sa = sa_ref[...]
    sb = sb_ref[...]

    def rope(t):
        return t * cos + pltpu.roll(t, LANES - ROT_DIM // 2, 1) * sa + pltpu.roll(t, ROT_DIM // 2, 1) * sb

    def proj(c0, n):
        return jnp.dot(xn, w_ref[:, c0:c0 + n], preferred_element_type=f32)

    q = proj(0, Q_W)
    for c in range(Q_W // LANES):
        q_ref[:, c * LANES:(c + 1) * LANES] = (rope(q[:, c * LANES:(c + 1) * LANES]) * Q_SCALE).astype(bf16)
    kv = proj(Q_W, 2 * KV_W)
    kv_ref[:, :KV_W] = rope(kv[:, :KV_W])
    kv_ref[:, KV_W:] = kv[:, KV_W:]
    c0 = Q_W + 2 * KV_W
    u = proj(c0, CONV_CH) * jax.nn.sigmoid(proj(c0 + CONV_CH, CONV_CH))
    u_ref[...] = u.astype(u_ref.dtype)
    ut_ref[...] = u[u.shape[0] - HALO:, :]
    c0 += 2 * CONV_CH
    ga_ref[...] = jax.nn.sigmoid(proj(c0, D_MODEL)).astype(ga_ref.dtype)
    gc_ref[...] = jax.nn.sigmoid(proj(c0 + D_MODEL, D_MODEL)).astype(gc_ref.dtype)


def _rope_tables(pos):
    half = ROT_DIM // 2
    inv = ROPE_THETA ** (-jnp.arange(half, dtype=f32) * 2.0 / ROT_DIM)
    ang = pos[:, None] * inv[None, :]
    cos, sin = jnp.cos(ang), jnp.sin(ang)
    t = pos.shape[0]
    ones = jnp.ones((t, HEAD_DIM - ROT_DIM), f32)
    zeros = jnp.zeros((t, HEAD_DIM - ROT_DIM), f32)
    zh = jnp.zeros((t, half), f32)
    c = jnp.concatenate([cos, cos, ones], axis=1)
    a = jnp.concatenate([-sin, zh, zeros], axis=1)
    b = jnp.concatenate([zh, sin, zeros], axis=1)
    rep = LANES // HEAD_DIM
    return jnp.tile(c, (1, rep)), jnp.tile(a, (1, rep)), jnp.tile(b, (1, rep))


def _in_proj(x2d, norm_g, w_bf, tables, tm, tiles_per_seq, act_dtype):
    n = x2d.shape[0]
    nt = n // tm
    row = lambda i: (i, 0)
    const = lambda i: (0, 0)
    tab = lambda i: (i % tiles_per_seq, 0)
    return pl.pallas_call(
        _inproj_kernel,
        grid=(nt,),
        in_specs=[pl.BlockSpec((tm, D_MODEL), row),
                  pl.BlockSpec((1, D_MODEL), const),
                  pl.BlockSpec((D_MODEL, IN_COLS), const, pipeline_mode=pl.Buffered(1)),
                  pl.BlockSpec((tm, LANES), tab),
                  pl.BlockSpec((tm, LANES), tab),
                  pl.BlockSpec((tm, LANES), tab)],
        out_specs=[pl.BlockSpec((tm, Q_W), row),
                   pl.BlockSpec((tm, 2 * KV_W), row),
                   pl.BlockSpec((tm, CONV_CH), row),
                   pl.BlockSpec((tm, D_MODEL), row),
                   pl.BlockSpec((tm, D_MODEL), row),
                   pl.BlockSpec((None, HALO, CONV_CH), lambda i: (i, 0, 0))],
        out_shape=[jax.ShapeDtypeStruct((n, Q_W), bf16),
                   jax.ShapeDtypeStruct((n, 2 * KV_W), f32),
                   jax.ShapeDtypeStruct((n, CONV_CH), act_dtype),
                   jax.ShapeDtypeStruct((n, D_MODEL), act_dtype),
                   jax.ShapeDtypeStruct((n, D_MODEL), act_dtype),
                   jax.ShapeDtypeStruct((nt, HALO, CONV_CH), f32)],
        compiler_params=_cparams(1),
        name="in_proj",
    )(x2d, norm_g, w_bf, *tables)


def _attn_group(qg, kg, vg, mask_t, sink):
    s = lax.dot_general(kg, qg, (((1,), (1,)), ((), ())), preferred_element_type=f32)
    if mask_t is not None:
        s = jnp.where(mask_t, s, NEG_INF)
    m = jnp.maximum(jnp.max(s, axis=0, keepdims=True), sink)
    p = jnp.exp(s - m)
    denom = jnp.sum(p, axis=0, keepdims=True) + jnp.exp(sink - m)
    pn = (p / denom).astype(bf16)
    return lax.dot_general(pn, vg, (((0,), (0,)), ((), ())), preferred_element_type=f32)


def _heads_to_rows(q, g):
    base = g * GROUP * HEAD_DIM
    return jnp.concatenate([q[:, base + h * HEAD_DIM: base + (h + 1) * HEAD_DIM] for h in range(GROUP)], axis=0)


def _rows_to_heads(o, tq):
    return [o[h * tq:(h + 1) * tq, :] for h in range(GROUP)]


def _attn_prompt_kernel(q_ref, kv_ref, sink_ref, o_ref):
    j = pl.program_id(1)
    for c in range(ATTN_CHUNKS):
        n = j * ATTN_CHUNKS + c
        first = jnp.maximum(n - WINDOW_CHUNKS, 0)
        win = kv_ref[pl.ds(pl.multiple_of(first * CHUNK, CHUNK), SPAN), :]
        if c < WINDOW_CHUNKS:
            mask = lax.broadcasted_iota(i32, (SPAN, GROUP * CHUNK), 0) < (n - first + 1) * CHUNK
        else:
            mask = None
        q = q_ref[c * CHUNK:(c + 1) * CHUNK, :]
        outs = []
        for g in range(N_KV_HEADS):
            kg = win[:, g * HEAD_DIM:(g + 1) * HEAD_DIM].astype(bf16)
            vg = win[:, KV_W + g * HEAD_DIM: KV_W + (g + 1) * HEAD_DIM].astype(bf16)
            o = _attn_group(_heads_to_rows(q, g), kg, vg, mask, sink_ref[g])
            outs += _rows_to_heads(o, CHUNK)
        o_ref[c * CHUNK:(c + 1) * CHUNK, :] = jnp.concatenate(outs, axis=1).astype(bf16)


def _attn_prompt(q, kv, sink_rows, batch, seq):
    qrows = ATTN_CHUNKS * CHUNK
    steps = seq // qrows
    return pl.pallas_call(
        _attn_prompt_kernel,
        grid=(batch, steps),
        in_specs=[pl.BlockSpec((qrows, Q_W), lambda b, j: (b * steps + j, 0)),
                  pl.BlockSpec((seq, 2 * KV_W), lambda b, j: (b, 0)),
                  pl.BlockSpec((N_KV_HEADS, 1, GROUP * CHUNK), lambda b, j: (0, 0, 0))],
        out_specs=pl.BlockSpec((qrows, Q_W), lambda b, j: (b * steps + j, 0)),
        out_shape=jax.ShapeDtypeStruct((batch * seq, Q_W), bf16),
        compiler_params=_cparams(2),
        name="attn_prompt",
    )(q, kv, sink_rows)


def _attn_sample_kernel(q_ref, kvn_ref, ck_ref, cv_ref, mask_ref, sink_ref, o_ref):
    tq = q_ref.shape[0]
    q = q_ref[...]
    kvn = kvn_ref[...]
    ck = ck_ref[...]
    cv = cv_ref[...]
    mask = mask_ref[...] > 0.5
    outs = []
    for g in range(N_KV_HEADS):
        sl = slice(g * HEAD_DIM, (g + 1) * HEAD_DIM)
        kg = jnp.concatenate([ck[:, sl], kvn[:, sl]], axis=0).astype(bf16)
        vg = jnp.concatenate([cv[:, sl], kvn[:, KV_W + g * HEAD_DIM: KV_W + (g + 1) * HEAD_DIM]], axis=0).astype(bf16)
        o = _attn_group(_heads_to_rows(q, g), kg, vg, mask, sink_ref[g])
        outs += _rows_to_heads(o, tq)
    o_ref[...] = jnp.concatenate(outs, axis=1)


def _attn_sample(q, kv_new, cache_k, cache_v, mask_rows, sink_rows, batch, tq):
    w = cache_k.shape[1]
    return pl.pallas_call(
        _attn_sample_kernel,
        grid=(batch,),
        in_specs=[pl.BlockSpec((tq, Q_W), lambda b: (b, 0)),
                  pl.BlockSpec((tq, 2 * KV_W), lambda b: (b, 0)),
                  pl.BlockSpec((None, w, KV_W), lambda b: (b, 0, 0)),
                  pl.BlockSpec((None, w, KV_W), lambda b: (b, 0, 0)),
                  pl.BlockSpec((w + tq, GROUP * tq), lambda b: (0, 0)),
                  pl.BlockSpec((N_KV_HEADS, 1, GROUP * tq), lambda b: (0, 0, 0))],
        out_specs=pl.BlockSpec((tq, Q_W), lambda b: (b, 0)),
        out_shape=jax.ShapeDtypeStruct((batch * tq, Q_W), f32),
        compiler_params=_cparams(1),
        name="attn_sample",
    )(q, kv_new, cache_k, cache_v, mask_rows, sink_rows)


def _dwconv(s_ref, row0, n_rows, w_ref, b_ref, y_ref, yrow0, rc):
    lead = HALO - (CONV_WIDTH - 1)
    offsets = range(lead, lead + CONV_WIDTH)

    def lane_body(lc, carry):
        c0 = pl.multiple_of(lc * LANES, LANES)
        cols = pl.ds(c0, LANES)
        bias = b_ref[:, cols]
        groups = rc // SUBLANES
        for t0 in range(0, n_rows, rc):
            acc = jnp.broadcast_to(bias[None], (groups, SUBLANES, LANES))
            for r in range(SUBLANES):
                offs = [o for o in offsets if o % SUBLANES == r]
                if not offs:
                    continue
                amax = max(offs) // SUBLANES
                z = s_ref[pl.ds(row0 + t0 + r, rc + SUBLANES * amax), cols].reshape(groups + amax, SUBLANES, LANES)
                for o in offs:
                    a = o // SUBLANES
                    acc = acc + z[a: a + groups] * w_ref[o - lead, :, cols][None]
            y_ref[pl.ds(yrow0 + t0, rc), cols] = acc.reshape(rc, LANES)
        return carry

    lax.fori_loop(0, CONV_CH // LANES, lane_body, 0)


def _mix_tail(y, attn, ga, gc, x, valid, lg_ref, lb_ref, wpw_ref, bpw_ref, wout_ref, nf_ref, wr_ref, br_ref):
    tm = y.shape[0]
    mu = jnp.mean(y, axis=-1, keepdims=True)
    d = y - mu
    var = jnp.mean(d * d, axis=-1, keepdims=True)
    yn = d * lax.rsqrt(var + NORM_EPS) * lg_ref[...] + lb_ref[...]
    act = (yn * jax.nn.sigmoid(yn)).astype(bf16)
    conv_out = jnp.dot(act, wpw_ref[...], preferred_element_type=f32) + bpw_ref[...]
    h = (ga.astype(f32) * attn.astype(f32) + gc.astype(f32) * conv_out).astype(bf16)
    x1 = x + jnp.dot(h, wout_ref[...], preferred_element_type=f32)
    xf = _rms(x1, nf_ref[...]).astype(bf16)
    logits = lax.dot_general(wr_ref[...], xf, (((1,), (1,)), ((), ())), preferred_element_type=f32) + br_ref[...]
    eidx = lax.broadcasted_iota(i32, (N_EXPERTS, tm), 0)
    routed = jnp.zeros((N_EXPERTS, tm), jnp.bool_)
    top_l, top_e = [], []
    l = logits
    for _ in range(TOP_K):
        m = jnp.max(l, axis=0, keepdims=True)
        idx = jnp.min(jnp.where(l == m, eidx, N_EXPERTS), axis=0, keepdims=True)
        sel = eidx == idx
        routed = routed | sel
        l = jnp.where(sel, -jnp.inf, l)
        top_l.append(m)
        top_e.append(idx)
    ex = [jnp.exp(t - top_l[0]) for t in top_l]
    tot = ex[0] + ex[1] + ex[2] + ex[3]
    gates = jnp.concatenate([e / tot for e in ex], axis=0)
    top_e = jnp.concatenate(top_e, axis=0)
    if valid is not None:
        routed = routed & valid
        top_e = jnp.where(valid, top_e, -1)
        gates = jnp.where(valid, gates, 0.0)
    counts = jnp.sum(jnp.where(routed, 1, 0).astype(i32), axis=1, keepdims=True)
    return x1, xf, top_e, _tokens_to_rows(gates, 0.0), counts


def _mix_prompt_kernel(u_ref, halo_ref, attn_ref, ga_ref, gc_ref, x_ref, cw_ref, cb_ref, lg_ref, lb_ref,
                       wpw_ref, bpw_ref, wout_ref, nf_ref, wr_ref, br_ref,
                       x1_ref, xf_ref, te_ref, gt_ref, cnt_ref, s_scr, y_scr, *, tiles_per_seq):
    i = pl.program_id(0)
    first = (i % tiles_per_seq) == 0
    s_scr[0:HALO, :] = jnp.where(first, 0.0, halo_ref[...].astype(f32))
    s_scr[HALO:, :] = u_ref[...].astype(f32)
    _dwconv(s_scr, 0, TM_MIX, cw_ref, cb_ref, y_scr, 0, 128)
    x1, xf, top_e, gates, counts = _mix_tail(
        y_scr[...], attn_ref[...], ga_ref[...], gc_ref[...], x_ref[...], None,
        lg_ref, lb_ref, wpw_ref, bpw_ref, wout_ref, nf_ref, wr_ref, br_ref)
    x1_ref[...] = x1
    xf_ref[...] = xf
    te_ref[...] = top_e
    gt_ref[...] = gates
    cnt_ref[...] = counts


def _mix_sample_kernel(u_ref, hist_ref, attn_ref, ga_ref, gc_ref, x_ref, cw_ref, cb_ref, lg_ref, lb_ref,
                       wpw_ref, bpw_ref, wout_ref, nf_ref, wr_ref, br_ref,
                       x1_ref, xf_ref, te_ref, gt_ref, cnt_ref, s_scr, y_scr, *, batch, tq):
    n = batch * tq
    hist_rows = CONV_WIDTH - 1
    lead = HALO - hist_rows
    stride = HALO + tq
    s_scr[...] = jnp.zeros(s_scr.shape, f32)
    as_conv_input = lambda a: a.astype(bf16).astype(f32)
    for b in range(batch):
        s_scr[b * stride + lead: b * stride + HALO, :] = as_conv_input(hist_ref[b])
        s_scr[b * stride + HALO: (b + 1) * stride, :] = as_conv_input(u_ref[b * tq:(b + 1) * tq, :])
        _dwconv(s_scr, b * stride, tq, cw_ref, cb_ref, y_scr, b * tq, tq)
    x1, xf, top_e, gates, counts = _mix_tail(
        y_scr[...], attn_ref[...], ga_ref[...], gc_ref[...], x_ref[...], None,
        lg_ref, lb_ref, wpw_ref, bpw_ref, wout_ref, nf_ref, wr_ref, br_ref)
    x1_ref[...] = jnp.zeros(x1_ref.shape, f32)
    xf_ref[...] = jnp.zeros(xf_ref.shape, bf16)
    te_ref[...] = jnp.full(te_ref.shape, -1, i32)
    gt_ref[...] = jnp.zeros(gt_ref.shape, f32)
    x1_ref[0:n, :] = x1
    xf_ref[0:n, :] = xf
    te_ref[:, 0:n] = top_e
    gt_ref[0:n, :] = gates
    cnt_ref[0] = counts


def _mix_weight_specs():
    shapes = [(CONV_WIDTH, SUBLANES, CONV_CH), (1, CONV_CH), (1, CONV_CH), (1, CONV_CH), (CONV_CH, D_MODEL),
              (1, D_MODEL), (D_MODEL, D_MODEL), (1, D_MODEL), (N_EXPERTS, D_MODEL), (N_EXPERTS, 1)]
    return [pl.BlockSpec(s, functools.partial(lambda nd, i: (0,) * nd, len(s))) for s in shapes]


def _mix_out(n, nt):
    shapes = [jax.ShapeDtypeStruct((n, D_MODEL), f32), jax.ShapeDtypeStruct((n, D_MODEL), bf16),
              jax.ShapeDtypeStruct((TOP_K, n), i32), jax.ShapeDtypeStruct((n, TOKEN_COLS), f32),
              jax.ShapeDtypeStruct((nt, N_EXPERTS, 1), i32)]
    return shapes


def _mix_prompt(u, attn, ga, gc, x2d, weights, tiles_per_seq):
    n = x2d.shape[0]
    nt = n // TM_MIX
    row = lambda i: (i, 0)
    halo = lambda i: (jnp.maximum(i * (TM_MIX // HALO) - 1, 0), 0)
    tok = lambda i: (0, i)
    return pl.pallas_call(
        functools.partial(_mix_prompt_kernel, tiles_per_seq=tiles_per_seq),
        grid=(nt,),
        in_specs=[pl.BlockSpec((TM_MIX, CONV_CH), row), pl.BlockSpec((HALO, CONV_CH), halo),
                  pl.BlockSpec((TM_MIX, Q_W), row), pl.BlockSpec((TM_MIX, D_MODEL), row),
                  pl.BlockSpec((TM_MIX, D_MODEL), row), pl.BlockSpec((TM_MIX, D_MODEL), row)] + _mix_weight_specs(),
        out_specs=[pl.BlockSpec((TM_MIX, D_MODEL), row), pl.BlockSpec((TM_MIX, D_MODEL), row),
                   pl.BlockSpec((TOP_K, TM_MIX), tok), pl.BlockSpec((TM_MIX, TOKEN_COLS), row),
                   pl.BlockSpec((None, N_EXPERTS, 1), lambda i: (i, 0, 0))],
        out_shape=_mix_out(n, nt),
        scratch_shapes=[pltpu.VMEM((TM_MIX + HALO, CONV_CH), f32), pltpu.VMEM((TM_MIX, CONV_CH), f32)],
        compiler_params=_cparams(1),
        name="mix_prompt",
    )(u, u, attn, ga, gc, x2d, *weights)


def _mix_sample(u, hist, attn, ga, gc, x2d, weights, batch, tq):
    return pl.pallas_call(
        functools.partial(_mix_sample_kernel, batch=batch, tq=tq),
        out_shape=_mix_out(TM_DISP, 1),
        scratch_shapes=[pltpu.VMEM((batch * (HALO + tq), CONV_CH), f32), pltpu.VMEM((batch * tq, CONV_CH), f32)],
        compiler_params=pltpu.CompilerParams(vmem_limit_bytes=VMEM_LIMIT),
        name="mix_sample",
    )(u, hist, attn, ga, gc, x2d, *weights)


def _segment_dmas(tile, seg_rows, seg_local, seg_slot, make_copy):
    def body(e, carry):
        idx = tile * N_EXPERTS + e
        n = seg_rows[idx]
        loc = seg_local[idx]
        dst = seg_slot[idx]
        for sz in SEG_CHUNKS:
            @pl.when((n & sz) != 0)
            def _():
                off = pl.multiple_of(n & ~(2 * sz - 1), ROW_ALIGN)
                make_copy(pl.multiple_of(loc + off, ROW_ALIGN), pl.multiple_of(dst + off, ROW_ALIGN), sz).start()
        return carry

    lax.fori_loop(0, N_EXPERTS, body, 0)


def _tile_wait(n_rows, make_copy):
    for sz in WAIT_CHUNKS:
        @pl.when((n_rows & sz) != 0)
        def _():
            make_copy(0, 0, sz).wait()


def _dispatch_kernel(seg_slot, seg_local, seg_rows, tile_rows, tail_slot, tail_rows,
                     xa_ref, xb_ref, te_ref, lb_ref, lpos_ref, slots_hbm, buf, zbuf, sem, zsem,
                     *, n_prompt_tiles, n_tiles):
    i = pl.program_id(0)
    slot = i % 2
    x = jnp.where(i < n_prompt_tiles, xa_ref[...], xb_ref[...])
    te = te_ref[...]
    eidx = lax.broadcasted_iota(i32, (N_EXPERTS, TM_DISP), 0)
    hits = [te[k:k + 1, :] == eidx for k in range(TOP_K)]
    routed = hits[0] | hits[1] | hits[2] | hits[3]
    before = lax.broadcasted_iota(i32, (TM_DISP, TM_DISP), 0) < lax.broadcasted_iota(i32, (TM_DISP, TM_DISP), 1)
    rank = jnp.dot(jnp.where(routed, 1.0, 0.0).astype(bf16), jnp.where(before, 1.0, 0.0).astype(bf16),
                   preferred_element_type=f32)
    pos = lb_ref[...] + rank.astype(i32)
    lpos = []
    for k in range(TOP_K):
        p = jnp.sum(jnp.where(hits[k], pos, 0), axis=0, keepdims=True)
        lpos.append(jnp.where(te[k:k + 1, :] >= 0, p, -1))
    lpos_ref[...] = _tokens_to_rows(jnp.concatenate(lpos, axis=0), -1)
    ridx = lax.broadcasted_iota(i32, (LOCAL_ROWS, TM_DISP), 0).astype(f32)
    lf = [p.astype(f32) for p in lpos]
    onehot = ((ridx - lf[0]) * (ridx - lf[1])) * ((ridx - lf[2]) * (ridx - lf[3])) == 0.0
    srt = jnp.dot(jnp.where(onehot, 1.0, 0.0).astype(bf16), x, preferred_element_type=f32)
    half = D_MODEL // 2
    lo = lax.shift_right_logical(lax.bitcast_convert_type(srt[:, :half], i32), 16)
    hi = lax.bitcast_convert_type(srt[:, half:], i32) & jnp.int32(-65536)
    buf[slot] = hi | lo

    def copy_of(s):
        def make(loc, dst, sz):
            return pltpu.make_async_copy(buf.at[s, pl.ds(loc, sz), :], slots_hbm.at[pl.ds(dst, sz), :], sem.at[s])
        return make

    _segment_dmas(i, seg_rows, seg_local, seg_slot, copy_of(slot))

    @pl.when(i > 0)
    def _():
        _tile_wait(tile_rows[i - 1], copy_of(1 - slot))

    @pl.when(i == n_tiles - 1)
    def _():
        _tile_wait(tile_rows[i], copy_of(slot))
        zbuf[...] = jnp.zeros(zbuf.shape, i32)
        for start in (True, False):
            def body(e, carry):
                n = tail_rows[e]
                dst = tail_slot[e]
                for sz in TAIL_CHUNKS:
                    @pl.when((n & sz) != 0)
                    def _():
                        off = pl.multiple_of(n & ~(2 * sz - 1), ROW_ALIGN)
                        cp = pltpu.make_async_copy(zbuf.at[pl.ds(0, sz), :],
                                                   slots_hbm.at[pl.ds(pl.multiple_of(dst + off, ROW_ALIGN), sz), :], zsem)
                        if start:
                            cp.start()
                        else:
                            cp.wait()
                return carry
            lax.fori_loop(0, N_EXPERTS, body, 0)


def _dispatch(xf_p, xf_s, top_e, lbase_v, meta, n_slots):
    n_prompt_tiles = xf_p.shape[0] // TM_DISP
    n_tiles = n_prompt_tiles + 1
    half = D_MODEL // 2
    grid_spec = pltpu.PrefetchScalarGridSpec(
        num_scalar_prefetch=6,
        grid=(n_tiles,),
        in_specs=[pl.BlockSpec((TM_DISP, D_MODEL), lambda i, *_: (jnp.minimum(i, n_prompt_tiles - 1), 0)),
                  pl.BlockSpec((TM_DISP, D_MODEL), lambda i, *_: (0, 0)),
                  pl.BlockSpec((TOP_K, TM_DISP), lambda i, *_: (0, i)),
                  pl.BlockSpec((None, N_EXPERTS, 1), lambda i, *_: (i, 0, 0))],
        out_specs=[pl.BlockSpec((TM_DISP, TOKEN_COLS), lambda i, *_: (i, 0)),
                   pl.BlockSpec(memory_space=pl.ANY)],
        scratch_shapes=[pltpu.VMEM((2, LOCAL_ROWS, half), i32), pltpu.VMEM((max(TAIL_CHUNKS), half), i32),
                        pltpu.SemaphoreType.DMA((2,)), pltpu.SemaphoreType.DMA(())],
    )
    return pl.pallas_call(
        functools.partial(_dispatch_kernel, n_prompt_tiles=n_prompt_tiles, n_tiles=n_tiles),
        grid_spec=grid_spec,
        out_shape=[jax.ShapeDtypeStruct((n_tiles * TM_DISP, TOKEN_COLS), i32),
                   jax.ShapeDtypeStruct((n_slots, half), i32)],
        compiler_params=_cparams(1),
        name="dispatch",
    )(*meta, xf_p, xf_s, top_e, lbase_v)


def _combine_kernel(seg_slot, seg_local, seg_rows, tile_rows,
                    lpos_ref, ga_ref, gb_ref, xa_ref, xb_ref, nfin_ref, y_hbm, ya_ref, yb_ref, buf, sem,
                    *, n_prompt_tiles, n_tiles):
    i = pl.program_id(0)
    slot = i % 2

    def copy_of(s):
        def make(loc, src, sz):
            return pltpu.make_async_copy(y_hbm.at[pl.ds(src, sz), :], buf.at[s, pl.ds(loc, sz), :], sem.at[s])
        return make

    @pl.when(i == 0)
    def _():
        buf[...] = jnp.zeros(buf.shape, f32)
        _segment_dmas(0, seg_rows, seg_local, seg_slot, copy_of(0))

    @pl.when(i + 1 < n_tiles)
    def _():
        _segment_dmas(i + 1, seg_rows, seg_local, seg_slot, copy_of(1 - slot))

    _tile_wait(tile_rows[i], copy_of(slot))

    lpos = lpos_ref[...]
    gate = jnp.where(i < n_prompt_tiles, ga_ref[...], gb_ref[...])
    cidx = lax.broadcasted_iota(i32, (TM_DISP, LOCAL_ROWS), 1)
    wmat = jnp.zeros((TM_DISP, LOCAL_ROWS), f32)
    for k in range(TOP_K):
        wmat = jnp.where(cidx == lpos[:, k:k + 1], gate[:, k:k + 1], wmat)
    moe = jnp.dot(wmat.astype(bf16), buf[slot].astype(bf16), preferred_element_type=f32)
    x1 = jnp.where(i < n_prompt_tiles, xa_ref[...], xb_ref[...])
    y = _rms(x1 + moe, nfin_ref[...])

    @pl.when(i < n_prompt_tiles)
    def _():
        ya_ref[...] = y

    @pl.when(i >= n_prompt_tiles)
    def _():
        yb_ref[...] = y


def _combine(y_slots, lpos_t, gate_p, gate_s, x1_p, x1_s, norm_final, meta):
    n_prompt_tiles = x1_p.shape[0] // TM_DISP
    n_tiles = n_prompt_tiles + 1
    last = n_prompt_tiles - 1
    grid_spec = pltpu.PrefetchScalarGridSpec(
        num_scalar_prefetch=4,
        grid=(n_tiles,),
        in_specs=[pl.BlockSpec((TM_DISP, TOKEN_COLS), lambda i, *_: (i, 0)),
                  pl.BlockSpec((TM_DISP, TOKEN_COLS), lambda i, *_: (jnp.minimum(i, last), 0)),
                  pl.BlockSpec((TM_DISP, TOKEN_COLS), lambda i, *_: (0, 0)),
                  pl.BlockSpec((TM_DISP, D_MODEL), lambda i, *_: (jnp.minimum(i, last), 0)),
                  pl.BlockSpec((TM_DISP, D_MODEL), lambda i, *_: (0, 0)),
                  pl.BlockSpec((1, D_MODEL), lambda i, *_: (0, 0)),
                  pl.BlockSpec(memory_space=pl.ANY)],
        out_specs=[pl.BlockSpec((TM_DISP, D_MODEL), lambda i, *_: (jnp.minimum(i, last), 0)),
                   pl.BlockSpec((TM_DISP, D_MODEL), lambda i, *_: (0, 0))],
        scratch_shapes=[pltpu.VMEM((2, LOCAL_ROWS, D_MODEL), f32), pltpu.SemaphoreType.DMA((2,))],
    )
    return pl.pallas_call(
        functools.partial(_combine_kernel, n_prompt_tiles=n_prompt_tiles, n_tiles=n_tiles),
        grid_spec=grid_spec,
        out_shape=[jax.ShapeDtypeStruct(x1_p.shape, f32), jax.ShapeDtypeStruct(x1_s.shape, f32)],
        compiler_params=_cparams(1),
        name="combine",
    )(*meta, lpos_t, gate_p, gate_s, x1_p, x1_s, norm_final, y_slots)


ROW_DMA_PRIORITY = 1


def _experts_kernel(first_blk, n_blk, n_total, xs_hbm, wgu_ref, bgu_ref, wd_ref, bd_ref, y_hbm,
                    wgu_bf, wd_bf, xbuf, ybuf, xsem, ysem):
    e = pl.program_id(0)
    nb = n_blk[e]
    g0 = first_blk[e]
    total = n_total[0]

    def rows(g):
        return pl.ds(pl.multiple_of(g * R_BLK, R_BLK), R_BLK)

    def x_copy(g, slot):
        return pltpu.make_async_copy(xs_hbm.at[rows(g), :], xbuf.at[slot], xsem.at[slot])

    def y_copy(g, slot):
        return pltpu.make_async_copy(ybuf.at[slot], y_hbm.at[rows(g), :], ysem.at[slot])

    @pl.when(e == 0)
    def _():
        x_copy(0, 0).start(priority=ROW_DMA_PRIORITY)

    @pl.when(nb > 0)
    def _():
        wgu_bf[...] = wgu_ref[...].astype(bf16)
        wd_bf[...] = wd_ref[...].astype(bf16)

        def body(j, carry):
            blk = g0 + j
            slot = blk % 2

            @pl.when(blk + 1 < total)
            def _():
                x_copy(blk + 1, 1 - slot).start(priority=ROW_DMA_PRIORITY)

            x_copy(blk, slot).wait()

            @pl.when(blk >= 2)
            def _():
                y_copy(blk - 2, slot).wait()

            pk = xbuf[slot]
            lo = lax.bitcast_convert_type(lax.shift_left(pk, 16), f32).astype(bf16)
            hi = lax.bitcast_convert_type(pk & jnp.int32(-65536), f32).astype(bf16)
            x = jnp.concatenate([lo, hi], axis=1)
            gu = jnp.dot(x, wgu_bf[...], preferred_element_type=f32) + bgu_ref[...]
            g = jnp.minimum(gu[:, :D_FF], SWIGLU_LIMIT)
            up = jnp.clip(gu[:, D_FF:], -SWIGLU_LIMIT, SWIGLU_LIMIT)
            h = (up + 1.0) * (g * jax.nn.sigmoid(SWIGLU_ALPHA * g))
            ybuf[slot] = jnp.dot(h.astype(bf16), wd_bf[...], preferred_element_type=f32) + bd_ref[...]
            y_copy(blk, slot).start(priority=ROW_DMA_PRIORITY)
            return carry

        lax.fori_loop(0, nb, body, 0)

    @pl.when(e == N_EXPERTS - 1)
    def _():
        @pl.when(total >= 2)
        def _():
            y_copy(total - 2, total % 2).wait()

        y_copy(total - 1, (total - 1) % 2).wait()


def _experts(x_slots, w_gu, b_gu, w_down, b_down, first_blk, n_blk, n_total):
    half = D_MODEL // 2
    wsel = lambda e, *_: (e, 0, 0)
    grid_spec = pltpu.PrefetchScalarGridSpec(
        num_scalar_prefetch=3,
        grid=(N_EXPERTS,),
        in_specs=[pl.BlockSpec(memory_space=pl.ANY),
                  pl.BlockSpec((None, D_MODEL, 2 * D_FF), wsel),
                  pl.BlockSpec((None, 1, 2 * D_FF), wsel),
                  pl.BlockSpec((None, D_FF, D_MODEL), wsel),
                  pl.BlockSpec((None, 1, D_MODEL), wsel)],
        out_specs=pl.BlockSpec(memory_space=pl.ANY),
        scratch_shapes=[pltpu.VMEM((D_MODEL, 2 * D_FF), bf16), pltpu.VMEM((D_FF, D_MODEL), bf16),
                        pltpu.VMEM((2, R_BLK, half), i32), pltpu.VMEM((2, R_BLK, D_MODEL), f32),
                        pltpu.SemaphoreType.DMA((2,)), pltpu.SemaphoreType.DMA((2,))],
    )
    return pl.pallas_call(
        _experts_kernel,
        grid_spec=grid_spec,
        out_shape=jax.ShapeDtypeStruct((x_slots.shape[0], D_MODEL), f32),
        compiler_params=_cparams(1),
        name="experts",
    )(first_blk, n_blk, n_total, x_slots, w_gu, b_gu.reshape(N_EXPERTS, 1, 2 * D_FF), w_down,
      b_down.reshape(N_EXPERTS, 1, D_MODEL))


def _slot_layout(counts):
    seg_rows = (counts + ROW_ALIGN - 1) // ROW_ALIGN * ROW_ALIGN
    seg_local = jnp.cumsum(seg_rows, axis=1) - seg_rows
    total = jnp.sum(seg_rows, axis=0)
    region = (total + R_BLK - 1) // R_BLK * R_BLK
    region_end = jnp.cumsum(region)
    ebase = region_end - region
    seg_slot = ebase[None, :] + jnp.cumsum(seg_rows, axis=0) - seg_rows
    flat = lambda a: a.reshape(-1).astype(i32)
    return dict(seg_slot=flat(seg_slot), seg_local=flat(seg_local), seg_rows=flat(seg_rows),
                tile_rows=flat(jnp.sum(seg_rows, axis=1)),
                tail_slot=flat(ebase + total), tail_rows=flat(region - total),
                first_blk=flat(ebase // R_BLK), n_blk=flat(region // R_BLK),
                n_total=flat(region_end[-1:] // R_BLK),
                lbase_v=seg_local.astype(i32)[:, :, None])


def kernel(x_prompt, x_sample, cache_k, cache_v, cache_conv, norm_mix, w_in, attn_sinks, conv_dw_w, conv_dw_b,
           conv_ln_g, conv_ln_b, w_pw, b_pw, w_out, norm_ffn, w_router, b_router, w_gu, b_gu, w_down, b_down,
           norm_final):
    depth = norm_mix.shape[0]
    assert depth == 1, "one layer per step"
    batch, seq, _ = x_prompt.shape
    dbatch, dseq, _ = x_sample.shape
    window = cache_k.shape[2]
    n_p, n_s = batch * seq, dbatch * dseq
    assert seq % TM_PROJ == 0 and seq % (ATTN_CHUNKS * CHUNK) == 0 and seq >= SPAN and n_s <= TM_DISP
    assert n_p % TM_DISP == 0 and TM_DISP % TM_MIX == 0 and seq % TM_MIX == 0
    l = 0
    row = lambda a: a.reshape(1, -1)

    w_in_bf = w_in[l].astype(bf16)
    tabs_p = _rope_tables(jnp.arange(seq, dtype=f32))
    tabs_s = _rope_tables(PAST_LEN + jnp.arange(dseq, dtype=f32))
    tabs_s = tuple(jnp.tile(t, (dbatch, 1)) for t in tabs_s)
    xp2 = x_prompt.reshape(n_p, D_MODEL)
    xs2 = x_sample.reshape(n_s, D_MODEL)
    q_p, kv_p, u_p, ga_p, gc_p, ut_p = _in_proj(xp2, row(norm_mix[l]), w_in_bf, tabs_p, TM_PROJ, seq // TM_PROJ, bf16)
    q_s, kv_s, u_s, ga_s, gc_s, _ = _in_proj(xs2, row(norm_mix[l]), w_in_bf, tabs_s, n_s, 1, f32)

    sinks = attn_sinks[l].astype(f32).reshape(N_KV_HEADS, 1, GROUP)
    attn_p = _attn_prompt(q_p, kv_p, jnp.repeat(sinks, CHUNK, axis=2), batch, seq)
    qpos = PAST_LEN + np.arange(dseq)
    kpos = PAST_LEN - window + np.arange(window + dseq)
    qch, kch = qpos // CHUNK, kpos // CHUNK
    mask = (kch[None, :] >= qch[:, None] - WINDOW_CHUNKS) & (kch[None, :] <= qch[:, None]) & (kpos[None, :] >= 0)
    mask_rows = jnp.asarray(np.tile(mask.astype(np.float32).T, (1, GROUP)))
    ck = cache_k[l].reshape(dbatch, window, KV_W)
    cv = cache_v[l].reshape(dbatch, window, KV_W)
    attn_s = _attn_sample(q_s, kv_s, ck, cv, mask_rows, jnp.repeat(sinks, dseq, axis=2), dbatch, dseq)

    conv_w = jnp.broadcast_to(conv_dw_w[l][:, None, :], (CONV_WIDTH, SUBLANES, CONV_CH))
    mix_w = (conv_w, row(conv_dw_b[l]), row(conv_ln_g[l]),
             row(conv_ln_b[l]), w_pw[l].astype(bf16), row(b_pw[l]), w_out[l].astype(bf16), row(norm_ffn[l]),
             w_router[l].T.astype(bf16), b_router[l].astype(f32).reshape(N_EXPERTS, 1))
    x1_p, xf_p, te_p, gt_p, cnt_p = _mix_prompt(u_p, attn_p, ga_p, gc_p, xp2, mix_w, seq // TM_MIX)
    x1_s, xf_s, te_s, gt_s, cnt_s = _mix_sample(u_s, cache_conv[l], attn_s, ga_s, gc_s, xs2, mix_w, dbatch, dseq)

    n_tiles = n_p // TM_DISP + 1
    n_all = n_tiles * TM_DISP
    n_blocks = -(-(TOP_K * n_all + (ROW_ALIGN - 1) * N_EXPERTS * n_tiles + N_EXPERTS * (R_BLK - ROW_ALIGN)) // R_BLK)
    cnt_p = cnt_p.reshape(n_p // TM_DISP, TM_DISP // TM_MIX, N_EXPERTS).sum(axis=1)
    lay = _slot_layout(jnp.concatenate([cnt_p, cnt_s[:, :, 0]], axis=0))
    top_e = jnp.concatenate([te_p, te_s], axis=1)
    lpos, x_slots = _dispatch(xf_p, xf_s, top_e, lay["lbase_v"],
                              (lay["seg_slot"], lay["seg_local"], lay["seg_rows"], lay["tile_rows"],
                               lay["tail_slot"], lay["tail_rows"]),
                              n_blocks * R_BLK)
    y_slots = _experts(x_slots, w_gu[l], b_gu[l], w_down[l], b_down[l], lay["first_blk"], lay["n_blk"],
                       lay["n_total"])
    y_p, y_s = _combine(y_slots, lpos, gt_p, gt_s, x1_p, x1_s, row(norm_final),
                        (lay["seg_slot"], lay["seg_local"], lay["seg_rows"], lay["tile_rows"]))

    y_prompt = y_p.reshape(batch, seq, D_MODEL)
    y_sample = y_s[:n_s].reshape(dbatch, dseq, D_MODEL)
    kv_tail = kv_p.reshape(batch, seq, 2 * KV_W)[:, seq - window:]
    new_k_p = kv_tail[:, :, :KV_W].reshape(1, batch, window, N_KV_HEADS, HEAD_DIM)
    new_v_p = kv_tail[:, :, KV_W:].reshape(1, batch, window, N_KV_HEADS, HEAD_DIM)
    tiles_per_seq = seq // TM_PROJ
    new_conv_p = ut_p.reshape(batch, tiles_per_seq, HALO, CONV_CH)[:, -1, HALO - (CONV_WIDTH - 1):][None]
    kv_s4 = kv_s.reshape(dbatch, dseq, 2, N_KV_HEADS, HEAD_DIM)
    new_k_s = jnp.concatenate([cache_k[l], kv_s4[:, :, 0]], axis=1)[:, -window:][None]
    new_v_s = jnp.concatenate([cache_v[l], kv_s4[:, :, 1]], axis=1)[:, -window:][None]
    new_conv_s = jnp.concatenate([cache_conv[l], u_s.reshape(dbatch, dseq, CONV_CH)], axis=1)[:, -(CONV_WIDTH - 1):][None]
    return (y_prompt, y_sample, new_k_p, new_v_p, new_conv_p, new_k_s, new_v_s, new_conv_s)
```

```python
import functools

import numpy as np
import jax
import jax.numpy as jnp
from jax import lax
from jax.experimental import pallas as pl
from jax.experimental.pallas import tpu as pltpu

f32 = jnp.float32
bf16 = jnp.bfloat16
i32 = jnp.int32

D_MODEL = 1024
PAST_LEN = 1024
CHUNK = 64
N_HEADS = 16
N_KV_HEADS = 2
HEAD_DIM = 64
GROUP = N_HEADS // N_KV_HEADS
ROT_DIM = HEAD_DIM // 4
ROPE_THETA = 500000.0
WINDOW = 128
WINDOW_CHUNKS = WINDOW // CHUNK
SPAN = (WINDOW_CHUNKS + 1) * CHUNK
CONV_CH = D_MODEL
CONV_WIDTH = 31
N_EXPERTS = 32
TOP_K = 4
D_FF = D_MODEL
SWIGLU_LIMIT = 7.0
SWIGLU_ALPHA = 1.702
NORM_EPS = 1e-5
NEG_INF = -1e30
Q_W = N_HEADS * HEAD_DIM
KV_W = N_KV_HEADS * HEAD_DIM
IN_COLS = Q_W + 2 * KV_W + 2 * CONV_CH + 2 * D_MODEL
Q_SCALE = HEAD_DIM ** -0.5

LANES = 128
SUBLANES = 8
VMEM_LIMIT = 56 * 1024 * 1024

HALO = 32
TM_PROJ = 1024
TM_MIX = 512
TM_DISP = 512
ATTN_CHUNKS = 16
TOKEN_COLS = SUBLANES
ROW_ALIGN = SUBLANES
R_BLK = 512
LOCAL_ROWS = -(-(TOP_K * TM_DISP + N_EXPERTS * (ROW_ALIGN - 1)) // LANES) * LANES
SEG_CHUNKS = tuple(2 ** p for p in range(int(np.log2(TM_DISP)), int(np.log2(ROW_ALIGN)) - 1, -1))
SEG_SPLIT = 2 * TOP_K * TM_DISP // N_EXPERTS
SEG_UNROLL = 4
BIG_SEGMENT_FLAG = 1 << 16
WAIT_CHUNKS = tuple(2 ** p for p in range(int(np.log2(LOCAL_ROWS)), int(np.log2(ROW_ALIGN)) - 1, -1))
TAIL_CHUNKS = tuple(c for c in SEG_CHUNKS if c < R_BLK)


def _cparams(n_axes):
    return pltpu.CompilerParams(dimension_semantics=("arbitrary",) * n_axes,
                                vmem_limit_bytes=VMEM_LIMIT)


def _rms(x, g):
    return x * lax.rsqrt(jnp.mean(x * x, axis=-1, keepdims=True) + NORM_EPS) * g


def _tokens_to_rows(a, fill):
    pad = jnp.full((LANES - a.shape[0], a.shape[1]), fill, a.dtype)
    return jnp.transpose(jnp.concatenate([a, pad], axis=0))[:, :TOKEN_COLS]


def _inproj_kernel(x_ref, g_ref, w_ref, cos_ref, sa_ref, sb_ref,
                   q_ref, kv_ref, u_ref, ga_ref, gc_ref, ut_ref):
    xn = _rms(x_ref[...], g_ref[...]).astype(bf16)
    cos = cos_ref[...]
    sa = sa_ref[...]
    sb = sb_ref[...]

    def rope(t):
        return t * cos + pltpu.roll(t, LANES - ROT_DIM // 2, 1) * sa + pltpu.roll(t, ROT_DIM // 2, 1) * sb

    def proj(c0, n):
        return jnp.dot(xn, w_ref[:, c0:c0 + n], preferred_element_type=f32)

    q = proj(0, Q_W)
    for c in range(Q_W // LANES):
        q_ref[:, c * LANES:(c + 1) * LANES] = (rope(q[:, c * LANES:(c + 1) * LANES]) * Q_SCALE).astype(bf16)
    kv = proj(Q_W, 2 * KV_W)
    kv_ref[:, :KV_W] = rope(kv[:, :KV_W])
    kv_ref[:, KV_W:] = kv[:, KV_W:]
    c0 = Q_W + 2 * KV_W
    u = proj(c0, CONV_CH) * jax.nn.sigmoid(proj(c0 + CONV_CH, CONV_CH))
    u_ref[...] = u.astype(u_ref.dtype)
    ut_ref[...] = u[u.shape[0] - HALO:, :]
    c0 += 2 * CONV_CH
    ga_ref[...] = jax.nn.sigmoid(proj(c0, D_MODEL)).astype(ga_ref.dtype)
    gc_ref[...] = jax.nn.sigmoid(proj(c0 + D_MODEL, D_MODEL)).astype(gc_ref.dtype)


def _rope_tables(pos):
    half = ROT_DIM // 2
    inv = ROPE_THETA ** (-jnp.arange(half, dtype=f32) * 2.0 / ROT_DIM)
    ang = pos[:, None] * inv[None, :]
    cos, sin = jnp.cos(ang), jnp.sin(ang)
    t = pos.shape[0]
    ones = jnp.ones((t, HEAD_DIM - ROT_DIM), f32)
    zeros = jnp.zeros((t, HEAD_DIM - ROT_DIM), f32)
    zh = jnp.zeros((t, half), f32)
    c = jnp.concatenate([cos, cos, ones], axis=1)
    a = jnp.concatenate([-sin, zh, zeros], axis=1)
    b = jnp.concatenate([zh, sin, zeros], axis=1)
    rep = LANES // HEAD_DIM
    return jnp.tile(c, (1, rep)), jnp.tile(a, (1, rep)), jnp.tile(b, (1, rep))


def _in_proj(x2d, norm_g, w_bf, tables, tm, tiles_per_seq, act_dtype):
    n = x2d.shape[0]
    nt = n // tm
    row = lambda i: (i, 0)
    const = lambda i: (0, 0)
    tab = lambda i: (i % tiles_per_seq, 0)
    return pl.pallas_call(
        _inproj_kernel,
        grid=(nt,),
        in_specs=[pl.BlockSpec((tm, D_MODEL), row),
                  pl.BlockSpec((1, D_MODEL), const),
                  pl.BlockSpec((D_MODEL, IN_COLS), const, pipeline_mode=pl.Buffered(1)),
                  pl.BlockSpec((tm, LANES), tab),
                  pl.BlockSpec((tm, LANES), tab),
                  pl.BlockSpec((tm, LANES), tab)],
        out_specs=[pl.BlockSpec((tm, Q_W), row),
                   pl.BlockSpec((tm, 2 * KV_W), row),
                   pl.BlockSpec((tm, CONV_CH), row),
                   pl.BlockSpec((tm, D_MODEL), row),
                   pl.BlockSpec((tm, D_MODEL), row),
                   pl.BlockSpec((None, HALO, CONV_CH), lambda i: (i, 0, 0))],
        out_shape=[jax.ShapeDtypeStruct((n, Q_W), bf16),
                   jax.ShapeDtypeStruct((n, 2 * KV_W), f32),
                   jax.ShapeDtypeStruct((n, CONV_CH), act_dtype),
                   jax.ShapeDtypeStruct((n, D_MODEL), act_dtype),
                   jax.ShapeDtypeStruct((n, D_MODEL), act_dtype),
                   jax.ShapeDtypeStruct((nt, HALO, CONV_CH), f32)],
        compiler_params=_cparams(1),
        name="in_proj",
    )(x2d, norm_g, w_bf, *tables)


def _attn_group(qg, kg, vg, mask_t, sink):
    s = lax.dot_general(kg, qg, (((1,), (1,)), ((), ())), preferred_element_type=f32)
    if mask_t is not None:
        s = jnp.where(mask_t, s, NEG_INF)
    m = jnp.maximum(jnp.max(s, axis=0, keepdims=True), sink)
    p = jnp.exp(s - m)
    denom = jnp.sum(p, axis=0, keepdims=True) + jnp.exp(sink - m)
    pn = (p / denom).astype(bf16)
    return lax.dot_general(pn, vg, (((0,), (0,)), ((), ())), preferred_element_type=f32)


def _heads_to_rows(q, g):
    base = g * GROUP * HEAD_DIM
    return jnp.concatenate([q[:, base + h * HEAD_DIM: base + (h + 1) * HEAD_DIM] for h in range(GROUP)], axis=0)


def _rows_to_heads(o, tq):
    return [o[h * tq:(h + 1) * tq, :] for h in range(GROUP)]


def _attn_prompt_kernel(q_ref, kv_ref, sink_ref, o_ref):
    j = pl.program_id(1)
    for c in range(ATTN_CHUNKS):
        n = j * ATTN_CHUNKS + c
        first = jnp.maximum(n - WINDOW_CHUNKS, 0)
        win = kv_ref[pl.ds(pl.multiple_of(first * CHUNK, CHUNK), SPAN), :]
        if c < WINDOW_CHUNKS:
            mask = lax.broadcasted_iota(i32, (SPAN, GROUP * CHUNK), 0) < (n - first + 1) * CHUNK
        else:
            mask = None
        q = q_ref[c * CHUNK:(c + 1) * CHUNK, :]
        outs = []
        for g in range(N_KV_HEADS):
            kg = win[:, g * HEAD_DIM:(g + 1) * HEAD_DIM].astype(bf16)
            vg = win[:, KV_W + g * HEAD_DIM: KV_W + (g + 1) * HEAD_DIM].astype(bf16)
            o = _attn_group(_heads_to_rows(q, g), kg, vg, mask, sink_ref[g])
            outs += _rows_to_heads(o, CHUNK)
        o_ref[c * CHUNK:(c + 1) * CHUNK, :] = jnp.concatenate(outs, axis=1).astype(bf16)


def _attn_prompt(q, kv, sink_rows, batch, seq):
    qrows = ATTN_CHUNKS * CHUNK
    steps = seq // qrows
    return pl.pallas_call(
        _attn_prompt_kernel,
        grid=(batch, steps),
        in_specs=[pl.BlockSpec((qrows, Q_W), lambda b, j: (b * steps + j, 0)),
                  pl.BlockSpec((seq, 2 * KV_W), lambda b, j: (b, 0)),
                  pl.BlockSpec((N_KV_HEADS, 1, GROUP * CHUNK), lambda b, j: (0, 0, 0))],
        out_specs=pl.BlockSpec((qrows, Q_W), lambda b, j: (b * steps + j, 0)),
        out_shape=jax.ShapeDtypeStruct((batch * seq, Q_W), bf16),
        compiler_params=_cparams(2),
        name="attn_prompt",
    )(q, kv, sink_rows)


def _attn_sample_kernel(q_ref, kvn_ref, ck_ref, cv_ref, mask_ref, sink_ref, o_ref):
    tq = q_ref.shape[0]
    q = q_ref[...]
    kvn = kvn_ref[...]
    ck = ck_ref[...]
    cv = cv_ref[...]
    mask = mask_ref[...] > 0.5
    outs = []
    for g in range(N_KV_HEADS):
        sl = slice(g * HEAD_DIM, (g + 1) * HEAD_DIM)
        kg = jnp.concatenate([ck[:, sl], kvn[:, sl]], axis=0).astype(bf16)
        vg = jnp.concatenate([cv[:, sl], kvn[:, KV_W + g * HEAD_DIM: KV_W + (g + 1) * HEAD_DIM]], axis=0).astype(bf16)
        o = _attn_group(_heads_to_rows(q, g), kg, vg, mask, sink_ref[g])
        outs += _rows_to_heads(o, tq)
    o_ref[...] = jnp.concatenate(outs, axis=1)


def _attn_sample(q, kv_new, cache_k, cache_v, mask_rows, sink_rows, batch, tq):
    w = cache_k.shape[1]
    return pl.pallas_call(
        _attn_sample_kernel,
        grid=(batch,),
        in_specs=[pl.BlockSpec((tq, Q_W), lambda b: (b, 0)),
                  pl.BlockSpec((tq, 2 * KV_W), lambda b: (b, 0)),
                  pl.BlockSpec((None, w, KV_W), lambda b: (b, 0, 0)),
                  pl.BlockSpec((None, w, KV_W), lambda b: (b, 0, 0)),
                  pl.BlockSpec((w + tq, GROUP * tq), lambda b: (0, 0)),
                  pl.BlockSpec((N_KV_HEADS, 1, GROUP * tq), lambda b: (0, 0, 0))],
        out_specs=pl.BlockSpec((tq, Q_W), lambda b: (b, 0)),
        out_shape=jax.ShapeDtypeStruct((batch * tq, Q_W), f32),
        compiler_params=_cparams(1),
        name="attn_sample",
    )(q, kv_new, cache_k, cache_v, mask_rows, sink_rows)


def _dwconv(s_ref, row0, n_rows, w_ref, b_ref, y_ref, yrow0, rc):
    lead = HALO - (CONV_WIDTH - 1)
    offsets = range(lead, lead + CONV_WIDTH)

    def lane_body(lc, carry):
        c0 = pl.multiple_of(lc * LANES, LANES)
        cols = pl.ds(c0, LANES)
        bias = b_ref[:, cols]
        groups = rc // SUBLANES
        for t0 in range(0, n_rows, rc):
            acc = jnp.broadcast_to(bias[None], (groups, SUBLANES, LANES))
            for r in range(SUBLANES):
                offs = [o for o in offsets if o % SUBLANES == r]
                if not offs:
                    continue
                amax = max(offs) // SUBLANES
                z = s_ref[pl.ds(row0 + t0 + r, rc + SUBLANES * amax), cols].reshape(groups + amax, SUBLANES, LANES)
                for o in offs:
                    a = o // SUBLANES
                    acc = acc + z[a: a + groups] * w_ref[o - lead, :, cols][None]
            y_ref[pl.ds(yrow0 + t0, rc), cols] = acc.reshape(rc, LANES)
        return carry

    lax.fori_loop(0, CONV_CH // LANES, lane_body, 0)


def _mix_tail(y, attn, ga, gc, x, valid, lg_ref, lb_ref, wpw_ref, bpw_ref, wout_ref, nf_ref, wr_ref, br_ref):
    tm = y.shape[0]
    mu = jnp.mean(y, axis=-1, keepdims=True)
    d = y - mu
    var = jnp.mean(d * d, axis=-1, keepdims=True)
    yn = d * lax.rsqrt(var + NORM_EPS) * lg_ref[...] + lb_ref[...]
    act = (yn * jax.nn.sigmoid(yn)).astype(bf16)
    conv_out = jnp.dot(act, wpw_ref[...], preferred_element_type=f32) + bpw_ref[...]
    h = (ga.astype(f32) * attn.astype(f32) + gc.astype(f32) * conv_out).astype(bf16)
    x1 = x + jnp.dot(h, wout_ref[...], preferred_element_type=f32)
    xf = _rms(x1, nf_ref[...]).astype(bf16)
    logits = lax.dot_general(wr_ref[...], xf, (((1,), (1,)), ((), ())), preferred_element_type=f32) + br_ref[...]
    eidx = lax.broadcasted_iota(i32, (N_EXPERTS, tm), 0)
    routed = jnp.zeros((N_EXPERTS, tm), jnp.bool_)
    top_l, top_e = [], []
    l = logits
    for _ in range(TOP_K):
        m = jnp.max(l, axis=0, keepdims=True)
        idx = jnp.min(jnp.where(l == m, eidx, N_EXPERTS), axis=0, keepdims=True)
        sel = eidx == idx
        routed = routed | sel
        l = jnp.where(sel, -jnp.inf, l)
        top_l.append(m)
        top_e.append(idx)
    ex = [jnp.exp(t - top_l[0]) for t in top_l]
    tot = ex[0] + ex[1] + ex[2] + ex[3]
    gates = jnp.concatenate([e / tot for e in ex], axis=0)
    top_e = jnp.concatenate(top_e, axis=0)
    if valid is not None:
        routed = routed & valid
        top_e = jnp.where(valid, top_e, -1)
        gates = jnp.where(valid, gates, 0.0)
    counts = jnp.sum(jnp.where(routed, 1, 0).astype(i32), axis=1, keepdims=True)
    return x1, xf, top_e, _tokens_to_rows(gates, 0.0), counts


def _mix_prompt_kernel(u_ref, halo_ref, attn_ref, ga_ref, gc_ref, x_ref, cw_ref, cb_ref, lg_ref, lb_ref,
                       wpw_ref, bpw_ref, wout_ref, nf_ref, wr_ref, br_ref,
                       x1_ref, xf_ref, te_ref, gt_ref, cnt_ref, s_scr, y_scr, *, tiles_per_seq):
    i = pl.program_id(0)
    first = (i % tiles_per_seq) == 0
    s_scr[0:HALO, :] = jnp.where(first, 0.0, halo_ref[...].astype(f32))
    s_scr[HALO:, :] = u_ref[...].astype(f32)
    _dwconv(s_scr, 0, TM_MIX, cw_ref, cb_ref, y_scr, 0, 128)
    x1, xf, top_e, gates, counts = _mix_tail(
        y_scr[...], attn_ref[...], ga_ref[...], gc_ref[...], x_ref[...], None,
        lg_ref, lb_ref, wpw_ref, bpw_ref, wout_ref, nf_ref, wr_ref, br_ref)
    x1_ref[...] = x1
    xf_ref[...] = xf
    te_ref[...] = top_e
    gt_ref[...] = gates
    cnt_ref[...] = counts


def _mix_sample_kernel(u_ref, hist_ref, attn_ref, ga_ref, gc_ref, x_ref, cw_ref, cb_ref, lg_ref, lb_ref,
                       wpw_ref, bpw_ref, wout_ref, nf_ref, wr_ref, br_ref,
                       x1_ref, xf_ref, te_ref, gt_ref, cnt_ref, s_scr, y_scr, *, batch, tq):
    n = batch * tq
    hist_rows = CONV_WIDTH - 1
    lead = HALO - hist_rows
    stride = HALO + tq
    s_scr[...] = jnp.zeros(s_scr.shape, f32)
    as_conv_input = lambda a: a.astype(bf16).astype(f32)
    for b in range(batch):
        s_scr[b * stride + lead: b * stride + HALO, :] = as_conv_input(hist_ref[b])
        s_scr[b * stride + HALO: (b + 1) * stride, :] = as_conv_input(u_ref[b * tq:(b + 1) * tq, :])
        _dwconv(s_scr, b * stride, tq, cw_ref, cb_ref, y_scr, b * tq, tq)
    x1, xf, top_e, gates, counts = _mix_tail(
        y_scr[...], attn_ref[...], ga_ref[...], gc_ref[...], x_ref[...], None,
        lg_ref, lb_ref, wpw_ref, bpw_ref, wout_ref, nf_ref, wr_ref, br_ref)
    x1_ref[...] = jnp.zeros(x1_ref.shape, f32)
    xf_ref[...] = jnp.zeros(xf_ref.shape, bf16)
    te_ref[...] = jnp.full(te_ref.shape, -1, i32)
    gt_ref[...] = jnp.zeros(gt_ref.shape, f32)
    x1_ref[0:n, :] = x1
    xf_ref[0:n, :] = xf
    te_ref[:, 0:n] = top_e
    gt_ref[0:n, :] = gates
    cnt_ref[0] = counts


def _mix_weight_specs():
    shapes = [(CONV_WIDTH, SUBLANES, CONV_CH), (1, CONV_CH), (1, CONV_CH), (1, CONV_CH), (CONV_CH, D_MODEL),
              (1, D_MODEL), (D_MODEL, D_MODEL), (1, D_MODEL), (N_EXPERTS, D_MODEL), (N_EXPERTS, 1)]
    return [pl.BlockSpec(s, functools.partial(lambda nd, i: (0,) * nd, len(s))) for s in shapes]


def _mix_out(n, nt):
    shapes = [jax.ShapeDtypeStruct((n, D_MODEL), f32), jax.ShapeDtypeStruct((n, D_MODEL), bf16),
              jax.ShapeDtypeStruct((TOP_K, n), i32), jax.ShapeDtypeStruct((n, TOKEN_COLS), f32),
              jax.ShapeDtypeStruct((nt, N_EXPERTS, 1), i32)]
    return shapes


def _mix_prompt(u, attn, ga, gc, x2d, weights, tiles_per_seq):
    n = x2d.shape[0]
    nt = n // TM_MIX
    row = lambda i: (i, 0)
    halo = lambda i: (jnp.maximum(i * (TM_MIX // HALO) - 1, 0), 0)
    tok = lambda i: (0, i)
    return pl.pallas_call(
        functools.partial(_mix_prompt_kernel, tiles_per_seq=tiles_per_seq),
        grid=(nt,),
        in_specs=[pl.BlockSpec((TM_MIX, CONV_CH), row), pl.BlockSpec((HALO, CONV_CH), halo),
                  pl.BlockSpec((TM_MIX, Q_W), row), pl.BlockSpec((TM_MIX, D_MODEL), row),
                  pl.BlockSpec((TM_MIX, D_MODEL), row), pl.BlockSpec((TM_MIX, D_MODEL), row)] + _mix_weight_specs(),
        out_specs=[pl.BlockSpec((TM_MIX, D_MODEL), row), pl.BlockSpec((TM_MIX, D_MODEL), row),
                   pl.BlockSpec((TOP_K, TM_MIX), tok), pl.BlockSpec((TM_MIX, TOKEN_COLS), row),
                   pl.BlockSpec((None, N_EXPERTS, 1), lambda i: (i, 0, 0))],
        out_shape=_mix_out(n, nt),
        scratch_shapes=[pltpu.VMEM((TM_MIX + HALO, CONV_CH), f32), pltpu.VMEM((TM_MIX, CONV_CH), f32)],
        compiler_params=_cparams(1),
        name="mix_prompt",
    )(u, u, attn, ga, gc, x2d, *weights)


def _mix_sample(u, hist, attn, ga, gc, x2d, weights, batch, tq):
    return pl.pallas_call(
        functools.partial(_mix_sample_kernel, batch=batch, tq=tq),
        out_shape=_mix_out(TM_DISP, 1),
        scratch_shapes=[pltpu.VMEM((batch * (HALO + tq), CONV_CH), f32), pltpu.VMEM((batch * tq, CONV_CH), f32)],
        compiler_params=pltpu.CompilerParams(vmem_limit_bytes=VMEM_LIMIT),
        name="mix_sample",
    )(u, hist, attn, ga, gc, x2d, *weights)


def _segment_dmas(tile, seg_rows, seg_local, seg_slot, tile_rows, make_copy):
    def issue(e, sizes):
        idx = tile * N_EXPERTS + e
        n = seg_rows[idx]
        loc = seg_local[idx]
        dst = seg_slot[idx]
        for sz in sizes:
            @pl.when((n & sz) != 0)
            def _():
                off = pl.multiple_of(n & ~(2 * sz - 1), ROW_ALIGN)
                make_copy(pl.multiple_of(loc + off, ROW_ALIGN), pl.multiple_of(dst + off, ROW_ALIGN), sz).start()

    def small_pass(j, carry):
        for u in range(SEG_UNROLL):
            issue(j * SEG_UNROLL + u, [sz for sz in SEG_CHUNKS if sz < SEG_SPLIT])
        return carry

    lax.fori_loop(0, N_EXPERTS // SEG_UNROLL, small_pass, 0)

    @pl.when(tile_rows[tile] >= BIG_SEGMENT_FLAG)
    def _():
        def big_pass(e, carry):
            issue(e, [sz for sz in SEG_CHUNKS if sz >= SEG_SPLIT])
            return carry

        lax.fori_loop(0, N_EXPERTS, big_pass, 0)


def _tile_wait(n_rows, make_copy):
    for sz in WAIT_CHUNKS:
        @pl.when((n_rows & sz) != 0)
        def _():
            make_copy(0, 0, sz).wait()


def _dispatch_kernel(seg_slot, seg_local, seg_rows, tile_rows, tail_slot, tail_rows,
                     xa_ref, xb_ref, te_ref, lb_ref, lpos_ref, slots_hbm, buf, zbuf, sem, zsem,
                     *, n_prompt_tiles, n_tiles):
    i = pl.program_id(0)
    slot = i % 2
    x = jnp.where(i < n_prompt_tiles, xa_ref[...], xb_ref[...])
    te = te_ref[...]
    eidx = lax.broadcasted_iota(i32, (N_EXPERTS, TM_DISP), 0)
    hits = [te[k:k + 1, :] == eidx for k in range(TOP_K)]
    routed = hits[0] | hits[1] | hits[2] | hits[3]
    before = lax.broadcasted_iota(i32, (TM_DISP, TM_DISP), 0) < lax.broadcasted_iota(i32, (TM_DISP, TM_DISP), 1)
    rank = jnp.dot(jnp.where(routed, 1.0, 0.0).astype(bf16), jnp.where(before, 1.0, 0.0).astype(bf16),
                   preferred_element_type=f32)
    pos = lb_ref[...] + rank.astype(i32)
    lpos = []
    for k in range(TOP_K):
        p = jnp.sum(jnp.where(hits[k], pos, 0), axis=0, keepdims=True)
        lpos.append(jnp.where(te[k:k + 1, :] >= 0, p, -1))
    lpos_ref[...] = _tokens_to_rows(jnp.concatenate(lpos, axis=0), -1)
    ridx = lax.broadcasted_iota(i32, (LOCAL_ROWS, TM_DISP), 0).astype(f32)
    lf = [p.astype(f32) for p in lpos]
    onehot = ((ridx - lf[0]) * (ridx - lf[1])) * ((ridx - lf[2]) * (ridx - lf[3])) == 0.0
    srt = jnp.dot(jnp.where(onehot, 1.0, 0.0).astype(bf16), x, preferred_element_type=f32)
    half = D_MODEL // 2
    lo = lax.shift_right_logical(lax.bitcast_convert_type(srt[:, :half], i32), 16)
    hi = lax.bitcast_convert_type(srt[:, half:], i32) & jnp.int32(-65536)
    buf[slot] = hi | lo

    def copy_of(s):
        def make(loc, dst, sz):
            return pltpu.make_async_copy(buf.at[s, pl.ds(loc, sz), :], slots_hbm.at[pl.ds(dst, sz), :], sem.at[s])
        return make

    _segment_dmas(i, seg_rows, seg_local, seg_slot, tile_rows, copy_of(slot))

    @pl.when(i > 0)
    def _():
        _tile_wait(tile_rows[i - 1], copy_of(1 - slot))

    @pl.when(i == n_tiles - 1)
    def _():
        _tile_wait(tile_rows[i], copy_of(slot))
        zbuf[...] = jnp.zeros(zbuf.shape, i32)
        for start in (True, False):
            def body(e, carry):
                n = tail_rows[e]
                dst = tail_slot[e]
                for sz in TAIL_CHUNKS:
                    @pl.when((n & sz) != 0)
                    def _():
                        off = pl.multiple_of(n & ~(2 * sz - 1), ROW_ALIGN)
                        cp = pltpu.make_async_copy(zbuf.at[pl.ds(0, sz), :],
                                                   slots_hbm.at[pl.ds(pl.multiple_of(dst + off, ROW_ALIGN), sz), :], zsem)
                        if start:
                            cp.start()
                        else:
                            cp.wait()
                return carry
            lax.fori_loop(0, N_EXPERTS, body, 0)


def _dispatch(xf_p, xf_s, top_e, lbase_v, meta, n_slots):
    n_prompt_tiles = xf_p.shape[0] // TM_DISP
    n_tiles = n_prompt_tiles + 1
    half = D_MODEL // 2
    grid_spec = pltpu.PrefetchScalarGridSpec(
        num_scalar_prefetch=6,
        grid=(n_tiles,),
        in_specs=[pl.BlockSpec((TM_DISP, D_MODEL), lambda i, *_: (jnp.minimum(i, n_prompt_tiles - 1), 0)),
                  pl.BlockSpec((TM_DISP, D_MODEL), lambda i, *_: (0, 0)),
                  pl.BlockSpec((TOP_K, TM_DISP), lambda i, *_: (0, i)),
                  pl.BlockSpec((None, N_EXPERTS, 1), lambda i, *_: (i, 0, 0))],
        out_specs=[pl.BlockSpec((TM_DISP, TOKEN_COLS), lambda i, *_: (i, 0)),
                   pl.BlockSpec(memory_space=pl.ANY)],
        scratch_shapes=[pltpu.VMEM((2, LOCAL_ROWS, half), i32), pltpu.VMEM((max(TAIL_CHUNKS), half), i32),
                        pltpu.SemaphoreType.DMA((2,)), pltpu.SemaphoreType.DMA(())],
    )
    return pl.pallas_call(
        functools.partial(_dispatch_kernel, n_prompt_tiles=n_prompt_tiles, n_tiles=n_tiles),
        grid_spec=grid_spec,
        out_shape=[jax.ShapeDtypeStruct((n_tiles * TM_DISP, TOKEN_COLS), i32),
                   jax.ShapeDtypeStruct((n_slots, half), i32)],
        compiler_params=_cparams(1),
        name="dispatch",
    )(*meta, xf_p, xf_s, top_e, lbase_v)


def _combine_kernel(seg_slot, seg_local, seg_rows, tile_rows,
                    lpos_ref, ga_ref, gb_ref, xa_ref, xb_ref, nfin_ref, y_hbm, ya_ref, yb_ref, buf, sem,
                    *, n_prompt_tiles, n_tiles):
    i = pl.program_id(0)
    slot = i % 2

    def copy_of(s):
        def make(loc, src, sz):
            return pltpu.make_async_copy(y_hbm.at[pl.ds(src, sz), :], buf.at[s, pl.ds(loc, sz), :], sem.at[s])
        return make

    @pl.when(i == 0)
    def _():
        buf[...] = jnp.zeros(buf.shape, f32)
        _segment_dmas(0, seg_rows, seg_local, seg_slot, tile_rows, copy_of(0))

    @pl.when(i + 1 < n_tiles)
    def _():
        _segment_dmas(i + 1, seg_rows, seg_local, seg_slot, tile_rows, copy_of(1 - slot))

    _tile_wait(tile_rows[i], copy_of(slot))

    lpos = lpos_ref[...]
    gate = jnp.where(i < n_prompt_tiles, ga_ref[...], gb_ref[...])
    cidx = lax.broadcasted_iota(i32, (TM_DISP, LOCAL_ROWS), 1)
    wmat = jnp.zeros((TM_DISP, LOCAL_ROWS), f32)
    for k in range(TOP_K):
        wmat = jnp.where(cidx == lpos[:, k:k + 1], gate[:, k:k + 1], wmat)
    moe = jnp.dot(wmat.astype(bf16), buf[slot].astype(bf16), preferred_element_type=f32)
    x1 = jnp.where(i < n_prompt_tiles, xa_ref[...], xb_ref[...])
    y = _rms(x1 + moe, nfin_ref[...])

    @pl.when(i < n_prompt_tiles)
    def _():
        ya_ref[...] = y

    @pl.when(i >= n_prompt_tiles)
    def _():
        yb_ref[...] = y


def _combine(y_slots, lpos_t, gate_p, gate_s, x1_p, x1_s, norm_final, meta):
    n_prompt_tiles = x1_p.shape[0] // TM_DISP
    n_tiles = n_prompt_tiles + 1
    last = n_prompt_tiles - 1
    grid_spec = pltpu.PrefetchScalarGridSpec(
        num_scalar_prefetch=4,
        grid=(n_tiles,),
        in_specs=[pl.BlockSpec((TM_DISP, TOKEN_COLS), lambda i, *_: (i, 0)),
                  pl.BlockSpec((TM_DISP, TOKEN_COLS), lambda i, *_: (jnp.minimum(i, last), 0)),
                  pl.BlockSpec((TM_DISP, TOKEN_COLS), lambda i, *_: (0, 0)),
                  pl.BlockSpec((TM_DISP, D_MODEL), lambda i, *_: (jnp.minimum(i, last), 0)),
                  pl.BlockSpec((TM_DISP, D_MODEL), lambda i, *_: (0, 0)),
                  pl.BlockSpec((1, D_MODEL), lambda i, *_: (0, 0)),
                  pl.BlockSpec(memory_space=pl.ANY)],
        out_specs=[pl.BlockSpec((TM_DISP, D_MODEL), lambda i, *_: (jnp.minimum(i, last), 0)),
                   pl.BlockSpec((TM_DISP, D_MODEL), lambda i, *_: (0, 0))],
        scratch_shapes=[pltpu.VMEM((2, LOCAL_ROWS, D_MODEL), f32), pltpu.SemaphoreType.DMA((2,))],
    )
    return pl.pallas_call(
        functools.partial(_combine_kernel, n_prompt_tiles=n_prompt_tiles, n_tiles=n_tiles),
        grid_spec=grid_spec,
        out_shape=[jax.ShapeDtypeStruct(x1_p.shape, f32), jax.ShapeDtypeStruct(x1_s.shape, f32)],
        compiler_params=_cparams(1),
        name="combine",
    )(*meta, lpos_t, gate_p, gate_s, x1_p, x1_s, norm_final, y_slots)


ROW_DMA_PRIORITY = 1


def _experts_kernel(first_blk, n_blk, n_total, xs_hbm, wgu_ref, bgu_ref, wd_ref, bd_ref, y_hbm,
                    wgu_bf, wd_bf, xbuf, ybuf, xsem, ysem):
    e = pl.program_id(0)
    nb = n_blk[e]
    g0 = first_blk[e]
    total = n_total[0]

    def rows(g):
        return pl.ds(pl.multiple_of(g * R_BLK, R_BLK), R_BLK)

    def x_copy(g, slot):
        return pltpu.make_async_copy(xs_hbm.at[rows(g), :], xbuf.at[slot], xsem.at[slot])

    def y_copy(g, slot):
        return pltpu.make_async_copy(ybuf.at[slot], y_hbm.at[rows(g), :], ysem.at[slot])

    @pl.when(e == 0)
    def _():
        x_copy(0, 0).start(priority=ROW_DMA_PRIORITY)

    @pl.when(nb > 0)
    def _():
        wgu_bf[...] = wgu_ref[...].astype(bf16)
        wd_bf[...] = wd_ref[...].astype(bf16)

        def body(j, carry):
            blk = g0 + j
            slot = blk % 2

            @pl.when(blk + 1 < total)
            def _():
                x_copy(blk + 1, 1 - slot).start(priority=ROW_DMA_PRIORITY)

            x_copy(blk, slot).wait()

            @pl.when(blk >= 2)
            def _():
                y_copy(blk - 2, slot).wait()

            pk = xbuf[slot]
            lo = lax.bitcast_convert_type(lax.shift_left(pk, 16), f32).astype(bf16)
            hi = lax.bitcast_convert_type(pk & jnp.int32(-65536), f32).astype(bf16)
            x = jnp.concatenate([lo, hi], axis=1)
            gu = jnp.dot(x, wgu_bf[...], preferred_element_type=f32) + bgu_ref[...]
            g = jnp.minimum(gu[:, :D_FF], SWIGLU_LIMIT)
            up = jnp.clip(gu[:, D_FF:], -SWIGLU_LIMIT, SWIGLU_LIMIT)
            h = (up + 1.0) * (g * jax.nn.sigmoid(SWIGLU_ALPHA * g))
            ybuf[slot] = jnp.dot(h.astype(bf16), wd_bf[...], preferred_element_type=f32) + bd_ref[...]
            y_copy(blk, slot).start(priority=ROW_DMA_PRIORITY)
            return carry

        lax.fori_loop(0, nb, body, 0)

    @pl.when(e == N_EXPERTS - 1)
    def _():
        @pl.when(total >= 2)
        def _():
            y_copy(total - 2, total % 2).wait()

        y_copy(total - 1, (total - 1) % 2).wait()


def _experts(x_slots, w_gu, b_gu, w_down, b_down, first_blk, n_blk, n_total):
    half = D_MODEL // 2
    wsel = lambda e, *_: (e, 0, 0)
    grid_spec = pltpu.PrefetchScalarGridSpec(
        num_scalar_prefetch=3,
        grid=(N_EXPERTS,),
        in_specs=[pl.BlockSpec(memory_space=pl.ANY),
                  pl.BlockSpec((None, D_MODEL, 2 * D_FF), wsel),
                  pl.BlockSpec((None, 1, 2 * D_FF), wsel),
                  pl.BlockSpec((None, D_FF, D_MODEL), wsel),
                  pl.BlockSpec((None, 1, D_MODEL), wsel)],
        out_specs=pl.BlockSpec(memory_space=pl.ANY),
        scratch_shapes=[pltpu.VMEM((D_MODEL, 2 * D_FF), bf16), pltpu.VMEM((D_FF, D_MODEL), bf16),
                        pltpu.VMEM((2, R_BLK, half), i32), pltpu.VMEM((2, R_BLK, D_MODEL), f32),
                        pltpu.SemaphoreType.DMA((2,)), pltpu.SemaphoreType.DMA((2,))],
    )
    return pl.pallas_call(
        _experts_kernel,
        grid_spec=grid_spec,
        out_shape=jax.ShapeDtypeStruct((x_slots.shape[0], D_MODEL), f32),
        compiler_params=_cparams(1),
        name="experts",
    )(first_blk, n_blk, n_total, x_slots, w_gu, b_gu.reshape(N_EXPERTS, 1, 2 * D_FF), w_down,
      b_down.reshape(N_EXPERTS, 1, D_MODEL))


def _slot_layout(counts):
    seg_rows = (counts + ROW_ALIGN - 1) // ROW_ALIGN * ROW_ALIGN
    seg_local = jnp.cumsum(seg_rows, axis=1) - seg_rows
    total = jnp.sum(seg_rows, axis=0)
    region = (total + R_BLK - 1) // R_BLK * R_BLK
    region_end = jnp.cumsum(region)
    ebase = region_end - region
    seg_slot = ebase[None, :] + jnp.cumsum(seg_rows, axis=0) - seg_rows
    flat = lambda a: a.reshape(-1).astype(i32)
    return dict(seg_slot=flat(seg_slot), seg_local=flat(seg_local), seg_rows=flat(seg_rows),
                tile_rows=flat(jnp.sum(seg_rows, axis=1)
                               + BIG_SEGMENT_FLAG * jnp.any(seg_rows >= SEG_SPLIT, axis=1).astype(i32)),
                tail_slot=flat(ebase + total), tail_rows=flat(region - total),
                first_blk=flat(ebase // R_BLK), n_blk=flat(region // R_BLK),
                n_total=flat(region_end[-1:] // R_BLK),
                lbase_v=seg_local.astype(i32)[:, :, None])


def kernel(x_prompt, x_sample, cache_k, cache_v, cache_conv, norm_mix, w_in, attn_sinks, conv_dw_w, conv_dw_b,
           conv_ln_g, conv_ln_b, w_pw, b_pw, w_out, norm_ffn, w_router, b_router, w_gu, b_gu, w_down, b_down,
           norm_final):
    depth = norm_mix.shape[0]
    assert depth == 1, "one layer per step"
    batch, seq, _ = x_prompt.shape
    dbatch, dseq, _ = x_sample.shape
    window = cache_k.shape[2]
    n_p, n_s = batch * seq, dbatch * dseq
    assert seq % TM_PROJ == 0 and seq % (ATTN_CHUNKS * CHUNK) == 0 and seq >= SPAN and n_s <= TM_DISP
    assert n_p % TM_DISP == 0 and TM_DISP % TM_MIX == 0 and seq % TM_MIX == 0
    l = 0
    row = lambda a: a.reshape(1, -1)

    w_in_bf = w_in[l].astype(bf16)
    tabs_p = _rope_tables(jnp.arange(seq, dtype=f32))
    tabs_s = _rope_tables(PAST_LEN + jnp.arange(dseq, dtype=f32))
    tabs_s = tuple(jnp.tile(t, (dbatch, 1)) for t in tabs_s)
    xp2 = x_prompt.reshape(n_p, D_MODEL)
    xs2 = x_sample.reshape(n_s, D_MODEL)
    q_p, kv_p, u_p, ga_p, gc_p, ut_p = _in_proj(xp2, row(norm_mix[l]), w_in_bf, tabs_p, TM_PROJ, seq // TM_PROJ, bf16)
    q_s, kv_s, u_s, ga_s, gc_s, _ = _in_proj(xs2, row(norm_mix[l]), w_in_bf, tabs_s, n_s, 1, f32)

    sinks = attn_sinks[l].astype(f32).reshape(N_KV_HEADS, 1, GROUP)
    attn_p = _attn_prompt(q_p, kv_p, jnp.repeat(sinks, CHUNK, axis=2), batch, seq)
    qpos = PAST_LEN + np.arange(dseq)
    kpos = PAST_LEN - window + np.arange(window + dseq)
    qch, kch = qpos // CHUNK, kpos // CHUNK
    mask = (kch[None, :] >= qch[:, None] - WINDOW_CHUNKS) & (kch[None, :] <= qch[:, None]) & (kpos[None, :] >= 0)
    mask_rows = jnp.asarray(np.tile(mask.astype(np.float32).T, (1, GROUP)))
    ck = cache_k[l].reshape(dbatch, window, KV_W)
    cv = cache_v[l].reshape(dbatch, window, KV_W)
    attn_s = _attn_sample(q_s, kv_s, ck, cv, mask_rows, jnp.repeat(sinks, dseq, axis=2), dbatch, dseq)

    conv_w = jnp.broadcast_to(conv_dw_w[l][:, None, :], (CONV_WIDTH, SUBLANES, CONV_CH))
    mix_w = (conv_w, row(conv_dw_b[l]), row(conv_ln_g[l]),
             row(conv_ln_b[l]), w_pw[l].astype(bf16), row(b_pw[l]), w_out[l].astype(bf16), row(norm_ffn[l]),
             w_router[l].T.astype(bf16), b_router[l].astype(f32).reshape(N_EXPERTS, 1))
    x1_p, xf_p, te_p, gt_p, cnt_p = _mix_prompt(u_p, attn_p, ga_p, gc_p, xp2, mix_w, seq // TM_MIX)
    x1_s, xf_s, te_s, gt_s, cnt_s = _mix_sample(u_s, cache_conv[l], attn_s, ga_s, gc_s, xs2, mix_w, dbatch, dseq)

    n_tiles = n_p // TM_DISP + 1
    n_all = n_tiles * TM_DISP
    n_blocks = -(-(TOP_K * n_all + (ROW_ALIGN - 1) * N_EXPERTS * n_tiles + N_EXPERTS * (R_BLK - ROW_ALIGN)) // R_BLK)
    cnt_p = cnt_p.reshape(n_p // TM_DISP, TM_DISP // TM_MIX, N_EXPERTS).sum(axis=1)
    lay = _slot_layout(jnp.concatenate([cnt_p, cnt_s[:, :, 0]], axis=0))
    top_e = jnp.concatenate([te_p, te_s], axis=1)
    lpos, x_slots = _dispatch(xf_p, xf_s, top_e, lay["lbase_v"],
                              (lay["seg_slot"], lay["seg_local"], lay["seg_rows"], lay["tile_rows"],
                               lay["tail_slot"], lay["tail_rows"]),
                              n_blocks * R_BLK)
    y_slots = _experts(x_slots, w_gu[l], b_gu[l], w_down[l], b_down[l], lay["first_blk"], lay["n_blk"],
                       lay["n_total"])
    y_p, y_s = _combine(y_slots, lpos, gt_p, gt_s, x1_p, x1_s, row(norm_final),
                        (lay["seg_slot"], lay["seg_local"], lay["seg_rows"], lay["tile_rows"]))

    y_prompt = y_p.reshape(batch, seq, D_MODEL)
    y_sample = y_s[:n_s].reshape(dbatch, dseq, D_MODEL)
    kv_tail = kv_p.reshape(batch, seq, 2 * KV_W)[:, seq - window:]
    new_k_p = kv_tail[:, :, :KV_W].reshape(1, batch, window, N_KV_HEADS, HEAD_DIM)
    new_v_p = kv_tail[:, :, KV_W:].reshape(1, batch, window, N_KV_HEADS, HEAD_DIM)
    tiles_per_seq = seq // TM_PROJ
    new_conv_p = ut_p.reshape(batch, tiles_per_seq, HALO, CONV_CH)[:, -1, HALO - (CONV_WIDTH - 1):][None]
    kv_s4 = kv_s.reshape(dbatch, dseq, 2, N_KV_HEADS, HEAD_DIM)
    new_k_s = jnp.concatenate([cache_k[l], kv_s4[:, :, 0]], axis=1)[:, -window:][None]
    new_v_s = jnp.concatenate([cache_v[l], kv_s4[:, :, 1]], axis=1)[:, -window:][None]
    new_conv_s = jnp.concatenate([cache_conv[l], u_s.reshape(dbatch, dseq, CONV_CH)], axis=1)[:, -(CONV_WIDTH - 1):][None]
    return (y_prompt, y_sample, new_k_p, new_v_p, new_conv_p, new_k_s, new_v_s, new_conv_s)
```
